```python
import math
import jax, jax.numpy as jnp
from jax import lax
import numpy as np

D_MODEL = 2048
BATCH = 4
SEQ = 2048
DEPTH = 2

N_A_LAYERS = DEPTH // 2
N_B_LAYERS = DEPTH - N_A_LAYERS
N_DENSE_LAYERS = (DEPTH + 1) // 2
N_MOE_LAYERS = DEPTH // 2
EPS = 1e-6

CHUNK = 128
GMLP_HALF = 2 * D_MODEL
GMLP_GROUPS = 8
GMLP_GROUP_CH = GMLP_HALF // GMLP_GROUPS

HEAD_DIM = 128
N_KV_HEADS = D_MODEL // HEAD_DIM
KV_WIDTH = N_KV_HEADS * HEAD_DIM
DILATED_GROUPS = ((128, 1), (512, 4), (2048, 16))
N_DIL_GROUPS = len(DILATED_GROUPS)
N_Q_HEADS = N_DIL_GROUPS * N_KV_HEADS
BLOCK = 128
NEG = -1e30

D_FF = 7168
N_EXPERTS = 8
TOP_K = 2
MOE_BLOCK = 256

kernel_name = "yoco_gmlp_dilated_moe_hybrid"


def _rmsnorm(x, g):
    xf = x.astype(jnp.float32)
    y = xf * lax.rsqrt(jnp.mean(xf * xf, axis=-1, keepdims=True) + EPS)
    return (y * g.astype(jnp.float32)).astype(x.dtype)


def _swiglu(x, w_gu, w_down):
    g, u = jnp.split(x @ w_gu, 2, axis=-1)
    return (jax.nn.silu(g) * u) @ w_down


def _gmlp_mixer(h, w_in, b_in, g_v, w_s, b_s, w_out):
    B, S, _ = h.shape
    z = jax.nn.gelu(h @ w_in + b_in)
    u, v = jnp.split(z, 2, axis=-1)
    v = _rmsnorm(v, g_v).reshape(B, S // CHUNK, CHUNK, GMLP_GROUPS, GMLP_GROUP_CH)
    mask = jnp.tril(jnp.ones((CHUNK, CHUNK), w_s.dtype))
    ws = w_s * mask
    mixed = jnp.einsum('gts,bcsgd->bctgd', ws, v) + b_s.T[:, :, None]
    return (u * mixed.reshape(B, S, GMLP_HALF)) @ w_out


def _to_sub(x, d):
    B, S, H, E = x.shape
    return x.reshape(B, S // d, d, H, E).transpose(0, 2, 3, 1, 4)


def _dilated_branch(q, k, v, window, dilation, base_slopes):
    B, S, H, E = q.shape
    d = dilation
    steps_max = window // dilation
    L = S // d
    nb = -(-L // BLOCK)
    Lp = nb * BLOCK
    qs = jnp.pad(_to_sub(q, d), ((0, 0), (0, 0), (0, 0), (0, Lp - L), (0, 0)))
    qb = qs.reshape(B, d, H, nb, BLOCK, E)

    def banded(x):
        xp = jnp.pad(_to_sub(x, d), ((0, 0), (0, 0), (0, 0), (BLOCK, Lp - L), (0, 0)))
        xp = xp.reshape(B, d, H, nb + 1, BLOCK, E)
        return jnp.concatenate([xp[:, :, :, :-1], xp[:, :, :, 1:]], axis=4)

    kb = banded(k).astype(jnp.float32)
    vb = banded(v).astype(jnp.float32)
    s = jnp.einsum('bdhnqe,bdhnke->bdhnqk', qb.astype(jnp.float32), kb) * (E ** -0.5)
    j = jnp.arange(BLOCK)[:, None]
    kk = jnp.arange(2 * BLOCK)[None, :]
    steps = BLOCK + j - kk
    key_idx = jnp.arange(nb)[:, None, None] * BLOCK + kk[None] - BLOCK
    valid = (steps >= 0) & (steps <= steps_max) & (key_idx >= 0)
    slopes = base_slopes / dilation
    dist = (steps * dilation).astype(jnp.float32)
    s = jnp.where(valid, s - slopes[:, None, None, None] * dist, NEG)
    m = jnp.max(s, axis=-1, keepdims=True)
    p = jnp.exp(s - m)
    den = jnp.sum(p, axis=-1, keepdims=True)
    o = jnp.einsum('bdhnqk,bdhnke->bdhnqe', p, vb) / den
    lse = (m + jnp.log(den))[..., 0]
    o = o.reshape(B, d, H, Lp, E)[:, :, :, :L].transpose(0, 3, 1, 2, 4).reshape(B, S, H, E)
    lse = lse.reshape(B, d, H, Lp)[:, :, :, :L].transpose(0, 3, 1, 2).reshape(B, S, H)
    return o, lse


def _dilated_mixer(h, k, v, w_q, w_o):
    B, S, _ = h.shape
    q = (h @ w_q).reshape(B, S, N_DIL_GROUPS, N_KV_HEADS, HEAD_DIM)
    base = jnp.exp2(-8.0 * jnp.arange(1, N_KV_HEADS + 1, dtype=jnp.float32) / N_KV_HEADS)
    outs, lses = [], []
    for gi, (window, dilation) in enumerate(DILATED_GROUPS):
        o, lse = _dilated_branch(q[:, :, gi], k, v, window, dilation, base)
        outs.append(o)
        lses.append(lse)
    alpha = jax.nn.softmax(jnp.stack(lses, axis=0), axis=0)
    o = jnp.sum(alpha[..., None] * jnp.stack(outs, axis=0), axis=0)
    return o.reshape(B, S, KV_WIDTH).astype(h.dtype) @ w_o


def _moe_ffn(h, w_router, w_gu, w_down):
    B, S, D = h.shape
    N = B * S
    xt = h.reshape(N, D)
    logits = (xt @ w_router).astype(jnp.float32)
    top_vals, top_idx = lax.top_k(logits, TOP_K)
    gates = jax.nn.softmax(top_vals, axis=-1)
    n_assign = N * TOP_K
    exp_flat = top_idx.reshape(-1)
    tok_flat = jnp.repeat(jnp.arange(N), TOP_K)
    gate_flat = gates.reshape(-1)
    order = jnp.argsort(exp_flat)
    e_sorted = exp_flat[order]
    tok_sorted = tok_flat[order]
    gate_sorted = gate_flat[order]
    counts = jnp.bincount(exp_flat, length=N_EXPERTS)
    starts = jnp.cumsum(counts) - counts
    padded = (counts + MOE_BLOCK - 1) // MOE_BLOCK * MOE_BLOCK
    pend = jnp.cumsum(padded)
    pstarts = pend - padded
    dest = pstarts[e_sorted] + (jnp.arange(n_assign) - starts[e_sorted])
    n_blocks = -(-n_assign // MOE_BLOCK) + N_EXPERTS
    xs = jnp.zeros((n_blocks * MOE_BLOCK, D), h.dtype).at[dest].set(xt[tok_sorted])
    block_expert = jnp.clip(
        jnp.searchsorted(pend, jnp.arange(n_blocks) * MOE_BLOCK, side='right'), 0, N_EXPERTS - 1)

    def expert_block(args):
        xb, e = args
        return _swiglu(xb, w_gu[e], w_down[e])

    ys = lax.map(expert_block, (xs.reshape(n_blocks, MOE_BLOCK, D), block_expert))
    contrib = ys.reshape(-1, D)[dest] * gate_sorted[:, None].astype(h.dtype)
    out = jnp.zeros((N, D), h.dtype).at[tok_sorted].add(contrib)
    return out.reshape(B, S, D)


def setup_inputs(seed: int = 0) -> dict:
    key = jax.random.key(seed)
    ks = jax.random.split(key, 18)

    def nrm(k, shape, scale):
        return jax.random.normal(k, shape, jnp.float32) * scale

    D = D_MODEL
    return {
        "x": nrm(ks[0], (BATCH, SEQ, D), 1.0),
        "norm_gains": 1.0 + nrm(ks[1], (DEPTH, 4, D), 0.05),
        "a_w_in": nrm(ks[2], (N_A_LAYERS, D, 2 * GMLP_HALF), D ** -0.5),
        "a_b_in": nrm(ks[3], (N_A_LAYERS, 2 * GMLP_HALF), 0.02),
        "a_norm_v": 1.0 + nrm(ks[4], (N_A_LAYERS, GMLP_HALF), 0.05),
        "a_w_s": nrm(ks[5], (N_A_LAYERS, GMLP_GROUPS, CHUNK, CHUNK), CHUNK ** -0.5),
        "a_b_s": 1.0 + nrm(ks[6], (N_A_LAYERS, GMLP_GROUPS, CHUNK), 0.05),
        "a_w_out": nrm(ks[7], (N_A_LAYERS, GMLP_HALF, D), GMLP_HALF ** -0.5),
        "kv_norm": 1.0 + nrm(ks[8], (D,), 0.05),
        "w_kv": nrm(ks[9], (D, 2 * KV_WIDTH), D ** -0.5),
        "b_w_q": nrm(ks[10], (N_B_LAYERS, D, N_Q_HEADS * HEAD_DIM), D ** -0.5),
        "b_w_o": nrm(ks[11], (N_B_LAYERS, KV_WIDTH, D), KV_WIDTH ** -0.5),
        "ffn_w_gu": nrm(ks[12], (N_DENSE_LAYERS, D, 2 * D_FF), D ** -0.5),
        "ffn_w_down": nrm(ks[13], (N_DENSE_LAYERS, D_FF, D), D_FF ** -0.5),
        "moe_router": nrm(ks[14], (N_MOE_LAYERS, D, N_EXPERTS), D ** -0.5),
        "moe_w_gu": nrm(ks[15], (N_MOE_LAYERS, N_EXPERTS, D, 2 * D_FF), D ** -0.5),
        "moe_w_down": nrm(ks[16], (N_MOE_LAYERS, N_EXPERTS, D_FF, D), D_FF ** -0.5),
    }


def reference(x, norm_gains, a_w_in, a_b_in, a_norm_v, a_w_s, a_b_s, a_w_out,
              kv_norm, w_kv, b_w_q, b_w_o, ffn_w_gu, ffn_w_down,
              moe_router, moe_w_gu, moe_w_down):
    B, S, _ = x.shape
    h = x
    k_shared = v_shared = None
    for layer in range(DEPTH):
        g = norm_gains[layer]
        if layer < N_A_LAYERS:
            i = layer
            mix = _gmlp_mixer(_rmsnorm(h, g[0]), a_w_in[i], a_b_in[i], a_norm_v[i],
                              a_w_s[i], a_b_s[i], a_w_out[i])
        else:
            if layer == N_A_LAYERS:
                kv = _rmsnorm(h, kv_norm) @ w_kv
                k_shared, v_shared = jnp.split(kv.reshape(B, S, 2 * N_KV_HEADS, HEAD_DIM), 2, axis=2)
            i = layer - N_A_LAYERS
            mix = _dilated_mixer(_rmsnorm(h, g[0]), k_shared, v_shared, b_w_q[i], b_w_o[i])
        h = h + _rmsnorm(mix, g[1])
        hn = _rmsnorm(h, g[2])
        if layer % 2 == 0:
            f = _swiglu(hn, ffn_w_gu[layer // 2], ffn_w_down[layer // 2])
        else:
            j = layer // 2
            f = _moe_ffn(hn, moe_router[j], moe_w_gu[j], moe_w_down[j])
        h = h + _rmsnorm(f, g[3])
    return h
```

```python
import functools

import jax
import jax.numpy as jnp
from jax import lax
from jax.experimental import pallas as pl
from jax.experimental.pallas import tpu as pltpu

F32 = jnp.float32
BF16 = jnp.bfloat16

D_MODEL = 2048
EPS = 1e-6
CHUNK = 128
GMLP_HALF = 2 * D_MODEL
GMLP_GROUPS = 8
GMLP_GROUP_CH = GMLP_HALF // GMLP_GROUPS
HEAD_DIM = 128
N_KV_HEADS = D_MODEL // HEAD_DIM
DILATED_GROUPS = ((128, 1), (512, 4), (2048, 16))
BLOCK = 128
NEG = -1e30
D_FF = 7168
N_EXPERTS = 8
TOP_K = 2

LANES = 128
VMEM_LIMIT_BYTES = 56 * 1024 * 1024

NORM_ROWS = 128
MOE_TM = 512
MOE_TF = 512


def _params(*sem):
    return pltpu.CompilerParams(dimension_semantics=sem, vmem_limit_bytes=VMEM_LIMIT_BYTES)


def _rms_scale(x):
    return lax.rsqrt(jnp.mean(x * x, axis=-1, keepdims=True) + EPS)


def _norm_rows_into(x_ref, g_ref, out_ref):
    def body(c, carry):
        rows = pl.ds(pl.multiple_of(c * NORM_ROWS, NORM_ROWS), NORM_ROWS)
        x = x_ref[rows, :]
        out_ref[rows, :] = ((x * _rms_scale(x)) * g_ref[...]).astype(out_ref.dtype)
        return carry
    lax.fori_loop(0, x_ref.shape[0] // NORM_ROWS, body, 0)


def _residual_norm_rows(res_ref, acc_ref, g_ref, out_ref):
    def body(c, carry):
        rows = pl.ds(pl.multiple_of(c * NORM_ROWS, NORM_ROWS), NORM_ROWS)
        a = acc_ref[rows, :]
        out_ref[rows, :] = res_ref[rows, :] + (a * _rms_scale(a)) * g_ref[...]
        return carry
    lax.fori_loop(0, acc_ref.shape[0] // NORM_ROWS, body, 0)


def _gelu_tanh(x):
    cdf = 0.5 * (1.0 + jnp.tanh(0.7978845608028654 * (x + 0.044715 * (x * x * x))))
    return x * cdf


def _proj_body(x_ref, g_ref, w_ref, o_ref, xn_ref):
    @pl.when(pl.program_id(1) == 0)
    def _():
        _norm_rows_into(x_ref, g_ref, xn_ref)
    o_ref[...] = jnp.dot(xn_ref[...], w_ref[...], preferred_element_type=F32).astype(o_ref.dtype)


def _norm_proj(x, gain, w, *, tm, tn, name):
    m, k = x.shape
    n = w.shape[1]
    return pl.pallas_call(
        _proj_body,
        grid=(m // tm, n // tn),
        in_specs=[
            pl.BlockSpec((tm, k), lambda i, j: (i, 0)),
            pl.BlockSpec((1, k), lambda i, j: (0, 0)),
            pl.BlockSpec((k, tn), lambda i, j: (0, j)),
        ],
        out_specs=pl.BlockSpec((tm, tn), lambda i, j: (i, j)),
        out_shape=jax.ShapeDtypeStruct((m, n), BF16),
        scratch_shapes=[pltpu.VMEM((tm, k), BF16)],
        compiler_params=_params("parallel", "arbitrary"),
        name=name,
    )(x, gain.reshape(1, k), w)


def _gmlp_in_body(x_ref, g_ref, w_ref, b_ref, z_ref, ssq_ref, xn_ref, ss_ref):
    j = pl.program_id(1)
    nj = pl.num_programs(1)

    @pl.when(j == 0)
    def _():
        _norm_rows_into(x_ref, g_ref, xn_ref)
        ss_ref[...] = jnp.zeros_like(ss_ref)

    z = jnp.dot(xn_ref[...], w_ref[...], preferred_element_type=F32) + b_ref[...]
    z = _gelu_tanh(z)
    z_ref[...] = z.astype(z_ref.dtype)

    @pl.when(j >= nj // 2)
    def _():
        ss_ref[...] += jnp.sum(z * z, axis=-1, keepdims=True)

    @pl.when(j == nj - 1)
    def _():
        ssq_ref[...] = jnp.broadcast_to(ss_ref[...], ssq_ref.shape)


def _gmlp_in(x, gain, w_in, b_in, *, tm=1024, tn=512):
    m, k = x.shape
    n = w_in.shape[1]
    return pl.pallas_call(
        _gmlp_in_body,
        grid=(m // tm, n // tn),
        in_specs=[
            pl.BlockSpec((tm, k), lambda i, j: (i, 0)),
            pl.BlockSpec((1, k), lambda i, j: (0, 0)),
            pl.BlockSpec((k, tn), lambda i, j: (0, j)),
            pl.BlockSpec((1, tn), lambda i, j: (0, j)),
        ],
        out_specs=[
            pl.BlockSpec((tm, tn), lambda i, j: (i, j)),
            pl.BlockSpec((tm, LANES), lambda i, j: (i, 0)),
        ],
        out_shape=[
            jax.ShapeDtypeStruct((m, n), BF16),
            jax.ShapeDtypeStruct((m, LANES), F32),
        ],
        scratch_shapes=[pltpu.VMEM((tm, k), BF16), pltpu.VMEM((tm, 1), F32)],
        compiler_params=_params("parallel", "arbitrary"),
        name="gmlp_in",
    )(x, gain.reshape(1, k), w_in, b_in.reshape(1, n))


def _gmlp_out_body(u_ref, v_ref, ssq_ref, gv_ref, ws_ref, bs_ref, x_ref, g_ref, wo_ref,
                   o_ref, acc_ref, gated_ref):
    grp = pl.program_id(1)
    tm = u_ref.shape[0]

    @pl.when(grp == 0)
    def _():
        acc_ref[...] = jnp.zeros_like(acc_ref)

    row = lax.broadcasted_iota(jnp.int32, (CHUNK, CHUNK), 0)
    col = lax.broadcasted_iota(jnp.int32, (CHUNK, CHUNK), 1)
    ws = jnp.where(row >= col, ws_ref[0], 0.0).astype(BF16)
    r = lax.rsqrt(ssq_ref[:, 0:1] * (1.0 / GMLP_HALF) + EPS)
    for c in range(tm // CHUNK):
        rows = slice(c * CHUNK, (c + 1) * CHUNK)
        vn = (v_ref[rows, :].astype(F32) * r[rows, :]) * gv_ref[...]
        mixed = jnp.dot(ws, vn.astype(BF16), preferred_element_type=F32) + bs_ref[0]
        gated_ref[rows, :] = (u_ref[rows, :].astype(F32) * mixed).astype(BF16)
    acc_ref[...] += jnp.dot(gated_ref[...], wo_ref[...], preferred_element_type=F32)

    @pl.when(grp == pl.num_programs(1) - 1)
    def _():
        _residual_norm_rows(x_ref, acc_ref, g_ref, o_ref)


def _gmlp_out(z, ssq, norm_v, w_s, b_s, x, gain, w_out, *, tm=512):
    m, d = x.shape
    gc = GMLP_GROUP_CH
    return pl.pallas_call(
        _gmlp_out_body,
        grid=(m // tm, GMLP_GROUPS),
        in_specs=[
            pl.BlockSpec((tm, gc), lambda i, g: (i, g)),
            pl.BlockSpec((tm, gc), lambda i, g: (i, GMLP_GROUPS + g)),
            pl.BlockSpec((tm, LANES), lambda i, g: (i, 0)),
            pl.BlockSpec((1, gc), lambda i, g: (0, g)),
            pl.BlockSpec((1, CHUNK, CHUNK), lambda i, g: (g, 0, 0)),
            pl.BlockSpec((1, CHUNK, 1), lambda i, g: (g, 0, 0)),
            pl.BlockSpec((tm, d), lambda i, g: (i, 0)),
            pl.BlockSpec((1, d), lambda i, g: (0, 0)),
            pl.BlockSpec((gc, d), lambda i, g: (g, 0)),
        ],
        out_specs=pl.BlockSpec((tm, d), lambda i, g: (i, 0)),
        out_shape=jax.ShapeDtypeStruct((m, d), F32),
        scratch_shapes=[pltpu.VMEM((tm, d), F32), pltpu.VMEM((tm, gc), BF16)],
        compiler_params=_params("parallel", "arbitrary"),
        name="gmlp_out",
    )(z, z, ssq, norm_v.reshape(1, GMLP_HALF), w_s, b_s[:, :, None], x, gain.reshape(1, d), w_out)


def _swiglu_step(xn, wg, wu, wd):
    g = jnp.dot(xn, wg, preferred_element_type=F32)
    u = jnp.dot(xn, wu, preferred_element_type=F32)
    a = (g * jax.nn.sigmoid(g)) * u
    return jnp.dot(a.astype(BF16), wd, preferred_element_type=F32)


def _ffn_body(x_ref, g_in_ref, wg_ref, wu_ref, wd_ref, g_out_ref, o_ref, xn_ref, acc_ref):
    j = pl.program_id(1)

    @pl.when(j == 0)
    def _():
        _norm_rows_into(x_ref, g_in_ref, xn_ref)
        acc_ref[...] = jnp.zeros_like(acc_ref)

    acc_ref[...] += _swiglu_step(xn_ref[...], wg_ref[...], wu_ref[...], wd_ref[...])

    @pl.when(j == pl.num_programs(1) - 1)
    def _():
        _residual_norm_rows(x_ref, acc_ref, g_out_ref, o_ref)


def _ffn_dense(x, g_in, w_gu, w_down, g_out, *, tm=512, tf=512):
    m, d = x.shape
    nj = D_FF // tf
    return pl.pallas_call(
        _ffn_body,
        grid=(m // tm, nj),
        in_specs=[
            pl.BlockSpec((tm, d), lambda i, j: (i, 0)),
            pl.BlockSpec((1, d), lambda i, j: (0, 0)),
            pl.BlockSpec((d, tf), lambda i, j: (0, j)),
            pl.BlockSpec((d, tf), lambda i, j: (0, nj + j)),
            pl.BlockSpec((tf, d), lambda i, j: (j, 0)),
            pl.BlockSpec((1, d), lambda i, j: (0, 0)),
        ],
        out_specs=pl.BlockSpec((tm, d), lambda i, j: (i, 0)),
        out_shape=jax.ShapeDtypeStruct((m, d), F32),
        scratch_shapes=[pltpu.VMEM((tm, d), BF16), pltpu.VMEM((tm, d), F32)],
        compiler_params=_params("parallel", "arbitrary"),
        name="ffn_dense",
    )(x, g_in.reshape(1, d), w_gu, w_gu, w_down, g_out.reshape(1, d))


def _attn_body(dilation, slope_ref, q_ref, kc_ref, kp_ref, vc_ref, vp_ref, o_ref, lse_ref):
    has_prev = pl.program_id(2) > 0
    jq = lax.broadcasted_iota(jnp.int32, (BLOCK, BLOCK), 0)
    kk = lax.broadcasted_iota(jnp.int32, (BLOCK, BLOCK), 1)
    valid_c = kk <= jq
    valid_p = jnp.logical_and(kk >= jq, has_prev)
    dist_c = ((jq - kk) * dilation).astype(F32)
    dist_p = ((BLOCK + jq - kk) * dilation).astype(F32)
    lane = lax.broadcasted_iota(jnp.int32, (BLOCK, LANES), 1)
    scale = HEAD_DIM ** -0.5
    contract_last = (((1,), (1,)), ((), ()))

    lse_tile = jnp.zeros((BLOCK, LANES), F32)
    for h in range(N_KV_HEADS):
        cols = slice(h * HEAD_DIM, (h + 1) * HEAD_DIM)
        q = q_ref[0, :, cols]
        slope = slope_ref[h]
        s_c = lax.dot_general(q, kc_ref[0, :, cols], contract_last, preferred_element_type=F32)
        s_p = lax.dot_general(q, kp_ref[0, :, cols], contract_last, preferred_element_type=F32)
        s_c = jnp.where(valid_c, s_c * scale - slope * dist_c, NEG)
        s_p = jnp.where(valid_p, s_p * scale - slope * dist_p, NEG)
        m = jnp.maximum(jnp.max(s_c, axis=-1, keepdims=True), jnp.max(s_p, axis=-1, keepdims=True))
        p_c = jnp.exp(s_c - m)
        p_p = jnp.exp(s_p - m)
        den = jnp.sum(p_c, axis=-1, keepdims=True) + jnp.sum(p_p, axis=-1, keepdims=True)
        o = (jnp.dot(p_c.astype(BF16), vc_ref[0, :, cols], preferred_element_type=F32)
             + jnp.dot(p_p.astype(BF16), vp_ref[0, :, cols], preferred_element_type=F32))
        o_ref[0, :, cols] = (o / den).astype(o_ref.dtype)
        lse_tile = jnp.where(lane == h, m + jnp.log(den), lse_tile)
    lse_ref[0] = lse_tile


def _attn_group(base, q, kv, group_index, window, dilation, batch, seq):
    d = dilation
    assert window // d == BLOCK, "keys per query must span exactly one previous block"
    sub_len = seq // d
    assert sub_len % BLOCK == 0
    nb = sub_len // BLOCK
    n_groups = len(DILATED_GROUPS)
    q_sub = q.reshape(batch, sub_len, d * n_groups * D_MODEL)
    kv_sub = kv.reshape(batch, sub_len, d * 2 * D_MODEL)
    blk = (1, BLOCK, D_MODEL)
    prev = lambda n: jnp.maximum(n - 1, 0)
    o, lse = pl.pallas_call(
        functools.partial(_attn_body, d),
        grid_spec=pltpu.PrefetchScalarGridSpec(
            num_scalar_prefetch=1,
            grid=(batch, d, nb),
            in_specs=[
                pl.BlockSpec(blk, lambda b, r, n, base: (b, n, r * n_groups + group_index)),
                pl.BlockSpec(blk, lambda b, r, n, base: (b, n, 2 * r)),
                pl.BlockSpec(blk, lambda b, r, n, base: (b, prev(n), 2 * r)),
                pl.BlockSpec(blk, lambda b, r, n, base: (b, n, 2 * r + 1)),
                pl.BlockSpec(blk, lambda b, r, n, base: (b, prev(n), 2 * r + 1)),
            ],
            out_specs=[
                pl.BlockSpec(blk, lambda b, r, n, base: (b, n, r)),
                pl.BlockSpec((1, BLOCK, LANES), lambda b, r, n, base: (b, n, r)),
            ],
        ),
        out_shape=[
            jax.ShapeDtypeStruct((batch, sub_len, d * D_MODEL), BF16),
            jax.ShapeDtypeStruct((batch, sub_len, d * LANES), F32),
        ],
        compiler_params=_params("parallel", "parallel", "arbitrary"),
        name=f"attn_g{group_index}",
    )(base / d, q_sub, kv_sub, kv_sub, kv_sub, kv_sub)
    return o.reshape(batch * seq, D_MODEL), lse.reshape(batch * seq, LANES)


def _attn_out_body(o1_ref, o2_ref, o3_ref, l1_ref, l2_ref, l3_ref, x_ref, g_ref, wo_ref,
                   out_ref, merged_ref, acc_ref):
    l1, l2, l3 = l1_ref[...], l2_ref[...], l3_ref[...]
    mx = jnp.maximum(jnp.maximum(l1, l2), l3)
    e1, e2, e3 = jnp.exp(l1 - mx), jnp.exp(l2 - mx), jnp.exp(l3 - mx)
    inv = 1.0 / (e1 + e2 + e3)
    a1, a2, a3 = e1 * inv, e2 * inv, e3 * inv
    for h in range(N_KV_HEADS):
        cols = slice(h * HEAD_DIM, (h + 1) * HEAD_DIM)
        merged = (a1[:, h:h + 1] * o1_ref[:, cols].astype(F32)
                  + a2[:, h:h + 1] * o2_ref[:, cols].astype(F32)
                  + a3[:, h:h + 1] * o3_ref[:, cols].astype(F32))
        merged_ref[:, cols] = merged.astype(BF16)
    acc_ref[...] = jnp.dot(merged_ref[...], wo_ref[...], preferred_element_type=F32)
    _residual_norm_rows(x_ref, acc_ref, g_ref, out_ref)


def _attn_out(outs, lses, x, gain, w_o, *, tm=256):
    m, d = x.shape
    row_blk = pl.BlockSpec((tm, d), lambda i: (i, 0))
    lse_blk = pl.BlockSpec((tm, LANES), lambda i: (i, 0))
    return pl.pallas_call(
        _attn_out_body,
        grid=(m // tm,),
        in_specs=[row_blk, row_blk, row_blk, lse_blk, lse_blk, lse_blk, row_blk,
                  pl.BlockSpec((1, d), lambda i: (0, 0)),
                  pl.BlockSpec((d, d), lambda i: (0, 0))],
        out_specs=row_blk,
        out_shape=jax.ShapeDtypeStruct((m, d), F32),
        scratch_shapes=[pltpu.VMEM((tm, d), BF16), pltpu.VMEM((tm, d), F32)],
        compiler_params=_params("parallel"),
        name="attn_out",
    )(*outs, *lses, x, gain.reshape(1, d), w_o)


def _route_body(x_ref, g_ref, wr_ref, o_ref, xn_ref):
    _norm_rows_into(x_ref, g_ref, xn_ref)
    logits = jnp.dot(xn_ref[...], wr_ref[...], preferred_element_type=F32,
                     precision=lax.Precision.HIGHEST)
    lane = lax.broadcasted_iota(jnp.int32, logits.shape, 1)
    logits = jnp.where(lane < N_EXPERTS, logits, -jnp.inf)
    m1 = jnp.max(logits, axis=-1, keepdims=True)
    i1 = jnp.min(jnp.where(logits == m1, lane, LANES), axis=-1, keepdims=True)
    rest = jnp.where(lane == i1, -jnp.inf, logits)
    m2 = jnp.max(rest, axis=-1, keepdims=True)
    i2 = jnp.min(jnp.where(rest == m2, lane, LANES), axis=-1, keepdims=True)
    t = jnp.exp(m2 - m1)
    den = 1.0 + t
    out = jnp.where(lane == 0, i1.astype(F32), 0.0)
    out = jnp.where(lane == 1, i2.astype(F32), out)
    out = jnp.where(lane == 2, 1.0 / den, out)
    out = jnp.where(lane == 3, t / den, out)
    o_ref[...] = out


def _moe_route(x, gain, w_router, *, tm=512):
    m, d = x.shape
    wr = jnp.zeros((d, LANES), F32).at[:, :N_EXPERTS].set(w_router)
    return pl.pallas_call(
        _route_body,
        grid=(m // tm,),
        in_specs=[pl.BlockSpec((tm, d), lambda i: (i, 0)),
                  pl.BlockSpec((1, d), lambda i: (0, 0)),
                  pl.BlockSpec((d, LANES), lambda i: (0, 0))],
        out_specs=pl.BlockSpec((tm, LANES), lambda i: (i, 0)),
        out_shape=jax.ShapeDtypeStruct((m, LANES), F32),
        scratch_shapes=[pltpu.VMEM((tm, d), F32)],
        compiler_params=_params("parallel"),
        name="moe_route",
    )(x, gain.reshape(1, d), wr)


def _row_copy(src_hbm, row, dst_ref, r, sem):
    return pltpu.make_async_copy(src_hbm.at[pl.ds(row, 1), :], dst_ref.at[pl.ds(r, 1), :], sem)


def _gather_rows(idx_ref, base, src_hbm, dst_ref, sem):
    n = dst_ref.shape[0]

    def start(r, carry):
        _row_copy(src_hbm, idx_ref[base + r], dst_ref, r, sem).start()
        return carry
    lax.fori_loop(0, n, start, 0)

    def wait(r, carry):
        _row_copy(src_hbm, idx_ref[base + r], dst_ref, r, sem).wait()
        return carry
    lax.fori_loop(0, n, wait, 0)


def _experts_body(tok_ref, bexp_ref, nused_ref, h_hbm, g_ref, gate_ref, wg_ref, wu_ref, wd_ref,
                  ys_ref, xf_ref, xn_ref, acc_ref, sem):
    i = pl.program_id(0)
    j = pl.program_id(1)
    last = pl.num_programs(1) - 1
    used = i < nused_ref[0]

    @pl.when(jnp.logical_and(used, j == 0))
    def _():
        _gather_rows(tok_ref, i * MOE_TM, h_hbm, xf_ref, sem)
        _norm_rows_into(xf_ref, g_ref, xn_ref)
        acc_ref[...] = jnp.zeros_like(acc_ref)

    @pl.when(used)
    def _():
        acc_ref[...] += _swiglu_step(xn_ref[...], wg_ref[0], wu_ref[0], wd_ref[0])

    @pl.when(jnp.logical_and(used, j == last))
    def _():
        ys_ref[...] = acc_ref[...] * gate_ref[...]

    @pl.when(jnp.logical_and(jnp.logical_not(used), j == last))
    def _():
        ys_ref[...] = jnp.zeros_like(ys_ref)


def _moe_experts(tok_pad, blk_expert, n_used, h, gain, gate_pad, w_gu, w_down):
    d = h.shape[1]
    n_blocks = tok_pad.shape[0] // MOE_TM
    nj = D_FF // MOE_TF

    def col(i, j, nused):
        return jnp.where(i < nused[0], j, nj - 1)

    return pl.pallas_call(
        _experts_body,
        grid_spec=pltpu.PrefetchScalarGridSpec(
            num_scalar_prefetch=3,
            grid=(n_blocks, nj),
            in_specs=[
                pl.BlockSpec(memory_space=pl.ANY),
                pl.BlockSpec((1, d), lambda i, j, tok, be, nu: (0, 0)),
                pl.BlockSpec((MOE_TM, 1), lambda i, j, tok, be, nu: (i, 0)),
                pl.BlockSpec((1, d, MOE_TF), lambda i, j, tok, be, nu: (be[i], 0, col(i, j, nu))),
                pl.BlockSpec((1, d, MOE_TF), lambda i, j, tok, be, nu: (be[i], 0, nj + col(i, j, nu))),
                pl.BlockSpec((1, MOE_TF, d), lambda i, j, tok, be, nu: (be[i], col(i, j, nu), 0)),
            ],
            out_specs=pl.BlockSpec((MOE_TM, d), lambda i, j, tok, be, nu: (i, 0)),
            scratch_shapes=[pltpu.VMEM((MOE_TM, d), F32), pltpu.VMEM((MOE_TM, d), BF16),
                            pltpu.VMEM((MOE_TM, d), F32), pltpu.SemaphoreType.DMA],
        ),
        out_shape=jax.ShapeDtypeStruct((n_blocks * MOE_TM, d), F32),
        compiler_params=_params("arbitrary", "arbitrary"),
        name="moe_experts",
    )(tok_pad, blk_expert, n_used, h, gain.reshape(1, d), gate_pad, w_gu, w_gu, w_down)


def _combine_body(p0_ref, p1_ref, ys_hbm, x_ref, g_ref, o_ref, a_ref, b_ref, sem):
    tm = x_ref.shape[0]
    base = pl.program_id(0) * tm
    _gather_rows(p0_ref, base, ys_hbm, a_ref, sem)
    _gather_rows(p1_ref, base, ys_hbm, b_ref, sem)
    a_ref[...] = a_ref[...] + b_ref[...]
    _residual_norm_rows(x_ref, a_ref, g_ref, o_ref)


def _moe_combine(pos0, pos1, ys, x, gain, *, tm=256):
    m, d = x.shape
    return pl.pallas_call(
        _combine_body,
        grid_spec=pltpu.PrefetchScalarGridSpec(
            num_scalar_prefetch=2,
            grid=(m // tm,),
            in_specs=[
                pl.BlockSpec(memory_space=pl.ANY),
                pl.BlockSpec((tm, d), lambda i, p0, p1: (i, 0)),
                pl.BlockSpec((1, d), lambda i, p0, p1: (0, 0)),
            ],
            out_specs=pl.BlockSpec((tm, d), lambda i, p0, p1: (i, 0)),
            scratch_shapes=[pltpu.VMEM((tm, d), F32), pltpu.VMEM((tm, d), F32),
                            pltpu.SemaphoreType.DMA],
        ),
        out_shape=jax.ShapeDtypeStruct((m, d), F32),
        compiler_params=_params("arbitrary"),
        name="moe_combine",
    )(pos0, pos1, ys, x, gain.reshape(1, d))


def _dispatch_plan(route):
    n = route.shape[0]
    experts = route[:, :TOP_K].astype(jnp.int32).reshape(-1)
    gates = route[:, TOP_K:2 * TOP_K].reshape(-1)
    onehot = (experts[:, None] == jnp.arange(N_EXPERTS)[None, :]).astype(jnp.int32)
    rank = jnp.take_along_axis(jnp.cumsum(onehot, axis=0) - onehot, experts[:, None], axis=1)[:, 0]
    counts = jnp.sum(onehot, axis=0)
    blocks_per_expert = (counts + MOE_TM - 1) // MOE_TM
    block_end = jnp.cumsum(blocks_per_expert)
    block_start = block_end - blocks_per_expert
    dest = block_start[experts] * MOE_TM + rank
    n_blocks = (n * TOP_K) // MOE_TM + N_EXPERTS
    tok_pad = jnp.zeros((n_blocks * MOE_TM,), jnp.int32).at[dest].set(jnp.arange(n * TOP_K) // TOP_K)
    gate_pad = jnp.zeros((n_blocks * MOE_TM,), F32).at[dest].set(gates)
    blk_expert = jnp.clip(jnp.searchsorted(block_end, jnp.arange(n_blocks), side='right'),
                          0, N_EXPERTS - 1).astype(jnp.int32)
    n_used = block_end[-1].astype(jnp.int32)
    blk_expert = jnp.where(jnp.arange(n_blocks) < n_used, blk_expert,
                           blk_expert[jnp.maximum(n_used - 1, 0)])
    pos = dest.reshape(n, TOP_K)
    return tok_pad, gate_pad[:, None], blk_expert, n_used.reshape(1), pos[:, 0], pos[:, 1]


def kernel(x, norm_gains, a_w_in, a_b_in, a_norm_v, a_w_s, a_b_s, a_w_out, kv_norm, w_kv,
           b_w_q, b_w_o, ffn_w_gu, ffn_w_down, moe_router, moe_w_gu, moe_w_down):
    batch, seq, d = x.shape
    h = x.reshape(batch * seq, d)
    bf = lambda w: w.astype(BF16)

    g = norm_gains[0]
    z, ssq = _gmlp_in(h, g[0], bf(a_w_in[0]), a_b_in[0])
    h = _gmlp_out(z, ssq, a_norm_v[0], a_w_s[0], a_b_s[0], h, g[1], bf(a_w_out[0]))
    h = _ffn_dense(h, g[2], bf(ffn_w_gu[0]), bf(ffn_w_down[0]), g[3])

    g = norm_gains[1]
    kv = _norm_proj(h, kv_norm, bf(w_kv), tm=1024, tn=512, name="proj_kv")
    q = _norm_proj(h, g[0], bf(b_w_q[0]), tm=1024, tn=512, name="proj_q")
    base = jnp.exp2(-8.0 * jnp.arange(1, N_KV_HEADS + 1, dtype=F32) / N_KV_HEADS)
    outs, lses = [], []
    for gi, (window, dilation) in enumerate(DILATED_GROUPS):
        o, lse = _attn_group(base, q, kv, gi, window, dilation, batch, seq)
        outs.append(o)
        lses.append(lse)
    h = _attn_out(outs, lses, h, g[1], bf(b_w_o[0]))

    route = _moe_route(h, g[2], moe_router[0])
    tok_pad, gate_pad, blk_expert, n_used, pos0, pos1 = _dispatch_plan(route)
    ys = _moe_experts(tok_pad, blk_expert, n_used, h, g[2], gate_pad,
                      bf(moe_w_gu[0]), bf(moe_w_down[0]))
    h = _moe_combine(pos0, pos1, ys, h, g[3])
    return h.reshape(batch, seq, d)
```

```python
import functools

import jax
import jax.numpy as jnp
from jax import lax
from jax.experimental import pallas as pl
from jax.experimental.pallas import tpu as pltpu

F32 = jnp.float32
BF16 = jnp.bfloat16

D_MODEL = 2048
EPS = 1e-6
CHUNK = 128
GMLP_HALF = 2 * D_MODEL
GMLP_GROUPS = 8
GMLP_GROUP_CH = GMLP_HALF // GMLP_GROUPS
HEAD_DIM = 128
N_KV_HEADS = D_MODEL // HEAD_DIM
DILATED_GROUPS = ((128, 1), (512, 4), (2048, 16))
BLOCK = 128
NEG = -1e30
D_FF = 7168
N_EXPERTS = 8
TOP_K = 2

LANES = 128
VMEM_LIMIT_BYTES = 56 * 1024 * 1024

NORM_ROWS = 128
MOE_TM = 512
MOE_TF = 512


def _params(*sem):
    return pltpu.CompilerParams(dimension_semantics=sem, vmem_limit_bytes=VMEM_LIMIT_BYTES)


def _rms_scale(x):
    return lax.rsqrt(jnp.mean(x * x, axis=-1, keepdims=True) + EPS)


def _norm_rows_into(x_ref, g_ref, out_ref):
    def body(c, carry):
        rows = pl.ds(pl.multiple_of(c * NORM_ROWS, NORM_ROWS), NORM_ROWS)
        x = x_ref[rows, :]
        out_ref[rows, :] = ((x * _rms_scale(x)) * g_ref[...]).astype(out_ref.dtype)
        return carry
    lax.fori_loop(0, x_ref.shape[0] // NORM_ROWS, body, 0)


def _residual_norm_rows(res_ref, acc_ref, g_ref, out_ref):
    def body(c, carry):
        rows = pl.ds(pl.multiple_of(c * NORM_ROWS, NORM_ROWS), NORM_ROWS)
        a = acc_ref[rows, :]
        out_ref[rows, :] = res_ref[rows, :] + (a * _rms_scale(a)) * g_ref[...]
        return carry
    lax.fori_loop(0, acc_ref.shape[0] // NORM_ROWS, body, 0)


def _gelu_tanh(x):
    cdf = 0.5 * (1.0 + jnp.tanh(0.7978845608028654 * (x + 0.044715 * (x * x * x))))
    return x * cdf


def _proj_body(x_ref, g_ref, w_ref, o_ref, xn_ref):
    @pl.when(pl.program_id(1) == 0)
    def _():
        _norm_rows_into(x_ref, g_ref, xn_ref)
    o_ref[...] = jnp.dot(xn_ref[...], w_ref[...], preferred_element_type=F32).astype(o_ref.dtype)


def _norm_proj(x, gain, w, *, tm, tn, name):
    m, k = x.shape
    n = w.shape[1]
    return pl.pallas_call(
        _proj_body,
        grid=(m // tm, n // tn),
        in_specs=[
            pl.BlockSpec((tm, k), lambda i, j: (i, 0)),
            pl.BlockSpec((1, k), lambda i, j: (0, 0)),
            pl.BlockSpec((k, tn), lambda i, j: (0, j)),
        ],
        out_specs=pl.BlockSpec((tm, tn), lambda i, j: (i, j)),
        out_shape=jax.ShapeDtypeStruct((m, n), BF16),
        scratch_shapes=[pltpu.VMEM((tm, k), BF16)],
        compiler_params=_params("parallel", "arbitrary"),
        name=name,
    )(x, gain.reshape(1, k), w)


def _gmlp_in_body(x_ref, g_ref, w_ref, b_ref, z_ref, ssq_ref, xn_ref, ss_ref):
    j = pl.program_id(1)
    nj = pl.num_programs(1)

    @pl.when(j == 0)
    def _():
        _norm_rows_into(x_ref, g_ref, xn_ref)
        ss_ref[...] = jnp.zeros_like(ss_ref)

    z = jnp.dot(xn_ref[...], w_ref[...], preferred_element_type=F32) + b_ref[...]
    z = _gelu_tanh(z)
    z_ref[...] = z.astype(z_ref.dtype)

    @pl.when(j >= nj // 2)
    def _():
        ss_ref[...] += jnp.sum(z * z, axis=-1, keepdims=True)

    @pl.when(j == nj - 1)
    def _():
        ssq_ref[...] = jnp.broadcast_to(ss_ref[...], ssq_ref.shape)


def _gmlp_in(x, gain, w_in, b_in, *, tm=1024, tn=512):
    m, k = x.shape
    n = w_in.shape[1]
    return pl.pallas_call(
        _gmlp_in_body,
        grid=(m // tm, n // tn),
        in_specs=[
            pl.BlockSpec((tm, k), lambda i, j: (i, 0)),
            pl.BlockSpec((1, k), lambda i, j: (0, 0)),
            pl.BlockSpec((k, tn), lambda i, j: (0, j)),
            pl.BlockSpec((1, tn), lambda i, j: (0, j)),
        ],
        out_specs=[
            pl.BlockSpec((tm, tn), lambda i, j: (i, j)),
            pl.BlockSpec((tm, LANES), lambda i, j: (i, 0)),
        ],
        out_shape=[
            jax.ShapeDtypeStruct((m, n), BF16),
            jax.ShapeDtypeStruct((m, LANES), F32),
        ],
        scratch_shapes=[pltpu.VMEM((tm, k), BF16), pltpu.VMEM((tm, 1), F32)],
        compiler_params=_params("parallel", "arbitrary"),
        name="gmlp_in",
    )(x, gain.reshape(1, k), w_in, b_in.reshape(1, n))


def _gmlp_out_body(u_ref, v_ref, ssq_ref, gv_ref, ws_ref, bs_ref, x_ref, g_ref, wo_ref,
                   o_ref, acc_ref, gated_ref):
    grp = pl.program_id(1)
    tm = u_ref.shape[0]

    @pl.when(grp == 0)
    def _():
        acc_ref[...] = jnp.zeros_like(acc_ref)

    row = lax.broadcasted_iota(jnp.int32, (CHUNK, CHUNK), 0)
    col = lax.broadcasted_iota(jnp.int32, (CHUNK, CHUNK), 1)
    ws = jnp.where(row >= col, ws_ref[0], 0.0).astype(BF16)
    r = lax.rsqrt(ssq_ref[:, 0:1] * (1.0 / GMLP_HALF) + EPS)
    for c in range(tm // CHUNK):
        rows = slice(c * CHUNK, (c + 1) * CHUNK)
        vn = (v_ref[rows, :].astype(F32) * r[rows, :]) * gv_ref[...]
        mixed = jnp.dot(ws, vn.astype(BF16), preferred_element_type=F32) + bs_ref[0]
        gated_ref[rows, :] = (u_ref[rows, :].astype(F32) * mixed).astype(BF16)
    acc_ref[...] += jnp.dot(gated_ref[...], wo_ref[...], preferred_element_type=F32)

    @pl.when(grp == pl.num_programs(1) - 1)
    def _():
        _residual_norm_rows(x_ref, acc_ref, g_ref, o_ref)


def _gmlp_out(z, ssq, norm_v, w_s, b_s, x, gain, w_out, *, tm=512):
    m, d = x.shape
    gc = GMLP_GROUP_CH
    return pl.pallas_call(
        _gmlp_out_body,
        grid=(m // tm, GMLP_GROUPS),
        in_specs=[
            pl.BlockSpec((tm, gc), lambda i, g: (i, g)),
            pl.BlockSpec((tm, gc), lambda i, g: (i, GMLP_GROUPS + g)),
            pl.BlockSpec((tm, LANES), lambda i, g: (i, 0)),
            pl.BlockSpec((1, gc), lambda i, g: (0, g)),
            pl.BlockSpec((1, CHUNK, CHUNK), lambda i, g: (g, 0, 0)),
            pl.BlockSpec((1, CHUNK, 1), lambda i, g: (g, 0, 0)),
            pl.BlockSpec((tm, d), lambda i, g: (i, 0)),
            pl.BlockSpec((1, d), lambda i, g: (0, 0)),
            pl.BlockSpec((gc, d), lambda i, g: (g, 0)),
        ],
        out_specs=pl.BlockSpec((tm, d), lambda i, g: (i, 0)),
        out_shape=jax.ShapeDtypeStruct((m, d), F32),
        scratch_shapes=[pltpu.VMEM((tm, d), F32), pltpu.VMEM((tm, gc), BF16)],
        compiler_params=_params("parallel", "arbitrary"),
        name="gmlp_out",
    )(z, z, ssq, norm_v.reshape(1, GMLP_HALF), w_s, b_s[:, :, None], x, gain.reshape(1, d), w_out)


def _swiglu_step(xn, wg, wu, wd):
    g = jnp.dot(xn, wg, preferred_element_type=F32)
    u = jnp.dot(xn, wu, preferred_element_type=F32)
    a = (g * jax.nn.sigmoid(g)) * u
    return jnp.dot(a.astype(BF16), wd, preferred_element_type=F32)


def _ffn_body(x_ref, g_in_ref, wg_ref, wu_ref, wd_ref, g_out_ref, o_ref, xn_ref, acc_ref):
    j = pl.program_id(1)

    @pl.when(j == 0)
    def _():
        _norm_rows_into(x_ref, g_in_ref, xn_ref)
        acc_ref[...] = jnp.zeros_like(acc_ref)

    acc_ref[...] += _swiglu_step(xn_ref[...], wg_ref[...], wu_ref[...], wd_ref[...])

    @pl.when(j == pl.num_programs(1) - 1)
    def _():
        _residual_norm_rows(x_ref, acc_ref, g_out_ref, o_ref)


def _ffn_dense(x, g_in, w_gu, w_down, g_out, *, tm=512, tf=512):
    m, d = x.shape
    nj = D_FF // tf
    return pl.pallas_call(
        _ffn_body,
        grid=(m // tm, nj),
        in_specs=[
            pl.BlockSpec((tm, d), lambda i, j: (i, 0)),
            pl.BlockSpec((1, d), lambda i, j: (0, 0)),
            pl.BlockSpec((d, tf), lambda i, j: (0, j)),
            pl.BlockSpec((d, tf), lambda i, j: (0, nj + j)),
            pl.BlockSpec((tf, d), lambda i, j: (j, 0)),
            pl.BlockSpec((1, d), lambda i, j: (0, 0)),
        ],
        out_specs=pl.BlockSpec((tm, d), lambda i, j: (i, 0)),
        out_shape=jax.ShapeDtypeStruct((m, d), F32),
        scratch_shapes=[pltpu.VMEM((tm, d), BF16), pltpu.VMEM((tm, d), F32)],
        compiler_params=_params("parallel", "arbitrary"),
        name="ffn_dense",
    )(x, g_in.reshape(1, d), w_gu, w_gu, w_down, g_out.reshape(1, d))


SUB = 4
ATTN_UNROLL = 8


def _deinterleave(src_ref, dst_ref, span):
    part = span // SUB
    for base in range(0, src_ref.shape[0], span):
        for r in range(SUB):
            dst_ref[base + r * part: base + (r + 1) * part, :] = (
                src_ref[pl.ds(base + r, part, stride=SUB), :].astype(dst_ref.dtype))


def _interleave(src_ref, dst_ref, span):
    part = span // SUB
    for base in range(0, src_ref.shape[0], span):
        for r in range(SUB):
            dst_ref[pl.ds(base + r, part, stride=SUB), :] = (
                src_ref[base + r * part: base + (r + 1) * part, :])


def _window_attention(q_ref, k_ref, v_ref, slope, dilation, blocks_per_seq,
                      bias_c_ref, bias_p_ref, pc_ref, pp_ref, den_ref, o_ref, lse_ref):
    jq = lax.broadcasted_iota(jnp.int32, (BLOCK, BLOCK), 0)
    kk = lax.broadcasted_iota(jnp.int32, (BLOCK, BLOCK), 1)
    dist_c = ((jq - kk) * dilation).astype(F32)
    dist_p = ((BLOCK + jq - kk) * dilation).astype(F32)
    bias_c_ref[...] = jnp.where(kk <= jq, -(slope * dist_c), NEG)
    bias_p_ref[...] = jnp.where(kk >= jq, -(slope * dist_p), NEG)
    scale = HEAD_DIM ** -0.5
    contract_last = (((1,), (1,)), ((), ()))
    with_prev = blocks_per_seq > 1

    def block_rows(n):
        rows = pl.ds(pl.multiple_of(n * BLOCK, BLOCK), BLOCK)
        prev = pl.ds(pl.multiple_of(jnp.maximum(n - 1, 0) * BLOCK, BLOCK), BLOCK)
        return rows, prev

    def probabilities(n, carry):
        rows, prev = block_rows(n)
        q = q_ref[rows, :]
        s_c = lax.dot_general(q, k_ref[rows, :], contract_last, preferred_element_type=F32)
        s_c = s_c * scale + bias_c_ref[...]
        if with_prev:
            has_prev = (n % blocks_per_seq) != 0
            s_p = lax.dot_general(q, k_ref[prev, :], contract_last, preferred_element_type=F32)
            s_p = jnp.where(has_prev, s_p * scale + bias_p_ref[...], NEG)
            m = jnp.max(jnp.maximum(s_c, s_p), axis=-1, keepdims=True)
            p_c = jnp.exp(s_c - m)
            p_p = jnp.exp(s_p - m)
            den = jnp.sum(p_c + p_p, axis=-1, keepdims=True)
            pp_ref[rows, :] = p_p.astype(BF16)
        else:
            m = jnp.max(s_c, axis=-1, keepdims=True)
            p_c = jnp.exp(s_c - m)
            den = jnp.sum(p_c, axis=-1, keepdims=True)
        pc_ref[rows, :] = p_c.astype(BF16)
        den_ref[rows, :] = den
        lse_ref[rows, :] = jnp.broadcast_to(m + jnp.log(den), (BLOCK, LANES))
        return carry

    def values(n, carry):
        rows, prev = block_rows(n)
        o = jnp.dot(pc_ref[rows, :], v_ref[rows, :], preferred_element_type=F32)
        if with_prev:
            o = o + jnp.dot(pp_ref[rows, :], v_ref[prev, :], preferred_element_type=F32)
        o_ref[rows, :] = o / den_ref[rows, :]
        return carry

    n_blocks = q_ref.shape[0] // BLOCK
    lax.fori_loop(0, n_blocks, probabilities, 0, unroll=ATTN_UNROLL)
    lax.fori_loop(0, n_blocks, values, 0, unroll=ATTN_UNROLL)


def _attn_body(base_ref, q1_ref, q4_ref, q16_ref, k_ref, v_ref, out_ref,
               stage_ref, tmp_ref, k4f_ref, v4f_ref,
               qp_ref, k4_ref, v4_ref, k16_ref, v16_ref,
               o1_ref, l1_ref, o4_ref, l4_ref, o16_ref, l16_ref, op_ref, lp_ref,
               bias_c_ref, bias_p_ref, pc_ref, pp_ref, den_ref):
    seq = k_ref.shape[0]
    base = base_ref[pl.program_id(1)]
    (_, d1), (_, d4), (_, d16) = DILATED_GROUPS

    stage_ref[...] = k_ref[...].astype(F32)
    _deinterleave(stage_ref, k4f_ref, seq)
    k4_ref[...] = k4f_ref[...].astype(BF16)
    _deinterleave(k4f_ref, k16_ref, seq // SUB)
    stage_ref[...] = v_ref[...].astype(F32)
    _deinterleave(stage_ref, v4f_ref, seq)
    v4_ref[...] = v4f_ref[...].astype(BF16)
    _deinterleave(v4f_ref, v16_ref, seq // SUB)

    _window_attention(q1_ref, k_ref, v_ref, base / d1, d1, seq // BLOCK,
                      bias_c_ref, bias_p_ref, pc_ref, pp_ref, den_ref, o1_ref, l1_ref)

    stage_ref[...] = q4_ref[...].astype(F32)
    _deinterleave(stage_ref, qp_ref, seq)
    _window_attention(qp_ref, k4_ref, v4_ref, base / d4, d4, seq // d4 // BLOCK,
                      bias_c_ref, bias_p_ref, pc_ref, pp_ref, den_ref, op_ref, lp_ref)
    _interleave(op_ref, o4_ref, seq)
    _interleave(lp_ref, l4_ref, seq)

    stage_ref[...] = q16_ref[...].astype(F32)
    _deinterleave(stage_ref, tmp_ref, seq)
    _deinterleave(tmp_ref, qp_ref, seq // SUB)
    _window_attention(qp_ref, k16_ref, v16_ref, base / d16, d16, seq // d16 // BLOCK,
                      bias_c_ref, bias_p_ref, pc_ref, pp_ref, den_ref, op_ref, lp_ref)
    _interleave(op_ref, tmp_ref, seq // SUB)
    _interleave(tmp_ref, o16_ref, seq)
    _interleave(lp_ref, tmp_ref, seq // SUB)
    _interleave(tmp_ref, l16_ref, seq)

    chunk = 2 * BLOCK

    def merge(c, carry):
        rows = pl.ds(pl.multiple_of(c * chunk, chunk), chunk)
        la, lb, lc = l1_ref[rows, :], l4_ref[rows, :], l16_ref[rows, :]
        mx = jnp.maximum(jnp.maximum(la, lb), lc)
        ea, eb, ec = jnp.exp(la - mx), jnp.exp(lb - mx), jnp.exp(lc - mx)
        den = ea + eb + ec
        o = (ea / den) * o1_ref[rows, :] + (eb / den) * o4_ref[rows, :] + (ec / den) * o16_ref[rows, :]
        out_ref[rows, :] = o.astype(out_ref.dtype)
        return carry
    lax.fori_loop(0, seq // chunk, merge, 0)


def _attention(base, q, kv, batch, seq):
    for window, dilation in DILATED_GROUPS:
        assert window // dilation == BLOCK, "keys per query must span exactly one previous block"
    assert [d for _, d in DILATED_GROUPS] == [1, SUB, SUB * SUB]
    assert seq % (SUB * SUB * BLOCK) == 0
    h = N_KV_HEADS
    blk = (seq, HEAD_DIM)
    f32buf = pltpu.VMEM((seq, HEAD_DIM), F32)
    bf16buf = pltpu.VMEM((seq, HEAD_DIM), BF16)
    return pl.pallas_call(
        _attn_body,
        grid_spec=pltpu.PrefetchScalarGridSpec(
            num_scalar_prefetch=1,
            grid=(batch, h),
            in_specs=[
                pl.BlockSpec(blk, lambda b, hh, base: (b, hh)),
                pl.BlockSpec(blk, lambda b, hh, base: (b, h + hh)),
                pl.BlockSpec(blk, lambda b, hh, base: (b, 2 * h + hh)),
                pl.BlockSpec(blk, lambda b, hh, base: (b, hh)),
                pl.BlockSpec(blk, lambda b, hh, base: (b, h + hh)),
            ],
            out_specs=pl.BlockSpec(blk, lambda b, hh, base: (b, hh)),
            scratch_shapes=[f32buf] * 4 + [bf16buf] * 5 + [f32buf] * 8
                           + [pltpu.VMEM((BLOCK, BLOCK), F32)] * 2
                           + [bf16buf] * 2 + [pltpu.VMEM((seq, 1), F32)],
        ),
        out_shape=jax.ShapeDtypeStruct((batch * seq, D_MODEL), BF16),
        compiler_params=_params("parallel", "arbitrary"),
        name="attn",
    )(base, q, q, q, kv, kv)


def _attn_out_body(o_ref, x_ref, g_ref, wo_ref, out_ref, acc_ref):
    acc_ref[...] = jnp.dot(o_ref[...], wo_ref[...], preferred_element_type=F32)
    _residual_norm_rows(x_ref, acc_ref, g_ref, out_ref)


def _attn_out(o, x, gain, w_o, *, tm=512):
    m, d = x.shape
    row_blk = pl.BlockSpec((tm, d), lambda i: (i, 0))
    return pl.pallas_call(
        _attn_out_body,
        grid=(m // tm,),
        in_specs=[row_blk, row_blk,
                  pl.BlockSpec((1, d), lambda i: (0, 0)),
                  pl.BlockSpec((d, d), lambda i: (0, 0))],
        out_specs=row_blk,
        out_shape=jax.ShapeDtypeStruct((m, d), F32),
        scratch_shapes=[pltpu.VMEM((tm, d), F32)],
        compiler_params=_params("parallel"),
        name="attn_out",
    )(o, x, gain.reshape(1, d), w_o)


def _route_body(x_ref, g_ref, wr_ref, o_ref, xn_ref):
    _norm_rows_into(x_ref, g_ref, xn_ref)
    logits = jnp.dot(xn_ref[...], wr_ref[...], preferred_element_type=F32,
                     precision=lax.Precision.HIGHEST)
    lane = lax.broadcasted_iota(jnp.int32, logits.shape, 1)
    logits = jnp.where(lane < N_EXPERTS, logits, -jnp.inf)
    m1 = jnp.max(logits, axis=-1, keepdims=True)
    i1 = jnp.min(jnp.where(logits == m1, lane, LANES), axis=-1, keepdims=True)
    rest = jnp.where(lane == i1, -jnp.inf, logits)
    m2 = jnp.max(rest, axis=-1, keepdims=True)
    i2 = jnp.min(jnp.where(rest == m2, lane, LANES), axis=-1, keepdims=True)
    t = jnp.exp(m2 - m1)
    den = 1.0 + t
    out = jnp.where(lane == 0, i1.astype(F32), 0.0)
    out = jnp.where(lane == 1, i2.astype(F32), out)
    out = jnp.where(lane == 2, 1.0 / den, out)
    out = jnp.where(lane == 3, t / den, out)
    o_ref[...] = out


def _moe_route(x, gain, w_router, *, tm=512):
    m, d = x.shape
    wr = jnp.zeros((d, LANES), F32).at[:, :N_EXPERTS].set(w_router)
    return pl.pallas_call(
        _route_body,
        grid=(m // tm,),
        in_specs=[pl.BlockSpec((tm, d), lambda i: (i, 0)),
                  pl.BlockSpec((1, d), lambda i: (0, 0)),
                  pl.BlockSpec((d, LANES), lambda i: (0, 0))],
        out_specs=pl.BlockSpec((tm, LANES), lambda i: (i, 0)),
        out_shape=jax.ShapeDtypeStruct((m, LANES), F32),
        scratch_shapes=[pltpu.VMEM((tm, d), F32)],
        compiler_params=_params("parallel"),
        name="moe_route",
    )(x, gain.reshape(1, d), wr)


def _row_copy(src_hbm, row, dst_ref, r, sem):
    return pltpu.make_async_copy(src_hbm.at[pl.ds(row, 1), :], dst_ref.at[pl.ds(r, 1), :], sem)


GATHER_UNROLL = 8


def _start_row_gather(idx_ref, base, src_hbm, dst_ref, sem):
    def start(r, carry):
        _row_copy(src_hbm, idx_ref[base + r], dst_ref, r, sem).start()
        return carry
    lax.fori_loop(0, dst_ref.shape[0], start, 0, unroll=GATHER_UNROLL)


def _wait_row_gather(idx_ref, base, src_hbm, dst_ref, sem):
    def wait(r, carry):
        _row_copy(src_hbm, idx_ref[base + r], dst_ref, r, sem).wait()
        return carry
    lax.fori_loop(0, dst_ref.shape[0], wait, 0, unroll=GATHER_UNROLL)


def _experts_body(tok_ref, bexp_ref, nused_ref, h_hbm, g_ref, gate_ref, wg_ref, wu_ref, wd_ref,
                  ys_ref, xf_ref, xn_ref, acc_ref, sem):
    i = pl.program_id(0)
    j = pl.program_id(1)
    last = pl.num_programs(1) - 1
    used = i < nused_ref[0]

    @pl.when(jnp.logical_and(used, j == 0))
    def _():
        _start_row_gather(tok_ref, i * MOE_TM, h_hbm, xf_ref, sem)
        _wait_row_gather(tok_ref, i * MOE_TM, h_hbm, xf_ref, sem)
        _norm_rows_into(xf_ref, g_ref, xn_ref)
        acc_ref[...] = jnp.zeros_like(acc_ref)

    @pl.when(used)
    def _():
        acc_ref[...] += _swiglu_step(xn_ref[...], wg_ref[0], wu_ref[0], wd_ref[0])

    @pl.when(jnp.logical_and(used, j == last))
    def _():
        ys_ref[...] = acc_ref[...] * gate_ref[...]

    @pl.when(jnp.logical_and(jnp.logical_not(used), j == last))
    def _():
        ys_ref[...] = jnp.zeros_like(ys_ref)


def _moe_experts(tok_pad, blk_expert, n_used, h, gain, gate_pad, w_gu, w_down):
    d = h.shape[1]
    n_blocks = tok_pad.shape[0] // MOE_TM
    nj = D_FF // MOE_TF

    def col(i, j, nused):
        return jnp.where(i < nused[0], j, nj - 1)

    return pl.pallas_call(
        _experts_body,
        grid_spec=pltpu.PrefetchScalarGridSpec(
            num_scalar_prefetch=3,
            grid=(n_blocks, nj),
            in_specs=[
                pl.BlockSpec(memory_space=pl.ANY),
                pl.BlockSpec((1, d), lambda i, j, tok, be, nu: (0, 0)),
                pl.BlockSpec((MOE_TM, 1), lambda i, j, tok, be, nu: (i, 0)),
                pl.BlockSpec((1, d, MOE_TF), lambda i, j, tok, be, nu: (be[i], 0, col(i, j, nu))),
                pl.BlockSpec((1, d, MOE_TF), lambda i, j, tok, be, nu: (be[i], 0, nj + col(i, j, nu))),
                pl.BlockSpec((1, MOE_TF, d), lambda i, j, tok, be, nu: (be[i], col(i, j, nu), 0)),
            ],
            out_specs=pl.BlockSpec((MOE_TM, d), lambda i, j, tok, be, nu: (i, 0)),
            scratch_shapes=[pltpu.VMEM((MOE_TM, d), F32), pltpu.VMEM((MOE_TM, d), BF16),
                            pltpu.VMEM((MOE_TM, d), F32), pltpu.SemaphoreType.DMA],
        ),
        out_shape=jax.ShapeDtypeStruct((n_blocks * MOE_TM, d), F32),
        compiler_params=_params("arbitrary", "arbitrary"),
        name="moe_experts",
    )(tok_pad, blk_expert, n_used, h, gain.reshape(1, d), gate_pad, w_gu, w_gu, w_down)


def _combine_body(p0_ref, p1_ref, ys_hbm, x_ref, g_ref, o_ref, a_ref, b_ref, sem):
    tm = x_ref.shape[0]
    base = pl.program_id(0) * tm
    _start_row_gather(p0_ref, base, ys_hbm, a_ref, sem.at[0])
    _start_row_gather(p1_ref, base, ys_hbm, b_ref, sem.at[1])
    _wait_row_gather(p0_ref, base, ys_hbm, a_ref, sem.at[0])
    _wait_row_gather(p1_ref, base, ys_hbm, b_ref, sem.at[1])
    a_ref[...] = a_ref[...] + b_ref[...]
    _residual_norm_rows(x_ref, a_ref, g_ref, o_ref)


def _moe_combine(pos0, pos1, ys, x, gain, *, tm=256):
    m, d = x.shape
    return pl.pallas_call(
        _combine_body,
        grid_spec=pltpu.PrefetchScalarGridSpec(
            num_scalar_prefetch=2,
            grid=(m // tm,),
            in_specs=[
                pl.BlockSpec(memory_space=pl.ANY),
                pl.BlockSpec((tm, d), lambda i, p0, p1: (i, 0)),
                pl.BlockSpec((1, d), lambda i, p0, p1: (0, 0)),
            ],
            out_specs=pl.BlockSpec((tm, d), lambda i, p0, p1: (i, 0)),
            scratch_shapes=[pltpu.VMEM((tm, d), F32), pltpu.VMEM((tm, d), F32),
                            pltpu.SemaphoreType.DMA((2,))],
        ),
        out_shape=jax.ShapeDtypeStruct((m, d), F32),
        compiler_params=_params("arbitrary"),
        name="moe_combine",
    )(pos0, pos1, ys, x, gain.reshape(1, d))


def _dispatch_plan(route):
    n = route.shape[0]
    experts = route[:, :TOP_K].astype(jnp.int32).reshape(-1)
    gates = route[:, TOP_K:2 * TOP_K].reshape(-1)
    onehot = (experts[:, None] == jnp.arange(N_EXPERTS)[None, :]).astype(jnp.int32)
    rank = jnp.take_along_axis(jnp.cumsum(onehot, axis=0) - onehot, experts[:, None], axis=1)[:, 0]
    counts = jnp.sum(onehot, axis=0)
    blocks_per_expert = (counts + MOE_TM - 1) // MOE_TM
    block_end = jnp.cumsum(blocks_per_expert)
    block_start = block_end - blocks_per_expert
    dest = block_start[experts] * MOE_TM + rank
    n_blocks = (n * TOP_K) // MOE_TM + N_EXPERTS
    tok_pad = jnp.zeros((n_blocks * MOE_TM,), jnp.int32).at[dest].set(jnp.arange(n * TOP_K) // TOP_K)
    gate_pad = jnp.zeros((n_blocks * MOE_TM,), F32).at[dest].set(gates)
    blk_expert = jnp.clip(jnp.searchsorted(block_end, jnp.arange(n_blocks), side='right'),
                          0, N_EXPERTS - 1).astype(jnp.int32)
    n_used = block_end[-1].astype(jnp.int32)
    blk_expert = jnp.where(jnp.arange(n_blocks) < n_used, blk_expert,
                           blk_expert[jnp.maximum(n_used - 1, 0)])
    pos = dest.reshape(n, TOP_K)
    return tok_pad, gate_pad[:, None], blk_expert, n_used.reshape(1), pos[:, 0], pos[:, 1]


def kernel(x, norm_gains, a_w_in, a_b_in, a_norm_v, a_w_s, a_b_s, a_w_out, kv_norm, w_kv,
           b_w_q, b_w_o, ffn_w_gu, ffn_w_down, moe_router, moe_w_gu, moe_w_down):
    batch, seq, d = x.shape
    h = x.reshape(batch * seq, d)
    bf = lambda w: w.astype(BF16)

    g = norm_gains[0]
    z, ssq = _gmlp_in(h, g[0], bf(a_w_in[0]), a_b_in[0])
    h = _gmlp_out(z, ssq, a_norm_v[0], a_w_s[0], a_b_s[0], h, g[1], bf(a_w_out[0]))
    h = _ffn_dense(h, g[2], bf(ffn_w_gu[0]), bf(ffn_w_down[0]), g[3])

    g = norm_gains[1]
    kv = _norm_proj(h, kv_norm, bf(w_kv), tm=1024, tn=512, name="proj_kv")
    q = _norm_proj(h, g[0], bf(b_w_q[0]), tm=1024, tn=512, name="proj_q")
    base = jnp.exp2(-8.0 * jnp.arange(1, N_KV_HEADS + 1, dtype=F32) / N_KV_HEADS)
    o = _attention(base, q, kv, batch, seq)
    h = _attn_out(o, h, g[1], bf(b_w_o[0]))

    route = _moe_route(h, g[2], moe_router[0])
    tok_pad, gate_pad, blk_expert, n_used, pos0, pos1 = _dispatch_plan(route)
    ys = _moe_experts(tok_pad, blk_expert, n_used, h, g[2], gate_pad,
                      bf(moe_w_gu[0]), bf(moe_w_down[0]))
    h = _moe_combine(pos0, pos1, ys, h, g[3])
    return h.reshape(batch, seq, d)
```

```python
import functools
from typing import NamedTuple

import jax
import jax.numpy as jnp
from jax import lax
from jax.experimental import pallas as pl
from jax.experimental.pallas import tpu as pltpu

F32 = jnp.float32
BF16 = jnp.bfloat16

D_MODEL = 2048
EPS = 1e-6
CHUNK = 128
GMLP_HALF = 2 * D_MODEL
GMLP_GROUPS = 8
GMLP_GROUP_CH = GMLP_HALF // GMLP_GROUPS
HEAD_DIM = 128
N_KV_HEADS = D_MODEL // HEAD_DIM
DILATED_GROUPS = ((128, 1), (512, 4), (2048, 16))
BLOCK = 128
NEG = -1e30
D_FF = 7168
N_EXPERTS = 8
TOP_K = 2

LANES = 128
VMEM_LIMIT_BYTES = 56 * 1024 * 1024

NORM_ROWS = 128
MOE_TM = 512
MOE_TF = 512


def _params(*sem):
    return pltpu.CompilerParams(dimension_semantics=sem, vmem_limit_bytes=VMEM_LIMIT_BYTES)


def _rms_scale(x):
    return lax.rsqrt(jnp.mean(x * x, axis=-1, keepdims=True) + EPS)


def _norm_rows_into(x_ref, g_ref, out_ref):
    def body(c, carry):
        rows = pl.ds(pl.multiple_of(c * NORM_ROWS, NORM_ROWS), NORM_ROWS)
        x = x_ref[rows, :]
        out_ref[rows, :] = ((x * _rms_scale(x)) * g_ref[...]).astype(out_ref.dtype)
        return carry
    lax.fori_loop(0, x_ref.shape[0] // NORM_ROWS, body, 0)


def _residual_norm_rows(res_ref, acc_ref, g_ref, out_ref):
    def body(c, carry):
        rows = pl.ds(pl.multiple_of(c * NORM_ROWS, NORM_ROWS), NORM_ROWS)
        a = acc_ref[rows, :]
        out_ref[rows, :] = res_ref[rows, :] + (a * _rms_scale(a)) * g_ref[...]
        return carry
    lax.fori_loop(0, acc_ref.shape[0] // NORM_ROWS, body, 0)


BF16_SUBLANES = 16


class _Rider(NamedTuple):
    src: jax.Array
    dst: jax.Array | None
    first_row: int
    steps: int
    rows: int


def _whole_rider(w, host_steps):
    total = w.shape[0]
    for steps in range(host_steps, 0, -1):
        if total % steps == 0 and (total // steps) % BF16_SUBLANES == 0:
            return _Rider(w, None, 0, steps, total // steps)
    raise ValueError(f"no chunking of {w.shape} over {host_steps} steps")


def _part_rider(w, dst, first_row, steps, rows):
    assert rows % BF16_SUBLANES == 0 and first_row % rows == 0
    assert first_row + steps * rows <= w.shape[0]
    return _Rider(w, dst, first_row, steps, rows)


def _riding_body(body, n_in, n_out, n_riders, n_alias, *refs):
    ins, refs = refs[:n_in], refs[n_in:]
    r_in, refs = refs[:n_riders], refs[n_riders + n_alias:]
    outs, refs = refs[:n_out], refs[n_out:]
    r_out, scratch = refs[:n_riders], refs[n_riders:]
    body(*ins, *outs, *scratch)
    for src, dst in zip(r_in, r_out):
        dst[...] = src[...].astype(BF16)


def _hosted_call(body, *, grid, in_specs, out_specs, out_shape, scratch_shapes, args, riders,
                 semantics, name, num_scalar_prefetch=0):
    n_grid = len(grid)
    strides = [1] * n_grid
    for ax in range(n_grid - 2, -1, -1):
        strides[ax] = strides[ax + 1] * grid[ax + 1]
    assert all(r.steps <= strides[0] * grid[0] for r in riders)

    r_in, r_alias, r_out, r_shapes = [], [], [], []
    for r in riders:
        def index(*g, r=r):
            step = sum(g[ax] * strides[ax] for ax in range(n_grid))
            return r.first_row // r.rows + jnp.minimum(step, r.steps - 1), 0
        block = (r.rows, r.src.shape[1])
        r_in.append(pl.BlockSpec(block, index))
        r_out.append(pl.BlockSpec(block, index))
        r_shapes.append(jax.ShapeDtypeStruct(r.src.shape, BF16))
        if r.dst is not None:
            r_alias.append(r.dst)

    n_in = num_scalar_prefetch + len(in_specs)
    n_out = len(out_specs)
    aliases, k = {}, 0
    for ridx, r in enumerate(riders):
        if r.dst is not None:
            aliases[n_in + len(riders) + k] = n_out + ridx
            k += 1
    all_in = list(in_specs) + r_in + [pl.BlockSpec(memory_space=pl.ANY)] * len(r_alias)
    all_out = list(out_specs) + r_out
    kernel_fn = functools.partial(_riding_body, body, n_in, n_out, len(riders), len(r_alias))
    if num_scalar_prefetch:
        spec = dict(grid_spec=pltpu.PrefetchScalarGridSpec(
            num_scalar_prefetch=num_scalar_prefetch, grid=grid, in_specs=all_in,
            out_specs=all_out, scratch_shapes=scratch_shapes))
    else:
        spec = dict(grid=grid, in_specs=all_in, out_specs=all_out, scratch_shapes=scratch_shapes)
    outs = pl.pallas_call(
        kernel_fn,
        out_shape=list(out_shape) + r_shapes,
        input_output_aliases=aliases,
        compiler_params=_params(*semantics),
        name=name,
        **spec,
    )(*args, *[r.src for r in riders], *r_alias)
    return outs[:n_out], outs[n_out:]


def _gelu_tanh(x):
    cdf = 0.5 * (1.0 + jnp.tanh(0.7978845608028654 * (x + 0.044715 * (x * x * x))))
    return x * cdf


def _proj_body(x_ref, g_ref, w_ref, o_ref, xn_ref):
    @pl.when(pl.program_id(1) == 0)
    def _():
        _norm_rows_into(x_ref, g_ref, xn_ref)
    o_ref[...] = jnp.dot(xn_ref[...], w_ref[...], preferred_element_type=F32).astype(o_ref.dtype)


def _norm_proj(x, gain, w, riders, *, tm, tn, name):
    m, k = x.shape
    n = w.shape[1]
    (out,), casts = _hosted_call(
        _proj_body,
        grid=(m // tm, n // tn),
        in_specs=[
            pl.BlockSpec((tm, k), lambda i, j: (i, 0)),
            pl.BlockSpec((1, k), lambda i, j: (0, 0)),
            pl.BlockSpec((k, tn), lambda i, j: (0, j)),
        ],
        out_specs=[pl.BlockSpec((tm, tn), lambda i, j: (i, j))],
        out_shape=[jax.ShapeDtypeStruct((m, n), BF16)],
        scratch_shapes=[pltpu.VMEM((tm, k), BF16)],
        args=(x, gain.reshape(1, k), w),
        riders=riders,
        semantics=("arbitrary", "arbitrary"),
        name=name,
    )
    return out, casts


def _gmlp_in_body(x_ref, g_ref, w_ref, b_ref, z_ref, ssq_ref, xn_ref, ss_ref):
    j = pl.program_id(1)
    nj = pl.num_programs(1)

    @pl.when(j == 0)
    def _():
        _norm_rows_into(x_ref, g_ref, xn_ref)
        ss_ref[...] = jnp.zeros_like(ss_ref)

    z = jnp.dot(xn_ref[...], w_ref[...], preferred_element_type=F32) + b_ref[...]
    z = _gelu_tanh(z)
    z_ref[...] = z.astype(z_ref.dtype)

    @pl.when(j >= nj // 2)
    def _():
        ss_ref[...] += jnp.sum(z * z, axis=-1, keepdims=True)

    @pl.when(j == nj - 1)
    def _():
        ssq_ref[...] = jnp.broadcast_to(ss_ref[...], ssq_ref.shape)


def _gmlp_in(x, gain, w_in, b_in, later_weights, *, tm=1024, tn=512):
    m, k = x.shape
    n = w_in.shape[1]
    grid = (m // tm, n // tn)
    return _hosted_call(
        _gmlp_in_body,
        grid=grid,
        in_specs=[
            pl.BlockSpec((tm, k), lambda i, j: (i, 0)),
            pl.BlockSpec((1, k), lambda i, j: (0, 0)),
            pl.BlockSpec((k, tn), lambda i, j: (0, j)),
            pl.BlockSpec((1, tn), lambda i, j: (0, j)),
        ],
        out_specs=[
            pl.BlockSpec((tm, tn), lambda i, j: (i, j)),
            pl.BlockSpec((tm, LANES), lambda i, j: (i, 0)),
        ],
        out_shape=[
            jax.ShapeDtypeStruct((m, n), BF16),
            jax.ShapeDtypeStruct((m, LANES), F32),
        ],
        scratch_shapes=[pltpu.VMEM((tm, k), BF16), pltpu.VMEM((tm, 1), F32)],
        args=(x, gain.reshape(1, k), w_in, b_in.reshape(1, n)),
        riders=[_whole_rider(w, grid[0] * grid[1]) for w in later_weights],
        semantics=("arbitrary", "arbitrary"),
        name="gmlp_in",
    )


def _gmlp_out_body(u_ref, v_ref, ssq_ref, gv_ref, ws_ref, bs_ref, x_ref, g_ref, wo_ref,
                   o_ref, acc_ref, gated_ref):
    grp = pl.program_id(1)
    tm = u_ref.shape[0]

    @pl.when(grp == 0)
    def _():
        acc_ref[...] = jnp.zeros_like(acc_ref)

    row = lax.broadcasted_iota(jnp.int32, (CHUNK, CHUNK), 0)
    col = lax.broadcasted_iota(jnp.int32, (CHUNK, CHUNK), 1)
    ws = jnp.where(row >= col, ws_ref[0], 0.0).astype(BF16)
    r = lax.rsqrt(ssq_ref[:, 0:1] * (1.0 / GMLP_HALF) + EPS)
    for c in range(tm // CHUNK):
        rows = slice(c * CHUNK, (c + 1) * CHUNK)
        vn = (v_ref[rows, :].astype(F32) * r[rows, :]) * gv_ref[...]
        mixed = jnp.dot(ws, vn.astype(BF16), preferred_element_type=F32) + bs_ref[0]
        gated_ref[rows, :] = (u_ref[rows, :].astype(F32) * mixed).astype(BF16)
    acc_ref[...] += jnp.dot(gated_ref[...], wo_ref[...], preferred_element_type=F32)

    @pl.when(grp == pl.num_programs(1) - 1)
    def _():
        _residual_norm_rows(x_ref, acc_ref, g_ref, o_ref)


def _gmlp_out(z, ssq, norm_v, w_s, b_s, x, gain, w_out, riders, *, tm=512):
    m, d = x.shape
    gc = GMLP_GROUP_CH
    (out,), casts = _hosted_call(
        _gmlp_out_body,
        grid=(m // tm, GMLP_GROUPS),
        in_specs=[
            pl.BlockSpec((tm, gc), lambda i, g: (i, g)),
            pl.BlockSpec((tm, gc), lambda i, g: (i, GMLP_GROUPS + g)),
            pl.BlockSpec((tm, LANES), lambda i, g: (i, 0)),
            pl.BlockSpec((1, gc), lambda i, g: (0, g)),
            pl.BlockSpec((1, CHUNK, CHUNK), lambda i, g: (g, 0, 0)),
            pl.BlockSpec((1, CHUNK, 1), lambda i, g: (g, 0, 0)),
            pl.BlockSpec((tm, d), lambda i, g: (i, 0)),
            pl.BlockSpec((1, d), lambda i, g: (0, 0)),
            pl.BlockSpec((gc, d), lambda i, g: (g, 0)),
        ],
        out_specs=[pl.BlockSpec((tm, d), lambda i, g: (i, 0))],
        out_shape=[jax.ShapeDtypeStruct((m, d), F32)],
        scratch_shapes=[pltpu.VMEM((tm, d), F32), pltpu.VMEM((tm, gc), BF16)],
        args=(z, z, ssq, norm_v.reshape(1, GMLP_HALF), w_s, b_s[:, :, None], x,
              gain.reshape(1, d), w_out),
        riders=riders,
        semantics=("arbitrary", "arbitrary"),
        name="gmlp_out",
    )
    return out, casts


def _swiglu_step(xn, wg, wu, wd):
    g = jnp.dot(xn, wg, preferred_element_type=F32)
    u = jnp.dot(xn, wu, preferred_element_type=F32)
    a = (g * jax.nn.sigmoid(g)) * u
    return jnp.dot(a.astype(BF16), wd, preferred_element_type=F32)


def _ffn_body(x_ref, g_in_ref, wg_ref, wu_ref, wd_ref, g_out_ref, o_ref, xn_ref, acc_ref):
    j = pl.program_id(1)

    @pl.when(j == 0)
    def _():
        _norm_rows_into(x_ref, g_in_ref, xn_ref)
        acc_ref[...] = jnp.zeros_like(acc_ref)

    acc_ref[...] += _swiglu_step(xn_ref[...], wg_ref[...], wu_ref[...], wd_ref[...])

    @pl.when(j == pl.num_programs(1) - 1)
    def _():
        _residual_norm_rows(x_ref, acc_ref, g_out_ref, o_ref)


def _ffn_dense(x, g_in, w_gu, w_down, g_out, riders, *, tm=512, tf=512):
    m, d = x.shape
    nj = D_FF // tf
    (out,), casts = _hosted_call(
        _ffn_body,
        grid=(m // tm, nj),
        in_specs=[
            pl.BlockSpec((tm, d), lambda i, j: (i, 0)),
            pl.BlockSpec((1, d), lambda i, j: (0, 0)),
            pl.BlockSpec((d, tf), lambda i, j: (0, j)),
            pl.BlockSpec((d, tf), lambda i, j: (0, nj + j)),
            pl.BlockSpec((tf, d), lambda i, j: (j, 0)),
            pl.BlockSpec((1, d), lambda i, j: (0, 0)),
        ],
        out_specs=[pl.BlockSpec((tm, d), lambda i, j: (i, 0))],
        out_shape=[jax.ShapeDtypeStruct((m, d), F32)],
        scratch_shapes=[pltpu.VMEM((tm, d), BF16), pltpu.VMEM((tm, d), F32)],
        args=(x, g_in.reshape(1, d), w_gu, w_gu, w_down, g_out.reshape(1, d)),
        riders=riders,
        semantics=("arbitrary", "arbitrary"),
        name="ffn_dense",
    )
    return out, casts


SUB = 4
ATTN_UNROLL = 8


def _deinterleave(src_ref, dst_ref, span):
    part = span // SUB
    for base in range(0, src_ref.shape[0], span):
        for r in range(SUB):
            dst_ref[base + r * part: base + (r + 1) * part, :] = (
                src_ref[pl.ds(base + r, part, stride=SUB), :].astype(dst_ref.dtype))


def _interleave(src_ref, dst_ref, span):
    part = span // SUB
    for base in range(0, src_ref.shape[0], span):
        for r in range(SUB):
            dst_ref[pl.ds(base + r, part, stride=SUB), :] = (
                src_ref[base + r * part: base + (r + 1) * part, :])


def _window_attention(q_ref, k_ref, v_ref, slope, dilation, blocks_per_seq,
                      bias_c_ref, bias_p_ref, pc_ref, pp_ref, den_ref, o_ref, lse_ref):
    jq = lax.broadcasted_iota(jnp.int32, (BLOCK, BLOCK), 0)
    kk = lax.broadcasted_iota(jnp.int32, (BLOCK, BLOCK), 1)
    dist_c = ((jq - kk) * dilation).astype(F32)
    dist_p = ((BLOCK + jq - kk) * dilation).astype(F32)
    bias_c_ref[...] = jnp.where(kk <= jq, -(slope * dist_c), NEG)
    bias_p_ref[...] = jnp.where(kk >= jq, -(slope * dist_p), NEG)
    scale = HEAD_DIM ** -0.5
    contract_last = (((1,), (1,)), ((), ()))
    with_prev = blocks_per_seq > 1

    def block_rows(n):
        rows = pl.ds(pl.multiple_of(n * BLOCK, BLOCK), BLOCK)
        prev = pl.ds(pl.multiple_of(jnp.maximum(n - 1, 0) * BLOCK, BLOCK), BLOCK)
        return rows, prev

    def probabilities(n, carry):
        rows, prev = block_rows(n)
        q = q_ref[rows, :]
        s_c = lax.dot_general(q, k_ref[rows, :], contract_last, preferred_element_type=F32)
        s_c = s_c * scale + bias_c_ref[...]
        if with_prev:
            has_prev = (n % blocks_per_seq) != 0
            s_p = lax.dot_general(q, k_ref[prev, :], contract_last, preferred_element_type=F32)
            s_p = jnp.where(has_prev, s_p * scale + bias_p_ref[...], NEG)
            m = jnp.max(jnp.maximum(s_c, s_p), axis=-1, keepdims=True)
            p_c = jnp.exp(s_c - m)
            p_p = jnp.exp(s_p - m)
            den = jnp.sum(p_c + p_p, axis=-1, keepdims=True)
            pp_ref[rows, :] = p_p.astype(BF16)
        else:
            m = jnp.max(s_c, axis=-1, keepdims=True)
            p_c = jnp.exp(s_c - m)
            den = jnp.sum(p_c, axis=-1, keepdims=True)
        pc_ref[rows, :] = p_c.astype(BF16)
        den_ref[rows, :] = den
        lse_ref[rows, :] = jnp.broadcast_to(m + jnp.log(den), (BLOCK, LANES))
        return carry

    def values(n, carry):
        rows, prev = block_rows(n)
        o = jnp.dot(pc_ref[rows, :], v_ref[rows, :], preferred_element_type=F32)
        if with_prev:
            o = o + jnp.dot(pp_ref[rows, :], v_ref[prev, :], preferred_element_type=F32)
        o_ref[rows, :] = o / den_ref[rows, :]
        return carry

    n_blocks = q_ref.shape[0] // BLOCK
    lax.fori_loop(0, n_blocks, probabilities, 0, unroll=ATTN_UNROLL)
    lax.fori_loop(0, n_blocks, values, 0, unroll=ATTN_UNROLL)


def _attn_body(base_ref, q1_ref, q4_ref, q16_ref, k_ref, v_ref, out_ref,
               stage_ref, tmp_ref, k4f_ref, v4f_ref,
               qp_ref, k4_ref, v4_ref, k16_ref, v16_ref,
               o1_ref, l1_ref, o4_ref, l4_ref, o16_ref, l16_ref, op_ref, lp_ref,
               bias_c_ref, bias_p_ref, pc_ref, pp_ref, den_ref):
    seq = k_ref.shape[0]
    base = base_ref[pl.program_id(1)]
    (_, d1), (_, d4), (_, d16) = DILATED_GROUPS

    stage_ref[...] = k_ref[...].astype(F32)
    _deinterleave(stage_ref, k4f_ref, seq)
    k4_ref[...] = k4f_ref[...].astype(BF16)
    _deinterleave(k4f_ref, k16_ref, seq // SUB)
    stage_ref[...] = v_ref[...].astype(F32)
    _deinterleave(stage_ref, v4f_ref, seq)
    v4_ref[...] = v4f_ref[...].astype(BF16)
    _deinterleave(v4f_ref, v16_ref, seq // SUB)

    _window_attention(q1_ref, k_ref, v_ref, base / d1, d1, seq // BLOCK,
                      bias_c_ref, bias_p_ref, pc_ref, pp_ref, den_ref, o1_ref, l1_ref)

    stage_ref[...] = q4_ref[...].astype(F32)
    _deinterleave(stage_ref, qp_ref, seq)
    _window_attention(qp_ref, k4_ref, v4_ref, base / d4, d4, seq // d4 // BLOCK,
                      bias_c_ref, bias_p_ref, pc_ref, pp_ref, den_ref, op_ref, lp_ref)
    _interleave(op_ref, o4_ref, seq)
    _interleave(lp_ref, l4_ref, seq)

    stage_ref[...] = q16_ref[...].astype(F32)
    _deinterleave(stage_ref, tmp_ref, seq)
    _deinterleave(tmp_ref, qp_ref, seq // SUB)
    _window_attention(qp_ref, k16_ref, v16_ref, base / d16, d16, seq // d16 // BLOCK,
                      bias_c_ref, bias_p_ref, pc_ref, pp_ref, den_ref, op_ref, lp_ref)
    _interleave(op_ref, tmp_ref, seq // SUB)
    _interleave(tmp_ref, o16_ref, seq)
    _interleave(lp_ref, tmp_ref, seq // SUB)
    _interleave(tmp_ref, l16_ref, seq)

    chunk = 2 * BLOCK

    def merge(c, carry):
        rows = pl.ds(pl.multiple_of(c * chunk, chunk), chunk)
        la, lb, lc = l1_ref[rows, :], l4_ref[rows, :], l16_ref[rows, :]
        mx = jnp.maximum(jnp.maximum(la, lb), lc)
        ea, eb, ec = jnp.exp(la - mx), jnp.exp(lb - mx), jnp.exp(lc - mx)
        den = ea + eb + ec
        o = (ea / den) * o1_ref[rows, :] + (eb / den) * o4_ref[rows, :] + (ec / den) * o16_ref[rows, :]
        out_ref[rows, :] = o.astype(out_ref.dtype)
        return carry
    lax.fori_loop(0, seq // chunk, merge, 0)


def _attention(base, q, kv, batch, seq, riders):
    for window, dilation in DILATED_GROUPS:
        assert window // dilation == BLOCK, "keys per query must span exactly one previous block"
    assert [d for _, d in DILATED_GROUPS] == [1, SUB, SUB * SUB]
    assert seq % (SUB * SUB * BLOCK) == 0
    h = N_KV_HEADS
    blk = (seq, HEAD_DIM)
    f32buf = pltpu.VMEM((seq, HEAD_DIM), F32)
    bf16buf = pltpu.VMEM((seq, HEAD_DIM), BF16)
    (out,), casts = _hosted_call(
        _attn_body,
        num_scalar_prefetch=1,
        grid=(batch, h),
        in_specs=[
            pl.BlockSpec(blk, lambda b, hh, base: (b, hh)),
            pl.BlockSpec(blk, lambda b, hh, base: (b, h + hh)),
            pl.BlockSpec(blk, lambda b, hh, base: (b, 2 * h + hh)),
            pl.BlockSpec(blk, lambda b, hh, base: (b, hh)),
            pl.BlockSpec(blk, lambda b, hh, base: (b, h + hh)),
        ],
        out_specs=[pl.BlockSpec(blk, lambda b, hh, base: (b, hh))],
        out_shape=[jax.ShapeDtypeStruct((batch * seq, D_MODEL), BF16)],
        scratch_shapes=[f32buf] * 4 + [bf16buf] * 5 + [f32buf] * 8
                       + [pltpu.VMEM((BLOCK, BLOCK), F32)] * 2
                       + [bf16buf] * 2 + [pltpu.VMEM((seq, 1), F32)],
        args=(base, q, q, q, kv, kv),
        riders=riders,
        semantics=("arbitrary", "arbitrary"),
        name="attn",
    )
    return out, casts


def _attn_out_body(o_ref, x_ref, g_ref, wo_ref, out_ref, acc_ref):
    acc_ref[...] = jnp.dot(o_ref[...], wo_ref[...], preferred_element_type=F32)
    _residual_norm_rows(x_ref, acc_ref, g_ref, out_ref)


def _attn_out(o, x, gain, w_o, *, tm=512):
    m, d = x.shape
    row_blk = pl.BlockSpec((tm, d), lambda i: (i, 0))
    return pl.pallas_call(
        _attn_out_body,
        grid=(m // tm,),
        in_specs=[row_blk, row_blk,
                  pl.BlockSpec((1, d), lambda i: (0, 0)),
                  pl.BlockSpec((d, d), lambda i: (0, 0))],
        out_specs=row_blk,
        out_shape=jax.ShapeDtypeStruct((m, d), F32),
        scratch_shapes=[pltpu.VMEM((tm, d), F32)],
        compiler_params=_params("parallel"),
        name="attn_out",
    )(o, x, gain.reshape(1, d), w_o)


def _route_body(x_ref, g_ref, wr_ref, o_ref, xn_ref):
    _norm_rows_into(x_ref, g_ref, xn_ref)
    logits = jnp.dot(xn_ref[...], wr_ref[...], preferred_element_type=F32,
                     precision=lax.Precision.HIGHEST)
    lane = lax.broadcasted_iota(jnp.int32, logits.shape, 1)
    logits = jnp.where(lane < N_EXPERTS, logits, -jnp.inf)
    m1 = jnp.max(logits, axis=-1, keepdims=True)
    i1 = jnp.min(jnp.where(logits == m1, lane, LANES), axis=-1, keepdims=True)
    rest = jnp.where(lane == i1, -jnp.inf, logits)
    m2 = jnp.max(rest, axis=-1, keepdims=True)
    i2 = jnp.min(jnp.where(rest == m2, lane, LANES), axis=-1, keepdims=True)
    t = jnp.exp(m2 - m1)
    den = 1.0 + t
    out = jnp.where(lane == 0, i1.astype(F32), 0.0)
    out = jnp.where(lane == 1, i2.astype(F32), out)
    out = jnp.where(lane == 2, 1.0 / den, out)
    out = jnp.where(lane == 3, t / den, out)
    o_ref[...] = out


def _moe_route(x, gain, w_router, *, tm=512):
    m, d = x.shape
    wr = jnp.zeros((d, LANES), F32).at[:, :N_EXPERTS].set(w_router)
    return pl.pallas_call(
        _route_body,
        grid=(m // tm,),
        in_specs=[pl.BlockSpec((tm, d), lambda i: (i, 0)),
                  pl.BlockSpec((1, d), lambda i: (0, 0)),
                  pl.BlockSpec((d, LANES), lambda i: (0, 0))],
        out_specs=[pl.BlockSpec((tm, LANES), lambda i: (i, 0)),
                   pl.BlockSpec((tm, d), lambda i: (i, 0))],
        out_shape=[jax.ShapeDtypeStruct((m, LANES), F32),
                   jax.ShapeDtypeStruct((m, d), F32)],
        compiler_params=_params("parallel"),
        name="moe_route",
    )(x, gain.reshape(1, d), wr)


def _row_copy(src_hbm, row, dst_ref, r, sem):
    return pltpu.make_async_copy(src_hbm.at[pl.ds(row, 1), :], dst_ref.at[pl.ds(r, 1), :], sem)


GATHER_UNROLL = 8


def _start_row_gather(idx_ref, base, src_hbm, dst_ref, sem):
    def start(r, carry):
        _row_copy(src_hbm, idx_ref[base + r], dst_ref, r, sem).start()
        return carry
    lax.fori_loop(0, dst_ref.shape[0], start, 0, unroll=GATHER_UNROLL)


def _wait_row_gather(idx_ref, base, src_hbm, dst_ref, sem):
    def wait(r, carry):
        _row_copy(src_hbm, idx_ref[base + r], dst_ref, r, sem).wait()
        return carry
    lax.fori_loop(0, dst_ref.shape[0], wait, 0, unroll=GATHER_UNROLL)


def _experts_body(tok_ref, bexp_ref, nused_ref, hn_hbm, gate_ref, wg_ref, wu_ref, wd_ref,
                  ys_ref, xg_ref, xn_ref, acc_ref, sem):
    i = pl.program_id(0)
    j = pl.program_id(1)
    last = pl.num_programs(1) - 1
    n_used = nused_ref[0]
    used = i < n_used

    @pl.when(jnp.logical_and(used, j == 0))
    def _():
        @pl.when(i == 0)
        def _():
            _start_row_gather(tok_ref, 0, hn_hbm, xg_ref, sem)
        _wait_row_gather(tok_ref, i * MOE_TM, hn_hbm, xg_ref, sem)
        xn_ref[...] = xg_ref[...].astype(BF16)
        acc_ref[...] = jnp.zeros_like(acc_ref)

    @pl.when(jnp.logical_and(i + 1 < n_used, j == 1))
    def _():
        _start_row_gather(tok_ref, (i + 1) * MOE_TM, hn_hbm, xg_ref, sem)

    @pl.when(used)
    def _():
        acc_ref[...] += _swiglu_step(xn_ref[...], wg_ref[0], wu_ref[0], wd_ref[0])

    @pl.when(jnp.logical_and(used, j == last))
    def _():
        ys_ref[...] = acc_ref[...] * gate_ref[...]

    @pl.when(jnp.logical_and(jnp.logical_not(used), j == last))
    def _():
        ys_ref[...] = jnp.zeros_like(ys_ref)


def _moe_experts(tok_pad, blk_expert, n_used, hn_tiles, gate_pad, w_gu, w_down):
    d = D_MODEL
    n_blocks = tok_pad.shape[0] // MOE_TM
    nj = D_FF // MOE_TF

    def col(i, j, nused):
        return jnp.where(i < nused[0], j, nj - 1)

    return pl.pallas_call(
        _experts_body,
        grid_spec=pltpu.PrefetchScalarGridSpec(
            num_scalar_prefetch=3,
            grid=(n_blocks, nj),
            in_specs=[
                pl.BlockSpec(memory_space=pl.ANY),
                pl.BlockSpec((MOE_TM, 1), lambda i, j, tok, be, nu: (i, 0)),
                pl.BlockSpec((1, d, MOE_TF), lambda i, j, tok, be, nu: (be[i], 0, col(i, j, nu))),
                pl.BlockSpec((1, d, MOE_TF), lambda i, j, tok, be, nu: (be[i], 0, nj + col(i, j, nu))),
                pl.BlockSpec((1, MOE_TF, d), lambda i, j, tok, be, nu: (be[i], col(i, j, nu), 0)),
            ],
            out_specs=pl.BlockSpec((MOE_TM, d), lambda i, j, tok, be, nu: (i, 0)),
            scratch_shapes=[pltpu.VMEM((MOE_TM, d), F32), pltpu.VMEM((MOE_TM, d), BF16),
                            pltpu.VMEM((MOE_TM, d), F32), pltpu.SemaphoreType.DMA],
        ),
        out_shape=jax.ShapeDtypeStruct((n_blocks * MOE_TM, d), F32),
        compiler_params=_params("arbitrary", "arbitrary"),
        name="moe_experts",
    )(tok_pad, blk_expert, n_used, hn_tiles, gate_pad, w_gu, w_gu, w_down)


def _combine_body(p0_ref, p1_ref, ys_hbm, x_ref, g_ref, o_ref, a_ref, b_ref, sem):
    tm = x_ref.shape[0]
    base = pl.program_id(0) * tm
    _start_row_gather(p0_ref, base, ys_hbm, a_ref, sem.at[0])
    _start_row_gather(p1_ref, base, ys_hbm, b_ref, sem.at[1])
    _wait_row_gather(p0_ref, base, ys_hbm, a_ref, sem.at[0])
    _wait_row_gather(p1_ref, base, ys_hbm, b_ref, sem.at[1])
    a_ref[...] = a_ref[...] + b_ref[...]
    _residual_norm_rows(x_ref, a_ref, g_ref, o_ref)


def _moe_combine(pos0, pos1, ys, x, gain, *, tm=256):
    m, d = x.shape
    return pl.pallas_call(
        _combine_body,
        grid_spec=pltpu.PrefetchScalarGridSpec(
            num_scalar_prefetch=2,
            grid=(m // tm,),
            in_specs=[
                pl.BlockSpec(memory_space=pl.ANY),
                pl.BlockSpec((tm, d), lambda i, p0, p1: (i, 0)),
                pl.BlockSpec((1, d), lambda i, p0, p1: (0, 0)),
            ],
            out_specs=pl.BlockSpec((tm, d), lambda i, p0, p1: (i, 0)),
            scratch_shapes=[pltpu.VMEM((tm, d), F32), pltpu.VMEM((tm, d), F32),
                            pltpu.SemaphoreType.DMA((2,))],
        ),
        out_shape=jax.ShapeDtypeStruct((m, d), F32),
        compiler_params=_params("arbitrary"),
        name="moe_combine",
    )(pos0, pos1, ys, x, gain.reshape(1, d))


def _dispatch_plan(route):
    n = route.shape[0]
    experts = route[:, :TOP_K].astype(jnp.int32).reshape(-1)
    gates = route[:, TOP_K:2 * TOP_K].reshape(-1)
    onehot = (experts[:, None] == jnp.arange(N_EXPERTS)[None, :]).astype(jnp.int32)
    rank = jnp.take_along_axis(jnp.cumsum(onehot, axis=0) - onehot, experts[:, None], axis=1)[:, 0]
    counts = jnp.sum(onehot, axis=0)
    blocks_per_expert = (counts + MOE_TM - 1) // MOE_TM
    block_end = jnp.cumsum(blocks_per_expert)
    block_start = block_end - blocks_per_expert
    dest = block_start[experts] * MOE_TM + rank
    n_blocks = (n * TOP_K) // MOE_TM + N_EXPERTS
    tok_pad = jnp.zeros((n_blocks * MOE_TM,), jnp.int32).at[dest].set(jnp.arange(n * TOP_K) // TOP_K)
    gate_pad = jnp.zeros((n_blocks * MOE_TM,), F32).at[dest].set(gates)
    blk_expert = jnp.clip(jnp.searchsorted(block_end, jnp.arange(n_blocks), side='right'),
                          0, N_EXPERTS - 1).astype(jnp.int32)
    n_used = block_end[-1].astype(jnp.int32)
    blk_expert = jnp.where(jnp.arange(n_blocks) < n_used, blk_expert,
                           blk_expert[jnp.maximum(n_used - 1, 0)])
    pos = dest.reshape(n, TOP_K)
    return tok_pad, gate_pad[:, None], blk_expert, n_used.reshape(1), pos[:, 0], pos[:, 1]


def kernel(x, norm_gains, a_w_in, a_b_in, a_norm_v, a_w_s, a_b_s, a_w_out, kv_norm, w_kv,
           b_w_q, b_w_o, ffn_w_gu, ffn_w_down, moe_router, moe_w_gu, moe_w_down):
    batch, seq, d = x.shape
    h = x.reshape(batch * seq, d)
    bf = lambda w: w.astype(BF16)

    g = norm_gains[0]
    later = [a_w_out[0], ffn_w_gu[0], ffn_w_down[0], w_kv, b_w_q[0], b_w_o[0]]
    (z, ssq), (w_out, w_gu, w_down, w_kv16, w_q, w_o) = _gmlp_in(
        h, g[0], bf(a_w_in[0]), a_b_in[0], later)
    e_gu = moe_w_gu[0].reshape(N_EXPERTS * d, 2 * D_FF)
    e_dn = moe_w_down[0].reshape(N_EXPERTS * D_FF, d)
    gu16 = jnp.zeros(e_gu.shape, BF16)
    dn16 = jnp.zeros(e_dn.shape, BF16)

    h, (dn16,) = _gmlp_out(z, ssq, a_norm_v[0], a_w_s[0], a_b_s[0], h, g[1], w_out,
                           [_part_rider(e_dn, dn16, 40960, 128, 128)])
    h, (gu16, dn16) = _ffn_dense(h, g[2], w_gu, w_down, g[3],
                                 [_part_rider(e_gu, gu16, 0, 224, 32),
                                  _part_rider(e_dn, dn16, 26624, 224, 64)])

    g = norm_gains[1]
    kv, (gu16,) = _norm_proj(h, kv_norm, w_kv16, [_part_rider(e_gu, gu16, 7168, 64, 32)],
                             tm=1024, tn=512, name="proj_kv")
    q, (gu16,) = _norm_proj(h, g[0], w_q, [_part_rider(e_gu, gu16, 9216, 96, 32)],
                            tm=1024, tn=512, name="proj_q")
    base = jnp.exp2(-8.0 * jnp.arange(1, N_KV_HEADS + 1, dtype=F32) / N_KV_HEADS)
    o, (gu16, dn16) = _attention(base, q, kv, batch, seq,
                                 [_part_rider(e_gu, gu16, 12288, 64, 64),
                                  _part_rider(e_dn, dn16, 0, 64, 416)])
    h = _attn_out(o, h, g[1], w_o)

    route, hn = _moe_route(h, g[2], moe_router[0])
    tok_pad, gate_pad, blk_expert, n_used, pos0, pos1 = _dispatch_plan(route)
    ys = _moe_experts(tok_pad, blk_expert, n_used, hn, gate_pad,
                      gu16.reshape(N_EXPERTS, d, 2 * D_FF), dn16.reshape(N_EXPERTS, D_FF, d))
    h = _moe_combine(pos0, pos1, ys, h, g[3])
    return h.reshape(batch, seq, d)
```

```python
import functools
from typing import NamedTuple

import jax
import jax.numpy as jnp
from jax import lax
from jax.experimental import pallas as pl
from jax.experimental.pallas import tpu as pltpu

F32 = jnp.float32
BF16 = jnp.bfloat16

D_MODEL = 2048
EPS = 1e-6
CHUNK = 128
GMLP_HALF = 2 * D_MODEL
GMLP_GROUPS = 8
GMLP_GROUP_CH = GMLP_HALF // GMLP_GROUPS
HEAD_DIM = 128
N_KV_HEADS = D_MODEL // HEAD_DIM
DILATED_GROUPS = ((128, 1), (512, 4), (2048, 16))
BLOCK = 128
NEG = -1e30
D_FF = 7168
N_EXPERTS = 8
TOP_K = 2

LANES = 128
VMEM_LIMIT_BYTES = 56 * 1024 * 1024

NORM_ROWS = 128
EPILOGUE_ROWS = 128
MOE_TM = 1024
MOE_TF = 512


def _params(*sem):
    return pltpu.CompilerParams(dimension_semantics=sem, vmem_limit_bytes=VMEM_LIMIT_BYTES)


def _rms_scale(x):
    return lax.rsqrt(jnp.mean(x * x, axis=-1, keepdims=True) + EPS)


def _norm_rows_into(x_ref, g_ref, out_ref):
    def body(c, carry):
        rows = pl.ds(pl.multiple_of(c * NORM_ROWS, NORM_ROWS), NORM_ROWS)
        x = x_ref[rows, :]
        out_ref[rows, :] = ((x * _rms_scale(x)) * g_ref[...]).astype(out_ref.dtype)
        return carry
    lax.fori_loop(0, x_ref.shape[0] // NORM_ROWS, body, 0)


def _residual_norm_rows(res_ref, acc_ref, g_ref, out_ref):
    def body(c, carry):
        rows = pl.ds(pl.multiple_of(c * NORM_ROWS, NORM_ROWS), NORM_ROWS)
        a = acc_ref[rows, :]
        out_ref[rows, :] = res_ref[rows, :] + (a * _rms_scale(a)) * g_ref[...]
        return carry
    lax.fori_loop(0, acc_ref.shape[0] // NORM_ROWS, body, 0)


BF16_SUBLANES = 16


class _Rider(NamedTuple):
    src: jax.Array
    dst: jax.Array | None
    first_row: int
    steps: int
    rows: int


def _whole_rider(w, host_steps):
    total = w.shape[0]
    for steps in range(host_steps, 0, -1):
        if total % steps == 0 and (total // steps) % BF16_SUBLANES == 0:
            return _Rider(w, None, 0, steps, total // steps)
    raise ValueError(f"no chunking of {w.shape} over {host_steps} steps")


def _part_rider(w, dst, first_row, steps, rows):
    assert rows % BF16_SUBLANES == 0 and first_row % rows == 0
    assert first_row + steps * rows <= w.shape[0]
    return _Rider(w, dst, first_row, steps, rows)


def _riding_body(body, n_in, n_out, n_riders, n_alias, *refs):
    ins, refs = refs[:n_in], refs[n_in:]
    r_in, refs = refs[:n_riders], refs[n_riders + n_alias:]
    outs, refs = refs[:n_out], refs[n_out:]
    r_out, scratch = refs[:n_riders], refs[n_riders:]

    def cast_part(part, n_parts):
        for src, dst in zip(r_in, r_out):
            width = src.shape[1] // n_parts
            assert width % LANES == 0 and width * n_parts == src.shape[1]
            cols = slice(part * width, (part + 1) * width)
            dst[:, cols] = src[:, cols].astype(BF16)
    body(cast_part, *ins, *outs, *scratch)


def _hosted_call(body, *, grid, in_specs, out_specs, out_shape, scratch_shapes, args, riders,
                 semantics, name, num_scalar_prefetch=0):
    n_grid = len(grid)
    strides = [1] * n_grid
    for ax in range(n_grid - 2, -1, -1):
        strides[ax] = strides[ax + 1] * grid[ax + 1]
    assert all(r.steps <= strides[0] * grid[0] for r in riders)

    r_in, r_alias, r_out, r_shapes = [], [], [], []
    for r in riders:
        def index(*g, r=r):
            step = sum(g[ax] * strides[ax] for ax in range(n_grid))
            return r.first_row // r.rows + jnp.minimum(step, r.steps - 1), 0
        block = (r.rows, r.src.shape[1])
        r_in.append(pl.BlockSpec(block, index))
        r_out.append(pl.BlockSpec(block, index))
        r_shapes.append(jax.ShapeDtypeStruct(r.src.shape, BF16))
        if r.dst is not None:
            r_alias.append(r.dst)

    n_in = num_scalar_prefetch + len(in_specs)
    n_out = len(out_specs)
    aliases, k = {}, 0
    for ridx, r in enumerate(riders):
        if r.dst is not None:
            aliases[n_in + len(riders) + k] = n_out + ridx
            k += 1
    all_in = list(in_specs) + r_in + [pl.BlockSpec(memory_space=pl.ANY)] * len(r_alias)
    all_out = list(out_specs) + r_out
    kernel_fn = functools.partial(_riding_body, body, n_in, n_out, len(riders), len(r_alias))
    if num_scalar_prefetch:
        spec = dict(grid_spec=pltpu.PrefetchScalarGridSpec(
            num_scalar_prefetch=num_scalar_prefetch, grid=grid, in_specs=all_in,
            out_specs=all_out, scratch_shapes=scratch_shapes))
    else:
        spec = dict(grid=grid, in_specs=all_in, out_specs=all_out, scratch_shapes=scratch_shapes)
    outs = pl.pallas_call(
        kernel_fn,
        out_shape=list(out_shape) + r_shapes,
        input_output_aliases=aliases,
        compiler_params=_params(*semantics),
        name=name,
        **spec,
    )(*args, *[r.src for r in riders], *r_alias)
    return outs[:n_out], outs[n_out:]


def _gelu_tanh(x):
    cdf = 0.5 * (1.0 + jnp.tanh(0.7978845608028654 * (x + 0.044715 * (x * x * x))))
    return x * cdf


def _row_chunks(ref):
    n = ref.shape[0] // EPILOGUE_ROWS
    return [(c, n, slice(c * EPILOGUE_ROWS, (c + 1) * EPILOGUE_ROWS)) for c in range(n)]


def _proj_body(cast_part, x_ref, g_ref, w_ref, o_ref, xn_ref):
    @pl.when(pl.program_id(1) == 0)
    def _():
        _norm_rows_into(x_ref, g_ref, xn_ref)
    for c, n, rows in _row_chunks(xn_ref):
        o_ref[rows, :] = jnp.dot(xn_ref[rows, :], w_ref[...],
                                 preferred_element_type=F32).astype(o_ref.dtype)
        cast_part(c, n)


def _norm_proj(x, gain, w, riders, *, tm, tn, name):
    m, k = x.shape
    n = w.shape[1]
    (out,), casts = _hosted_call(
        _proj_body,
        grid=(m // tm, n // tn),
        in_specs=[
            pl.BlockSpec((tm, k), lambda i, j: (i, 0)),
            pl.BlockSpec((1, k), lambda i, j: (0, 0)),
            pl.BlockSpec((k, tn), lambda i, j: (0, j)),
        ],
        out_specs=[pl.BlockSpec((tm, tn), lambda i, j: (i, j))],
        out_shape=[jax.ShapeDtypeStruct((m, n), BF16)],
        scratch_shapes=[pltpu.VMEM((tm, k), BF16)],
        args=(x, gain.reshape(1, k), w),
        riders=riders,
        semantics=("arbitrary", "arbitrary"),
        name=name,
    )
    return out, casts


def _gmlp_in_body(cast_part, x_ref, g_ref, w_ref, b_ref, z_ref, ssq_ref, xn_ref, ss_ref):
    j = pl.program_id(1)
    nj = pl.num_programs(1)

    @pl.when(j == 0)
    def _():
        _norm_rows_into(x_ref, g_ref, xn_ref)
        ss_ref[...] = jnp.zeros_like(ss_ref)

    is_v = (j >= nj // 2).astype(F32)
    for c, n, rows in _row_chunks(xn_ref):
        z = jnp.dot(xn_ref[rows, :], w_ref[...], preferred_element_type=F32) + b_ref[...]
        z = _gelu_tanh(z)
        z_ref[rows, :] = z.astype(z_ref.dtype)
        ss_ref[rows, :] += is_v * jnp.sum(z * z, axis=-1, keepdims=True)
        cast_part(c, n)

    @pl.when(j == nj - 1)
    def _():
        ssq_ref[...] = jnp.broadcast_to(ss_ref[...], ssq_ref.shape)


def _gmlp_in(x, gain, w_in, b_in, later_weights, *, tm=1024, tn=512):
    m, k = x.shape
    n = w_in.shape[1]
    grid = (m // tm, n // tn)
    return _hosted_call(
        _gmlp_in_body,
        grid=grid,
        in_specs=[
            pl.BlockSpec((tm, k), lambda i, j: (i, 0)),
            pl.BlockSpec((1, k), lambda i, j: (0, 0)),
            pl.BlockSpec((k, tn), lambda i, j: (0, j)),
            pl.BlockSpec((1, tn), lambda i, j: (0, j)),
        ],
        out_specs=[
            pl.BlockSpec((tm, tn), lambda i, j: (i, j)),
            pl.BlockSpec((tm, LANES), lambda i, j: (i, 0)),
        ],
        out_shape=[
            jax.ShapeDtypeStruct((m, n), BF16),
            jax.ShapeDtypeStruct((m, LANES), F32),
        ],
        scratch_shapes=[pltpu.VMEM((tm, k), BF16), pltpu.VMEM((tm, 1), F32)],
        args=(x, gain.reshape(1, k), w_in, b_in.reshape(1, n)),
        riders=[_whole_rider(w, grid[0] * grid[1]) for w in later_weights],
        semantics=("arbitrary", "arbitrary"),
        name="gmlp_in",
    )


def _gmlp_out_body(cast_part, u_ref, v_ref, ssq_ref, gv_ref, ws_ref, bs_ref, x_ref, g_ref, wo_ref,
                   o_ref):
    grp = pl.program_id(1)
    tm = u_ref.shape[0]
    acc_ref = o_ref

    @pl.when(grp == 0)
    def _():
        acc_ref[...] = jnp.zeros_like(acc_ref)

    row = lax.broadcasted_iota(jnp.int32, (CHUNK, CHUNK), 0)
    col = lax.broadcasted_iota(jnp.int32, (CHUNK, CHUNK), 1)
    ws = jnp.where(row >= col, ws_ref[0], 0.0).astype(BF16)
    r = lax.rsqrt(ssq_ref[:, 0:1] * (1.0 / GMLP_HALF) + EPS)
    def project(rows, gated):
        acc_ref[rows, :] += jnp.dot(gated, wo_ref[...], preferred_element_type=F32)

    pending = None
    n = tm // CHUNK
    for c in range(n):
        rows = slice(c * CHUNK, (c + 1) * CHUNK)
        vn = (v_ref[rows, :].astype(F32) * r[rows, :]) * gv_ref[...]
        mixed = jnp.dot(ws, vn.astype(BF16), preferred_element_type=F32) + bs_ref[0]
        gated = (u_ref[rows, :].astype(F32) * mixed).astype(BF16)
        if pending is not None:
            project(*pending)
        pending = (rows, gated)
        cast_part(c, n)
    project(*pending)

    @pl.when(grp == pl.num_programs(1) - 1)
    def _():
        _residual_norm_rows(x_ref, acc_ref, g_ref, o_ref)


def _gmlp_out(z, ssq, norm_v, w_s, b_s, x, gain, w_out, riders, *, tm=1024):
    m, d = x.shape
    gc = GMLP_GROUP_CH
    (out,), casts = _hosted_call(
        _gmlp_out_body,
        grid=(m // tm, GMLP_GROUPS),
        in_specs=[
            pl.BlockSpec((tm, gc), lambda i, g: (i, g)),
            pl.BlockSpec((tm, gc), lambda i, g: (i, GMLP_GROUPS + g)),
            pl.BlockSpec((tm, LANES), lambda i, g: (i, 0)),
            pl.BlockSpec((1, gc), lambda i, g: (0, g)),
            pl.BlockSpec((1, CHUNK, CHUNK), lambda i, g: (g, 0, 0)),
            pl.BlockSpec((1, CHUNK, 1), lambda i, g: (g, 0, 0)),
            pl.BlockSpec((tm, d), lambda i, g: (i, 0)),
            pl.BlockSpec((1, d), lambda i, g: (0, 0)),
            pl.BlockSpec((gc, d), lambda i, g: (g, 0)),
        ],
        out_specs=[pl.BlockSpec((tm, d), lambda i, g: (i, 0))],
        out_shape=[jax.ShapeDtypeStruct((m, d), F32)],
        scratch_shapes=[],
        args=(z, z, ssq, norm_v.reshape(1, GMLP_HALF), w_s, b_s[:, :, None], x,
              gain.reshape(1, d), w_out),
        riders=riders,
        semantics=("arbitrary", "arbitrary"),
        name="gmlp_out",
    )
    return out, casts


def _swiglu_accumulate(xn_ref, wg, wu, wd, acc_ref, cast_part=None, n_chunks=None):
    n = n_chunks or xn_ref.shape[0] // EPILOGUE_ROWS

    def rows(c):
        return slice(c * EPILOGUE_ROWS, (c + 1) * EPILOGUE_ROWS)

    def down(c, a):
        acc_ref[rows(c), :] += jnp.dot(a, wd, preferred_element_type=F32)

    pending = None
    for c in range(n):
        x = xn_ref[rows(c), :]
        g = jnp.dot(x, wg, preferred_element_type=F32)
        u = jnp.dot(x, wu, preferred_element_type=F32)
        a = ((g * jax.nn.sigmoid(g)) * u).astype(BF16)
        if pending is not None:
            down(*pending)
        pending = (c, a)
        if cast_part is not None:
            cast_part(c, n)
    down(*pending)


def _ffn_body(cast_part, x_ref, g_in_ref, wg_ref, wu_ref, wd_ref, g_out_ref, o_ref, xn_ref):
    j = pl.program_id(1)
    acc_ref = o_ref

    @pl.when(j == 0)
    def _():
        _norm_rows_into(x_ref, g_in_ref, xn_ref)
        acc_ref[...] = jnp.zeros_like(acc_ref)

    _swiglu_accumulate(xn_ref, wg_ref[...], wu_ref[...], wd_ref[...], acc_ref, cast_part)

    @pl.when(j == pl.num_programs(1) - 1)
    def _():
        _residual_norm_rows(x_ref, acc_ref, g_out_ref, o_ref)


def _ffn_dense(x, g_in, w_gu, w_down, g_out, riders, *, tm=1024, tf=512):
    m, d = x.shape
    nj = D_FF // tf
    once = dict(pipeline_mode=pl.Buffered(1))
    (out,), casts = _hosted_call(
        _ffn_body,
        grid=(m // tm, nj),
        in_specs=[
            pl.BlockSpec((tm, d), lambda i, j: (i, 0), **once),
            pl.BlockSpec((1, d), lambda i, j: (0, 0)),
            pl.BlockSpec((d, tf), lambda i, j: (0, j)),
            pl.BlockSpec((d, tf), lambda i, j: (0, nj + j)),
            pl.BlockSpec((tf, d), lambda i, j: (j, 0)),
            pl.BlockSpec((1, d), lambda i, j: (0, 0)),
        ],
        out_specs=[pl.BlockSpec((tm, d), lambda i, j: (i, 0), **once)],
        out_shape=[jax.ShapeDtypeStruct((m, d), F32)],
        scratch_shapes=[pltpu.VMEM((tm, d), BF16)],
        args=(x, g_in.reshape(1, d), w_gu, w_gu, w_down, g_out.reshape(1, d)),
        riders=riders,
        semantics=("arbitrary", "arbitrary"),
        name="ffn_dense",
    )
    return out, casts


SUB = 4
ATTN_UNROLL = 8


def _deinterleave(src_ref, dst_ref, span):
    part = span // SUB
    for base in range(0, src_ref.shape[0], span):
        for r in range(SUB):
            dst_ref[base + r * part: base + (r + 1) * part, :] = (
                src_ref[pl.ds(base + r, part, stride=SUB), :].astype(dst_ref.dtype))


def _interleave(src_ref, dst_ref, span):
    part = span // SUB
    for base in range(0, src_ref.shape[0], span):
        for r in range(SUB):
            dst_ref[pl.ds(base + r, part, stride=SUB), :] = (
                src_ref[base + r * part: base + (r + 1) * part, :])


def _window_attention(q_ref, k_ref, v_ref, slope, dilation, blocks_per_seq,
                      bias_c_ref, bias_p_ref, pc_ref, pp_ref, den_ref, o_ref, lse_ref):
    jq = lax.broadcasted_iota(jnp.int32, (BLOCK, BLOCK), 0)
    kk = lax.broadcasted_iota(jnp.int32, (BLOCK, BLOCK), 1)
    dist_c = ((jq - kk) * dilation).astype(F32)
    dist_p = ((BLOCK + jq - kk) * dilation).astype(F32)
    bias_c_ref[...] = jnp.where(kk <= jq, -(slope * dist_c), NEG)
    bias_p_ref[...] = jnp.where(kk >= jq, -(slope * dist_p), NEG)
    scale = HEAD_DIM ** -0.5
    contract_last = (((1,), (1,)), ((), ()))
    with_prev = blocks_per_seq > 1

    def block_rows(n):
        rows = pl.ds(pl.multiple_of(n * BLOCK, BLOCK), BLOCK)
        prev = pl.ds(pl.multiple_of(jnp.maximum(n - 1, 0) * BLOCK, BLOCK), BLOCK)
        return rows, prev

    def probabilities(n, carry):
        rows, prev = block_rows(n)
        q = q_ref[rows, :]
        s_c = lax.dot_general(q, k_ref[rows, :], contract_last, preferred_element_type=F32)
        s_c = s_c * scale + bias_c_ref[...]
        if with_prev:
            has_prev = (n % blocks_per_seq) != 0
            s_p = lax.dot_general(q, k_ref[prev, :], contract_last, preferred_element_type=F32)
            s_p = jnp.where(has_prev, s_p * scale + bias_p_ref[...], NEG)
            m = jnp.max(jnp.maximum(s_c, s_p), axis=-1, keepdims=True)
            p_c = jnp.exp(s_c - m)
            p_p = jnp.exp(s_p - m)
            den = jnp.sum(p_c + p_p, axis=-1, keepdims=True)
            pp_ref[rows, :] = p_p.astype(BF16)
        else:
            m = jnp.max(s_c, axis=-1, keepdims=True)
            p_c = jnp.exp(s_c - m)
            den = jnp.sum(p_c, axis=-1, keepdims=True)
        pc_ref[rows, :] = p_c.astype(BF16)
        den_ref[rows, :] = den
        lse_ref[rows, :] = jnp.broadcast_to(m + jnp.log(den), (BLOCK, LANES))
        return carry

    def values(n, carry):
        rows, prev = block_rows(n)
        o = jnp.dot(pc_ref[rows, :], v_ref[rows, :], preferred_element_type=F32)
        if with_prev:
            o = o + jnp.dot(pp_ref[rows, :], v_ref[prev, :], preferred_element_type=F32)
        o_ref[rows, :] = o / den_ref[rows, :]
        return carry

    n_blocks = q_ref.shape[0] // BLOCK
    lax.fori_loop(0, n_blocks, probabilities, 0, unroll=ATTN_UNROLL)
    lax.fori_loop(0, n_blocks, values, 0, unroll=ATTN_UNROLL)


def _attn_body(cast_part, base_ref, q1_ref, q4_ref, q16_ref, k_ref, v_ref, out_ref,
               stage_ref, tmp_ref, k4f_ref, v4f_ref,
               qp_ref, k4_ref, v4_ref, k16_ref, v16_ref,
               o1_ref, l1_ref, o4_ref, l4_ref, o16_ref, l16_ref, op_ref, lp_ref,
               bias_c_ref, bias_p_ref, pc_ref, pp_ref, den_ref):
    seq = k_ref.shape[0]
    base = base_ref[pl.program_id(1)]
    (_, d1), (_, d4), (_, d16) = DILATED_GROUPS
    cast_part(0, 1)

    stage_ref[...] = k_ref[...].astype(F32)
    _deinterleave(stage_ref, k4f_ref, seq)
    k4_ref[...] = k4f_ref[...].astype(BF16)
    _deinterleave(k4f_ref, k16_ref, seq // SUB)
    stage_ref[...] = v_ref[...].astype(F32)
    _deinterleave(stage_ref, v4f_ref, seq)
    v4_ref[...] = v4f_ref[...].astype(BF16)
    _deinterleave(v4f_ref, v16_ref, seq // SUB)

    _window_attention(q1_ref, k_ref, v_ref, base / d1, d1, seq // BLOCK,
                      bias_c_ref, bias_p_ref, pc_ref, pp_ref, den_ref, o1_ref, l1_ref)

    stage_ref[...] = q4_ref[...].astype(F32)
    _deinterleave(stage_ref, qp_ref, seq)
    _window_attention(qp_ref, k4_ref, v4_ref, base / d4, d4, seq // d4 // BLOCK,
                      bias_c_ref, bias_p_ref, pc_ref, pp_ref, den_ref, op_ref, lp_ref)
    _interleave(op_ref, o4_ref, seq)
    _interleave(lp_ref, l4_ref, seq)

    stage_ref[...] = q16_ref[...].astype(F32)
    _deinterleave(stage_ref, tmp_ref, seq)
    _deinterleave(tmp_ref, qp_ref, seq // SUB)
    _window_attention(qp_ref, k16_ref, v16_ref, base / d16, d16, seq // d16 // BLOCK,
                      bias_c_ref, bias_p_ref, pc_ref, pp_ref, den_ref, op_ref, lp_ref)
    _interleave(op_ref, tmp_ref, seq // SUB)
    _interleave(tmp_ref, o16_ref, seq)
    _interleave(lp_ref, tmp_ref, seq // SUB)
    _interleave(tmp_ref, l16_ref, seq)

    chunk = 2 * BLOCK

    def merge(c, carry):
        rows = pl.ds(pl.multiple_of(c * chunk, chunk), chunk)
        la, lb, lc = l1_ref[rows, :], l4_ref[rows, :], l16_ref[rows, :]
        mx = jnp.maximum(jnp.maximum(la, lb), lc)
        ea, eb, ec = jnp.exp(la - mx), jnp.exp(lb - mx), jnp.exp(lc - mx)
        den = ea + eb + ec
        o = (ea / den) * o1_ref[rows, :] + (eb / den) * o4_ref[rows, :] + (ec / den) * o16_ref[rows, :]
        out_ref[rows, :] = o.astype(out_ref.dtype)
        return carry
    lax.fori_loop(0, seq // chunk, merge, 0)


def _attention(base, q, kv, batch, seq, riders):
    for window, dilation in DILATED_GROUPS:
        assert window // dilation == BLOCK, "keys per query must span exactly one previous block"
    assert [d for _, d in DILATED_GROUPS] == [1, SUB, SUB * SUB]
    assert seq % (SUB * SUB * BLOCK) == 0
    h = N_KV_HEADS
    blk = (seq, HEAD_DIM)
    f32buf = pltpu.VMEM((seq, HEAD_DIM), F32)
    bf16buf = pltpu.VMEM((seq, HEAD_DIM), BF16)
    (out,), casts = _hosted_call(
        _attn_body,
        num_scalar_prefetch=1,
        grid=(batch, h),
        in_specs=[
            pl.BlockSpec(blk, lambda b, hh, base: (b, hh)),
            pl.BlockSpec(blk, lambda b, hh, base: (b, h + hh)),
            pl.BlockSpec(blk, lambda b, hh, base: (b, 2 * h + hh)),
            pl.BlockSpec(blk, lambda b, hh, base: (b, hh)),
            pl.BlockSpec(blk, lambda b, hh, base: (b, h + hh)),
        ],
        out_specs=[pl.BlockSpec(blk, lambda b, hh, base: (b, hh))],
        out_shape=[jax.ShapeDtypeStruct((batch * seq, D_MODEL), BF16)],
        scratch_shapes=[f32buf] * 4 + [bf16buf] * 5 + [f32buf] * 8
                       + [pltpu.VMEM((BLOCK, BLOCK), F32)] * 2
                       + [bf16buf] * 2 + [pltpu.VMEM((seq, 1), F32)],
        args=(base, q, q, q, kv, kv),
        riders=riders,
        semantics=("arbitrary", "arbitrary"),
        name="attn",
    )
    return out, casts


def _attn_out_body(o_ref, x_ref, g_ref, wo_ref, out_ref, acc_ref):
    acc_ref[...] = jnp.dot(o_ref[...], wo_ref[...], preferred_element_type=F32)
    _residual_norm_rows(x_ref, acc_ref, g_ref, out_ref)


def _attn_out(o, x, gain, w_o, *, tm=512):
    m, d = x.shape
    row_blk = pl.BlockSpec((tm, d), lambda i: (i, 0))
    return pl.pallas_call(
        _attn_out_body,
        grid=(m // tm,),
        in_specs=[row_blk, row_blk,
                  pl.BlockSpec((1, d), lambda i: (0, 0)),
                  pl.BlockSpec((d, d), lambda i: (0, 0))],
        out_specs=row_blk,
        out_shape=jax.ShapeDtypeStruct((m, d), F32),
        scratch_shapes=[pltpu.VMEM((tm, d), F32)],
        compiler_params=_params("parallel"),
        name="attn_out",
    )(o, x, gain.reshape(1, d), w_o)


def _route_body(x_ref, g_ref, wr_ref, o_ref, xn_ref):
    _norm_rows_into(x_ref, g_ref, xn_ref)
    logits = jnp.dot(xn_ref[...], wr_ref[...], preferred_element_type=F32,
                     precision=lax.Precision.HIGHEST)
    lane = lax.broadcasted_iota(jnp.int32, logits.shape, 1)
    logits = jnp.where(lane < N_EXPERTS, logits, -jnp.inf)
    m1 = jnp.max(logits, axis=-1, keepdims=True)
    i1 = jnp.min(jnp.where(logits == m1, lane, LANES), axis=-1, keepdims=True)
    rest = jnp.where(lane == i1, -jnp.inf, logits)
    m2 = jnp.max(rest, axis=-1, keepdims=True)
    i2 = jnp.min(jnp.where(rest == m2, lane, LANES), axis=-1, keepdims=True)
    t = jnp.exp(m2 - m1)
    den = 1.0 + t
    out = jnp.where(lane == 0, i1.astype(F32), 0.0)
    out = jnp.where(lane == 1, i2.astype(F32), out)
    out = jnp.where(lane == 2, 1.0 / den, out)
    out = jnp.where(lane == 3, t / den, out)
    o_ref[...] = out


def _moe_route(x, gain, w_router, *, tm=512):
    m, d = x.shape
    wr = jnp.zeros((d, LANES), F32).at[:, :N_EXPERTS].set(w_router)
    return pl.pallas_call(
        _route_body,
        grid=(m // tm,),
        in_specs=[pl.BlockSpec((tm, d), lambda i: (i, 0)),
                  pl.BlockSpec((1, d), lambda i: (0, 0)),
                  pl.BlockSpec((d, LANES), lambda i: (0, 0))],
        out_specs=[pl.BlockSpec((tm, LANES), lambda i: (i, 0)),
                   pl.BlockSpec((tm, d), lambda i: (i, 0))],
        out_shape=[jax.ShapeDtypeStruct((m, LANES), F32),
                   jax.ShapeDtypeStruct((m, d), F32)],
        compiler_params=_params("parallel"),
        name="moe_route",
    )(x, gain.reshape(1, d), wr)


def _row_copy(src_hbm, row, dst_ref, r, sem):
    return pltpu.make_async_copy(src_hbm.at[pl.ds(row, 1), :], dst_ref.at[pl.ds(r, 1), :], sem)


GATHER_UNROLL = 8


def _start_row_gather(idx_ref, base, src_hbm, dst_ref, sem):
    def start(r, carry):
        _row_copy(src_hbm, idx_ref[base + r], dst_ref, r, sem).start()
        return carry
    lax.fori_loop(0, dst_ref.shape[0], start, 0, unroll=GATHER_UNROLL)


def _wait_row_gather(idx_ref, base, src_hbm, dst_ref, sem):
    def wait(r, carry):
        _row_copy(src_hbm, idx_ref[base + r], dst_ref, r, sem).wait()
        return carry
    lax.fori_loop(0, dst_ref.shape[0], wait, 0, unroll=GATHER_UNROLL)


def _experts_body(tok_ref, bexp_ref, rows_ref, nused_ref, hn_hbm, wg_ref, wu_ref, wd_ref,
                  ys_ref, xg_ref, xn_ref, sem):
    i = pl.program_id(0)
    j = pl.program_id(1)
    n_used = nused_ref[0]
    used = i < n_used
    acc_ref = ys_ref

    @pl.when(j == 0)
    def _():
        acc_ref[...] = jnp.zeros_like(acc_ref)

    @pl.when(jnp.logical_and(used, j == 0))
    def _():
        @pl.when(i == 0)
        def _():
            _start_row_gather(tok_ref, 0, hn_hbm, xg_ref, sem)
        _wait_row_gather(tok_ref, i * MOE_TM, hn_hbm, xg_ref, sem)
        xn_ref[...] = xg_ref[...].astype(BF16)

    @pl.when(jnp.logical_and(i + 1 < n_used, j == 1))
    def _():
        _start_row_gather(tok_ref, (i + 1) * MOE_TM, hn_hbm, xg_ref, sem)

    n_rows = rows_ref[i]
    n_max = MOE_TM // EPILOGUE_ROWS
    variants = [n for n in (1, 2, 4, 8, 16) if n < n_max] + [n_max]
    lower = 0
    for n in variants:
        upper = n * EPILOGUE_ROWS
        fits = jnp.logical_and(n_rows > lower, n_rows <= upper)

        @pl.when(jnp.logical_and(used, fits))
        def _(n=n):
            _swiglu_accumulate(xn_ref, wg_ref[0], wu_ref[0], wd_ref[0], acc_ref, n_chunks=n)
        lower = upper


def _moe_experts(tok_pad, blk_expert, blk_rows, n_used, hn, w_gu, w_down):
    d = D_MODEL
    n_blocks = tok_pad.shape[0] // MOE_TM
    nj = D_FF // MOE_TF

    def col(i, j, nused):
        return jnp.where(i < nused[0], j, nj - 1)

    return pl.pallas_call(
        _experts_body,
        grid_spec=pltpu.PrefetchScalarGridSpec(
            num_scalar_prefetch=4,
            grid=(n_blocks, nj),
            in_specs=[
                pl.BlockSpec(memory_space=pl.ANY),
                pl.BlockSpec((1, d, MOE_TF), lambda i, j, tok, be, br, nu: (be[i], 0, col(i, j, nu))),
                pl.BlockSpec((1, d, MOE_TF),
                             lambda i, j, tok, be, br, nu: (be[i], 0, nj + col(i, j, nu))),
                pl.BlockSpec((1, MOE_TF, d), lambda i, j, tok, be, br, nu: (be[i], col(i, j, nu), 0)),
            ],
            out_specs=pl.BlockSpec((MOE_TM, d), lambda i, j, tok, be, br, nu: (i, 0)),
            scratch_shapes=[pltpu.VMEM((MOE_TM, d), F32), pltpu.VMEM((MOE_TM, d), BF16),
                            pltpu.SemaphoreType.DMA],
        ),
        out_shape=jax.ShapeDtypeStruct((n_blocks * MOE_TM, d), F32),
        compiler_params=_params("arbitrary", "arbitrary"),
        name="moe_experts",
    )(tok_pad, blk_expert, blk_rows, n_used, hn, w_gu, w_gu, w_down)


def _combine_body(p0_ref, p1_ref, ys_hbm, route_ref, x_ref, g_ref, o_ref, a_ref, b_ref, sem):
    tm = x_ref.shape[0]
    base = pl.program_id(0) * tm
    _start_row_gather(p0_ref, base, ys_hbm, a_ref, sem.at[0])
    _start_row_gather(p1_ref, base, ys_hbm, b_ref, sem.at[1])
    _wait_row_gather(p0_ref, base, ys_hbm, a_ref, sem.at[0])
    _wait_row_gather(p1_ref, base, ys_hbm, b_ref, sem.at[1])
    gate0 = route_ref[:, TOP_K:TOP_K + 1]
    gate1 = route_ref[:, TOP_K + 1:TOP_K + 2]
    a_ref[...] = a_ref[...] * gate0 + b_ref[...] * gate1
    _residual_norm_rows(x_ref, a_ref, g_ref, o_ref)


def _moe_combine(pos0, pos1, ys, route, x, gain, *, tm=256):
    m, d = x.shape
    return pl.pallas_call(
        _combine_body,
        grid_spec=pltpu.PrefetchScalarGridSpec(
            num_scalar_prefetch=2,
            grid=(m // tm,),
            in_specs=[
                pl.BlockSpec(memory_space=pl.ANY),
                pl.BlockSpec((tm, LANES), lambda i, p0, p1: (i, 0)),
                pl.BlockSpec((tm, d), lambda i, p0, p1: (i, 0)),
                pl.BlockSpec((1, d), lambda i, p0, p1: (0, 0)),
            ],
            out_specs=pl.BlockSpec((tm, d), lambda i, p0, p1: (i, 0)),
            scratch_shapes=[pltpu.VMEM((tm, d), F32), pltpu.VMEM((tm, d), F32),
                            pltpu.SemaphoreType.DMA((2,))],
        ),
        out_shape=jax.ShapeDtypeStruct((m, d), F32),
        compiler_params=_params("arbitrary"),
        name="moe_combine",
    )(pos0, pos1, ys, route, x, gain.reshape(1, d))


def _dispatch_plan(route):
    n = route.shape[0]
    experts = route[:, :TOP_K].astype(jnp.int32).reshape(-1)
    onehot = (experts[:, None] == jnp.arange(N_EXPERTS)[None, :]).astype(jnp.int32)
    rank = jnp.take_along_axis(jnp.cumsum(onehot, axis=0) - onehot, experts[:, None], axis=1)[:, 0]
    counts = jnp.sum(onehot, axis=0)
    blocks_per_expert = (counts + MOE_TM - 1) // MOE_TM
    block_end = jnp.cumsum(blocks_per_expert)
    block_start = block_end - blocks_per_expert
    dest = block_start[experts] * MOE_TM + rank
    n_blocks = (n * TOP_K) // MOE_TM + N_EXPERTS
    tok_pad = jnp.zeros((n_blocks * MOE_TM,), jnp.int32).at[dest].set(jnp.arange(n * TOP_K) // TOP_K)
    blocks = jnp.arange(n_blocks)
    blk_expert = jnp.clip(jnp.searchsorted(block_end, blocks, side='right'),
                          0, N_EXPERTS - 1).astype(jnp.int32)
    blk_rows = jnp.clip(counts[blk_expert] - (blocks - block_start[blk_expert]) * MOE_TM,
                        0, MOE_TM).astype(jnp.int32)
    n_used = block_end[-1].astype(jnp.int32)
    blk_expert = jnp.where(blocks < n_used, blk_expert, blk_expert[jnp.maximum(n_used - 1, 0)])
    pos = dest.reshape(n, TOP_K)
    return tok_pad, blk_expert, blk_rows, n_used.reshape(1), pos[:, 0], pos[:, 1]


def kernel(x, norm_gains, a_w_in, a_b_in, a_norm_v, a_w_s, a_b_s, a_w_out, kv_norm, w_kv,
           b_w_q, b_w_o, ffn_w_gu, ffn_w_down, moe_router, moe_w_gu, moe_w_down):
    batch, seq, d = x.shape
    h = x.reshape(batch * seq, d)
    bf = lambda w: w.astype(BF16)

    g = norm_gains[0]
    later = [a_w_out[0], ffn_w_gu[0], ffn_w_down[0], w_kv, b_w_q[0], b_w_o[0]]
    (z, ssq), (w_out, w_gu, w_down, w_kv16, w_q, w_o) = _gmlp_in(
        h, g[0], bf(a_w_in[0]), a_b_in[0], later)
    e_gu = moe_w_gu[0].reshape(N_EXPERTS * d, 2 * D_FF)
    e_dn = moe_w_down[0].reshape(N_EXPERTS * D_FF, d)
    gu16 = jnp.zeros(e_gu.shape, BF16)
    dn16 = jnp.zeros(e_dn.shape, BF16)

    h, (dn16,) = _gmlp_out(z, ssq, a_norm_v[0], a_w_s[0], a_b_s[0], h, g[1], w_out,
                           [_part_rider(e_dn, dn16, 40960, 64, 256)])
    h, (gu16, dn16) = _ffn_dense(h, g[2], w_gu, w_down, g[3],
                                 [_part_rider(e_gu, gu16, 0, 112, 64),
                                  _part_rider(e_dn, dn16, 26624, 112, 128)])

    g = norm_gains[1]
    kv, (gu16,) = _norm_proj(h, kv_norm, w_kv16, [_part_rider(e_gu, gu16, 7168, 32, 64)],
                             tm=1024, tn=1024, name="proj_kv")
    q, (gu16,) = _norm_proj(h, g[0], w_q, [_part_rider(e_gu, gu16, 9216, 48, 64)],
                            tm=1024, tn=1024, name="proj_q")
    base = jnp.exp2(-8.0 * jnp.arange(1, N_KV_HEADS + 1, dtype=F32) / N_KV_HEADS)
    o, (gu16, dn16) = _attention(base, q, kv, batch, seq,
                                 [_part_rider(e_gu, gu16, 12288, 64, 64),
                                  _part_rider(e_dn, dn16, 0, 64, 416)])
    h = _attn_out(o, h, g[1], w_o)

    route, hn = _moe_route(h, g[2], moe_router[0])
    tok_pad, blk_expert, blk_rows, n_used, pos0, pos1 = _dispatch_plan(route)
    ys = _moe_experts(tok_pad, blk_expert, blk_rows, n_used, hn,
                      gu16.reshape(N_EXPERTS, d, 2 * D_FF), dn16.reshape(N_EXPERTS, D_FF, d))
    h = _moe_combine(pos0, pos1, ys, route, h, g[3])
    return h.reshape(batch, seq, d)
```

```python
import functools
from typing import NamedTuple

import jax
import jax.numpy as jnp
from jax import lax
from jax.experimental import pallas as pl
from jax.experimental.pallas import tpu as pltpu

F32 = jnp.float32
BF16 = jnp.bfloat16

D_MODEL = 2048
EPS = 1e-6
CHUNK = 128
GMLP_HALF = 2 * D_MODEL
GMLP_GROUPS = 8
GMLP_GROUP_CH = GMLP_HALF // GMLP_GROUPS
HEAD_DIM = 128
N_KV_HEADS = D_MODEL // HEAD_DIM
DILATED_GROUPS = ((128, 1), (512, 4), (2048, 16))
BLOCK = 128
NEG = -1e30
D_FF = 7168
N_EXPERTS = 8
TOP_K = 2

LANES = 128
VMEM_LIMIT_BYTES = 56 * 1024 * 1024

NORM_ROWS = 128
EPILOGUE_ROWS = 128
MOE_TM = 1152
MOE_TF = 256


def _params(*sem):
    return pltpu.CompilerParams(dimension_semantics=sem, vmem_limit_bytes=VMEM_LIMIT_BYTES)


def _rms_scale(x):
    return lax.rsqrt(jnp.mean(x * x, axis=-1, keepdims=True) + EPS)


def _norm_rows_into(x_ref, g_ref, out_ref):
    def body(c, carry):
        rows = pl.ds(pl.multiple_of(c * NORM_ROWS, NORM_ROWS), NORM_ROWS)
        x = x_ref[rows, :]
        out_ref[rows, :] = ((x * _rms_scale(x)) * g_ref[...]).astype(out_ref.dtype)
        return carry
    lax.fori_loop(0, x_ref.shape[0] // NORM_ROWS, body, 0)


def _residual_norm_rows(res_ref, acc_ref, g_ref, out_ref):
    def body(c, carry):
        rows = pl.ds(pl.multiple_of(c * NORM_ROWS, NORM_ROWS), NORM_ROWS)
        a = acc_ref[rows, :]
        out_ref[rows, :] = res_ref[rows, :] + (a * _rms_scale(a)) * g_ref[...]
        return carry
    lax.fori_loop(0, acc_ref.shape[0] // NORM_ROWS, body, 0)


BF16_SUBLANES = 16


class _Rider(NamedTuple):
    src: jax.Array
    dst: jax.Array | None
    first_row: int
    steps: int
    rows: int


def _whole_rider(w, host_steps):
    total = w.shape[0]
    for steps in range(host_steps, 0, -1):
        if total % steps == 0 and (total // steps) % BF16_SUBLANES == 0:
            return _Rider(w, None, 0, steps, total // steps)
    raise ValueError(f"no chunking of {w.shape} over {host_steps} steps")


def _part_rider(w, dst, first_row, steps, rows):
    assert rows % BF16_SUBLANES == 0 and first_row % rows == 0
    assert first_row + steps * rows <= w.shape[0]
    return _Rider(w, dst, first_row, steps, rows)


def _riding_body(body, n_in, n_out, n_riders, n_alias, *refs):
    ins, refs = refs[:n_in], refs[n_in:]
    r_in, refs = refs[:n_riders], refs[n_riders + n_alias:]
    outs, refs = refs[:n_out], refs[n_out:]
    r_out, scratch = refs[:n_riders], refs[n_riders:]

    def cast_part(part, n_parts):
        for src, dst in zip(r_in, r_out):
            width = src.shape[1] // n_parts
            assert width % LANES == 0 and width * n_parts == src.shape[1]
            cols = slice(part * width, (part + 1) * width)
            dst[:, cols] = src[:, cols].astype(BF16)
    body(cast_part, *ins, *outs, *scratch)


def _hosted_call(body, *, grid, in_specs, out_specs, out_shape, scratch_shapes, args, riders,
                 semantics, name, num_scalar_prefetch=0):
    n_grid = len(grid)
    strides = [1] * n_grid
    for ax in range(n_grid - 2, -1, -1):
        strides[ax] = strides[ax + 1] * grid[ax + 1]
    assert all(r.steps <= strides[0] * grid[0] for r in riders)

    r_in, r_alias, r_out, r_shapes = [], [], [], []
    for r in riders:
        def index(*g, r=r):
            step = sum(g[ax] * strides[ax] for ax in range(n_grid))
            return r.first_row // r.rows + jnp.minimum(step, r.steps - 1), 0
        block = (r.rows, r.src.shape[1])
        r_in.append(pl.BlockSpec(block, index))
        r_out.append(pl.BlockSpec(block, index))
        r_shapes.append(jax.ShapeDtypeStruct(r.src.shape, BF16))
        if r.dst is not None:
            r_alias.append(r.dst)

    n_in = num_scalar_prefetch + len(in_specs)
    n_out = len(out_specs)
    aliases, k = {}, 0
    for ridx, r in enumerate(riders):
        if r.dst is not None:
            aliases[n_in + len(riders) + k] = n_out + ridx
            k += 1
    all_in = list(in_specs) + r_in + [pl.BlockSpec(memory_space=pl.ANY)] * len(r_alias)
    all_out = list(out_specs) + r_out
    kernel_fn = functools.partial(_riding_body, body, n_in, n_out, len(riders), len(r_alias))
    if num_scalar_prefetch:
        spec = dict(grid_spec=pltpu.PrefetchScalarGridSpec(
            num_scalar_prefetch=num_scalar_prefetch, grid=grid, in_specs=all_in,
            out_specs=all_out, scratch_shapes=scratch_shapes))
    else:
        spec = dict(grid=grid, in_specs=all_in, out_specs=all_out, scratch_shapes=scratch_shapes)
    outs = pl.pallas_call(
        kernel_fn,
        out_shape=list(out_shape) + r_shapes,
        input_output_aliases=aliases,
        compiler_params=_params(*semantics),
        name=name,
        **spec,
    )(*args, *[r.src for r in riders], *r_alias)
    return outs[:n_out], outs[n_out:]


def _gelu_tanh(x):
    cdf = 0.5 * (1.0 + jnp.tanh(0.7978845608028654 * (x + 0.044715 * (x * x * x))))
    return x * cdf


def _row_chunks(ref):
    n = ref.shape[0] // EPILOGUE_ROWS
    return [(c, n, slice(c * EPILOGUE_ROWS, (c + 1) * EPILOGUE_ROWS)) for c in range(n)]


def _proj_body(cast_part, x_ref, g_ref, w_ref, o_ref, xn_ref):
    @pl.when(pl.program_id(1) == 0)
    def _():
        _norm_rows_into(x_ref, g_ref, xn_ref)
    for c, n, rows in _row_chunks(xn_ref):
        o_ref[rows, :] = jnp.dot(xn_ref[rows, :], w_ref[...],
                                 preferred_element_type=F32).astype(o_ref.dtype)
        cast_part(c, n)


def _norm_proj(x, gain, w, riders, *, tm, tn, name):
    m, k = x.shape
    n = w.shape[1]
    (out,), casts = _hosted_call(
        _proj_body,
        grid=(m // tm, n // tn),
        in_specs=[
            pl.BlockSpec((tm, k), lambda i, j: (i, 0)),
            pl.BlockSpec((1, k), lambda i, j: (0, 0)),
            pl.BlockSpec((k, tn), lambda i, j: (0, j)),
        ],
        out_specs=[pl.BlockSpec((tm, tn), lambda i, j: (i, j))],
        out_shape=[jax.ShapeDtypeStruct((m, n), BF16)],
        scratch_shapes=[pltpu.VMEM((tm, k), BF16)],
        args=(x, gain.reshape(1, k), w),
        riders=riders,
        semantics=("arbitrary", "arbitrary"),
        name=name,
    )
    return out, casts


def _gmlp_in_body(cast_part, x_ref, g_ref, w_ref, b_ref, z_ref, ssq_ref, xn_ref, ss_ref):
    j = pl.program_id(1)
    nj = pl.num_programs(1)

    @pl.when(j == 0)
    def _():
        _norm_rows_into(x_ref, g_ref, xn_ref)
        ss_ref[...] = jnp.zeros_like(ss_ref)

    is_v = (j >= nj // 2).astype(F32)
    for c, n, rows in _row_chunks(xn_ref):
        z = jnp.dot(xn_ref[rows, :], w_ref[...], preferred_element_type=F32) + b_ref[...]
        z = _gelu_tanh(z)
        z_ref[rows, :] = z.astype(z_ref.dtype)
        ss_ref[rows, :] += is_v * jnp.sum(z * z, axis=-1, keepdims=True)
        cast_part(c, n)

    @pl.when(j == nj - 1)
    def _():
        ssq_ref[...] = jnp.broadcast_to(ss_ref[...], ssq_ref.shape)


def _gmlp_in(x, gain, w_in, b_in, later_weights, *, tm=1024, tn=512):
    m, k = x.shape
    n = w_in.shape[1]
    grid = (m // tm, n // tn)
    return _hosted_call(
        _gmlp_in_body,
        grid=grid,
        in_specs=[
            pl.BlockSpec((tm, k), lambda i, j: (i, 0)),
            pl.BlockSpec((1, k), lambda i, j: (0, 0)),
            pl.BlockSpec((k, tn), lambda i, j: (0, j)),
            pl.BlockSpec((1, tn), lambda i, j: (0, j)),
        ],
        out_specs=[
            pl.BlockSpec((tm, tn), lambda i, j: (i, j)),
            pl.BlockSpec((tm, LANES), lambda i, j: (i, 0)),
        ],
        out_shape=[
            jax.ShapeDtypeStruct((m, n), BF16),
            jax.ShapeDtypeStruct((m, LANES), F32),
        ],
        scratch_shapes=[pltpu.VMEM((tm, k), BF16), pltpu.VMEM((tm, 1), F32)],
        args=(x, gain.reshape(1, k), w_in, b_in.reshape(1, n)),
        riders=[_whole_rider(w, grid[0] * grid[1]) for w in later_weights],
        semantics=("arbitrary", "arbitrary"),
        name="gmlp_in",
    )


def _gmlp_out_body(cast_part, u_ref, v_ref, ssq_ref, gv_ref, ws_ref, bs_ref, x_ref, g_ref, wo_ref,
                   o_ref):
    grp = pl.program_id(1)
    tm = u_ref.shape[0]
    acc_ref = o_ref

    @pl.when(grp == 0)
    def _():
        acc_ref[...] = jnp.zeros_like(acc_ref)

    row = lax.broadcasted_iota(jnp.int32, (CHUNK, CHUNK), 0)
    col = lax.broadcasted_iota(jnp.int32, (CHUNK, CHUNK), 1)
    ws = jnp.where(row >= col, ws_ref[0], 0.0).astype(BF16)
    r = lax.rsqrt(ssq_ref[:, 0:1] * (1.0 / GMLP_HALF) + EPS)
    def project(rows, gated):
        acc_ref[rows, :] += jnp.dot(gated, wo_ref[...], preferred_element_type=F32)

    pending = None
    n = tm // CHUNK
    for c in range(n):
        rows = slice(c * CHUNK, (c + 1) * CHUNK)
        vn = (v_ref[rows, :].astype(F32) * r[rows, :]) * gv_ref[...]
        mixed = jnp.dot(ws, vn.astype(BF16), preferred_element_type=F32) + bs_ref[0]
        gated = (u_ref[rows, :].astype(F32) * mixed).astype(BF16)
        if pending is not None:
            project(*pending)
        pending = (rows, gated)
        cast_part(c, n)
    project(*pending)

    @pl.when(grp == pl.num_programs(1) - 1)
    def _():
        _residual_norm_rows(x_ref, acc_ref, g_ref, o_ref)


def _gmlp_out(z, ssq, norm_v, w_s, b_s, x, gain, w_out, riders, *, tm=1024):
    m, d = x.shape
    gc = GMLP_GROUP_CH
    (out,), casts = _hosted_call(
        _gmlp_out_body,
        grid=(m // tm, GMLP_GROUPS),
        in_specs=[
            pl.BlockSpec((tm, gc), lambda i, g: (i, g)),
            pl.BlockSpec((tm, gc), lambda i, g: (i, GMLP_GROUPS + g)),
            pl.BlockSpec((tm, LANES), lambda i, g: (i, 0)),
            pl.BlockSpec((1, gc), lambda i, g: (0, g)),
            pl.BlockSpec((1, CHUNK, CHUNK), lambda i, g: (g, 0, 0)),
            pl.BlockSpec((1, CHUNK, 1), lambda i, g: (g, 0, 0)),
            pl.BlockSpec((tm, d), lambda i, g: (i, 0)),
            pl.BlockSpec((1, d), lambda i, g: (0, 0)),
            pl.BlockSpec((gc, d), lambda i, g: (g, 0)),
        ],
        out_specs=[pl.BlockSpec((tm, d), lambda i, g: (i, 0))],
        out_shape=[jax.ShapeDtypeStruct((m, d), F32)],
        scratch_shapes=[],
        args=(z, z, ssq, norm_v.reshape(1, GMLP_HALF), w_s, b_s[:, :, None], x,
              gain.reshape(1, d), w_out),
        riders=riders,
        semantics=("arbitrary", "arbitrary"),
        name="gmlp_out",
    )
    return out, casts


def _swiglu_accumulate(xn_ref, wg, wu, wd, acc_ref, cast_part=None, n_chunks=None):
    n = n_chunks or xn_ref.shape[0] // EPILOGUE_ROWS

    def rows(c):
        return slice(c * EPILOGUE_ROWS, (c + 1) * EPILOGUE_ROWS)

    def down(c, a):
        acc_ref[rows(c), :] += jnp.dot(a, wd(), preferred_element_type=F32)

    pending = None
    for c in range(n):
        x = xn_ref[rows(c), :]
        g = jnp.dot(x, wg(), preferred_element_type=F32)
        u = jnp.dot(x, wu(), preferred_element_type=F32)
        a = ((g * jax.nn.sigmoid(g)) * u).astype(BF16)
        if pending is not None:
            down(*pending)
        pending = (c, a)
        if cast_part is not None:
            cast_part(c, n)
    down(*pending)


def _cast_once(src_ref, dst_ref):
    done = []

    def get():
        if not done:
            dst_ref[...] = src_ref[0].astype(BF16)
            done.append(True)
        return dst_ref[...]
    return get


def _ffn_body(cast_part, x_ref, g_in_ref, wg_ref, wu_ref, wd_ref, g_out_ref, o_ref, xn_ref):
    j = pl.program_id(1)
    acc_ref = o_ref

    @pl.when(j == 0)
    def _():
        _norm_rows_into(x_ref, g_in_ref, xn_ref)
        acc_ref[...] = jnp.zeros_like(acc_ref)

    _swiglu_accumulate(xn_ref, lambda: wg_ref[...], lambda: wu_ref[...], lambda: wd_ref[...],
                       acc_ref, cast_part)

    @pl.when(j == pl.num_programs(1) - 1)
    def _():
        _residual_norm_rows(x_ref, acc_ref, g_out_ref, o_ref)


def _ffn_dense(x, g_in, w_gu, w_down, g_out, riders, *, tm=1024, tf=512):
    m, d = x.shape
    nj = D_FF // tf
    once = dict(pipeline_mode=pl.Buffered(1))
    (out,), casts = _hosted_call(
        _ffn_body,
        grid=(m // tm, nj),
        in_specs=[
            pl.BlockSpec((tm, d), lambda i, j: (i, 0), **once),
            pl.BlockSpec((1, d), lambda i, j: (0, 0)),
            pl.BlockSpec((d, tf), lambda i, j: (0, j)),
            pl.BlockSpec((d, tf), lambda i, j: (0, nj + j)),
            pl.BlockSpec((tf, d), lambda i, j: (j, 0)),
            pl.BlockSpec((1, d), lambda i, j: (0, 0)),
        ],
        out_specs=[pl.BlockSpec((tm, d), lambda i, j: (i, 0), **once)],
        out_shape=[jax.ShapeDtypeStruct((m, d), F32)],
        scratch_shapes=[pltpu.VMEM((tm, d), BF16)],
        args=(x, g_in.reshape(1, d), w_gu, w_gu, w_down, g_out.reshape(1, d)),
        riders=riders,
        semantics=("arbitrary", "arbitrary"),
        name="ffn_dense",
    )
    return out, casts


SUB = 4
ATTN_UNROLL = 8


def _deinterleave(src_ref, dst_ref, span):
    part = span // SUB
    for base in range(0, src_ref.shape[0], span):
        for r in range(SUB):
            dst_ref[base + r * part: base + (r + 1) * part, :] = (
                src_ref[pl.ds(base + r, part, stride=SUB), :].astype(dst_ref.dtype))


def _interleave(src_ref, dst_ref, span):
    part = span // SUB
    for base in range(0, src_ref.shape[0], span):
        for r in range(SUB):
            dst_ref[pl.ds(base + r, part, stride=SUB), :] = (
                src_ref[base + r * part: base + (r + 1) * part, :])


def _window_attention(q_ref, k_ref, v_ref, slope, dilation, blocks_per_seq,
                      bias_c_ref, bias_p_ref, pc_ref, pp_ref, den_ref, o_ref, lse_ref):
    jq = lax.broadcasted_iota(jnp.int32, (BLOCK, BLOCK), 0)
    kk = lax.broadcasted_iota(jnp.int32, (BLOCK, BLOCK), 1)
    dist_c = ((jq - kk) * dilation).astype(F32)
    dist_p = ((BLOCK + jq - kk) * dilation).astype(F32)
    bias_c_ref[...] = jnp.where(kk <= jq, -(slope * dist_c), NEG)
    bias_p_ref[...] = jnp.where(kk >= jq, -(slope * dist_p), NEG)
    scale = HEAD_DIM ** -0.5
    contract_last = (((1,), (1,)), ((), ()))
    with_prev = blocks_per_seq > 1

    def block_rows(n):
        rows = pl.ds(pl.multiple_of(n * BLOCK, BLOCK), BLOCK)
        prev = pl.ds(pl.multiple_of(jnp.maximum(n - 1, 0) * BLOCK, BLOCK), BLOCK)
        return rows, prev

    def probabilities(n, carry):
        rows, prev = block_rows(n)
        q = q_ref[rows, :]
        s_c = lax.dot_general(q, k_ref[rows, :], contract_last, preferred_element_type=F32)
        s_c = s_c * scale + bias_c_ref[...]
        if with_prev:
            has_prev = (n % blocks_per_seq) != 0
            s_p = lax.dot_general(q, k_ref[prev, :], contract_last, preferred_element_type=F32)
            s_p = jnp.where(has_prev, s_p * scale + bias_p_ref[...], NEG)
            m = jnp.max(jnp.maximum(s_c, s_p), axis=-1, keepdims=True)
            p_c = jnp.exp(s_c - m)
            p_p = jnp.exp(s_p - m)
            den = jnp.sum(p_c + p_p, axis=-1, keepdims=True)
            pp_ref[rows, :] = p_p.astype(BF16)
        else:
            m = jnp.max(s_c, axis=-1, keepdims=True)
            p_c = jnp.exp(s_c - m)
            den = jnp.sum(p_c, axis=-1, keepdims=True)
        pc_ref[rows, :] = p_c.astype(BF16)
        den_ref[rows, :] = den
        lse_ref[rows, :] = jnp.broadcast_to(m + jnp.log(den), (BLOCK, LANES))
        return carry

    def values(n, carry):
        rows, prev = block_rows(n)
        o = jnp.dot(pc_ref[rows, :], v_ref[rows, :], preferred_element_type=F32)
        if with_prev:
            o = o + jnp.dot(pp_ref[rows, :], v_ref[prev, :], preferred_element_type=F32)
        o_ref[rows, :] = o / den_ref[rows, :]
        return carry

    n_blocks = q_ref.shape[0] // BLOCK
    lax.fori_loop(0, n_blocks, probabilities, 0, unroll=ATTN_UNROLL)
    lax.fori_loop(0, n_blocks, values, 0, unroll=ATTN_UNROLL)


def _attn_body(cast_part, base_ref, q1_ref, q4_ref, q16_ref, k_ref, v_ref, out_ref,
               stage_ref, tmp_ref, k4f_ref, v4f_ref,
               qp_ref, k4_ref, v4_ref, k16_ref, v16_ref,
               o1_ref, l1_ref, o4_ref, l4_ref, o16_ref, l16_ref, op_ref, lp_ref,
               bias_c_ref, bias_p_ref, pc_ref, pp_ref, den_ref):
    seq = k_ref.shape[0]
    base = base_ref[pl.program_id(1)]
    (_, d1), (_, d4), (_, d16) = DILATED_GROUPS
    cast_part(0, 1)

    stage_ref[...] = k_ref[...].astype(F32)
    _deinterleave(stage_ref, k4f_ref, seq)
    k4_ref[...] = k4f_ref[...].astype(BF16)
    _deinterleave(k4f_ref, k16_ref, seq // SUB)
    stage_ref[...] = v_ref[...].astype(F32)
    _deinterleave(stage_ref, v4f_ref, seq)
    v4_ref[...] = v4f_ref[...].astype(BF16)
    _deinterleave(v4f_ref, v16_ref, seq // SUB)

    _window_attention(q1_ref, k_ref, v_ref, base / d1, d1, seq // BLOCK,
                      bias_c_ref, bias_p_ref, pc_ref, pp_ref, den_ref, o1_ref, l1_ref)

    stage_ref[...] = q4_ref[...].astype(F32)
    _deinterleave(stage_ref, qp_ref, seq)
    _window_attention(qp_ref, k4_ref, v4_ref, base / d4, d4, seq // d4 // BLOCK,
                      bias_c_ref, bias_p_ref, pc_ref, pp_ref, den_ref, op_ref, lp_ref)
    _interleave(op_ref, o4_ref, seq)
    _interleave(lp_ref, l4_ref, seq)

    stage_ref[...] = q16_ref[...].astype(F32)
    _deinterleave(stage_ref, tmp_ref, seq)
    _deinterleave(tmp_ref, qp_ref, seq // SUB)
    _window_attention(qp_ref, k16_ref, v16_ref, base / d16, d16, seq // d16 // BLOCK,
                      bias_c_ref, bias_p_ref, pc_ref, pp_ref, den_ref, op_ref, lp_ref)
    _interleave(op_ref, tmp_ref, seq // SUB)
    _interleave(tmp_ref, o16_ref, seq)
    _interleave(lp_ref, tmp_ref, seq // SUB)
    _interleave(tmp_ref, l16_ref, seq)

    chunk = 2 * BLOCK

    def merge(c, carry):
        rows = pl.ds(pl.multiple_of(c * chunk, chunk), chunk)
        la, lb, lc = l1_ref[rows, :], l4_ref[rows, :], l16_ref[rows, :]
        mx = jnp.maximum(jnp.maximum(la, lb), lc)
        ea, eb, ec = jnp.exp(la - mx), jnp.exp(lb - mx), jnp.exp(lc - mx)
        den = ea + eb + ec
        o = (ea / den) * o1_ref[rows, :] + (eb / den) * o4_ref[rows, :] + (ec / den) * o16_ref[rows, :]
        out_ref[rows, :] = o.astype(out_ref.dtype)
        return carry
    lax.fori_loop(0, seq // chunk, merge, 0)


def _attention(base, q, kv, batch, seq, riders):
    for window, dilation in DILATED_GROUPS:
        assert window // dilation == BLOCK, "keys per query must span exactly one previous block"
    assert [d for _, d in DILATED_GROUPS] == [1, SUB, SUB * SUB]
    assert seq % (SUB * SUB * BLOCK) == 0
    h = N_KV_HEADS
    blk = (seq, HEAD_DIM)
    f32buf = pltpu.VMEM((seq, HEAD_DIM), F32)
    bf16buf = pltpu.VMEM((seq, HEAD_DIM), BF16)
    (out,), casts = _hosted_call(
        _attn_body,
        num_scalar_prefetch=1,
        grid=(batch, h),
        in_specs=[
            pl.BlockSpec(blk, lambda b, hh, base: (b, hh)),
            pl.BlockSpec(blk, lambda b, hh, base: (b, h + hh)),
            pl.BlockSpec(blk, lambda b, hh, base: (b, 2 * h + hh)),
            pl.BlockSpec(blk, lambda b, hh, base: (b, hh)),
            pl.BlockSpec(blk, lambda b, hh, base: (b, h + hh)),
        ],
        out_specs=[pl.BlockSpec(blk, lambda b, hh, base: (b, hh))],
        out_shape=[jax.ShapeDtypeStruct((batch * seq, D_MODEL), BF16)],
        scratch_shapes=[f32buf] * 4 + [bf16buf] * 5 + [f32buf] * 8
                       + [pltpu.VMEM((BLOCK, BLOCK), F32)] * 2
                       + [bf16buf] * 2 + [pltpu.VMEM((seq, 1), F32)],
        args=(base, q, q, q, kv, kv),
        riders=riders,
        semantics=("arbitrary", "arbitrary"),
        name="attn",
    )
    return out, casts


def _attn_out_body(o_ref, x_ref, g_ref, wo_ref, out_ref, acc_ref):
    acc_ref[...] = jnp.dot(o_ref[...], wo_ref[...], preferred_element_type=F32)
    _residual_norm_rows(x_ref, acc_ref, g_ref, out_ref)


def _attn_out(o, x, gain, w_o, *, tm=512):
    m, d = x.shape
    row_blk = pl.BlockSpec((tm, d), lambda i: (i, 0))
    return pl.pallas_call(
        _attn_out_body,
        grid=(m // tm,),
        in_specs=[row_blk, row_blk,
                  pl.BlockSpec((1, d), lambda i: (0, 0)),
                  pl.BlockSpec((d, d), lambda i: (0, 0))],
        out_specs=row_blk,
        out_shape=jax.ShapeDtypeStruct((m, d), F32),
        scratch_shapes=[pltpu.VMEM((tm, d), F32)],
        compiler_params=_params("parallel"),
        name="attn_out",
    )(o, x, gain.reshape(1, d), w_o)


def _route_body(x_ref, g_ref, wr_ref, o_ref, xn_ref):
    _norm_rows_into(x_ref, g_ref, xn_ref)
    logits = jnp.dot(xn_ref[...], wr_ref[...], preferred_element_type=F32,
                     precision=lax.Precision.HIGHEST)
    lane = lax.broadcasted_iota(jnp.int32, logits.shape, 1)
    logits = jnp.where(lane < N_EXPERTS, logits, -jnp.inf)
    m1 = jnp.max(logits, axis=-1, keepdims=True)
    i1 = jnp.min(jnp.where(logits == m1, lane, LANES), axis=-1, keepdims=True)
    rest = jnp.where(lane == i1, -jnp.inf, logits)
    m2 = jnp.max(rest, axis=-1, keepdims=True)
    i2 = jnp.min(jnp.where(rest == m2, lane, LANES), axis=-1, keepdims=True)
    t = jnp.exp(m2 - m1)
    den = 1.0 + t
    out = jnp.where(lane == 0, i1.astype(F32), 0.0)
    out = jnp.where(lane == 1, i2.astype(F32), out)
    out = jnp.where(lane == 2, 1.0 / den, out)
    out = jnp.where(lane == 3, t / den, out)
    o_ref[...] = out


def _moe_route(x, gain, w_router, *, tm=512):
    m, d = x.shape
    wr = jnp.zeros((d, LANES), F32).at[:, :N_EXPERTS].set(w_router)
    return pl.pallas_call(
        _route_body,
        grid=(m // tm,),
        in_specs=[pl.BlockSpec((tm, d), lambda i: (i, 0)),
                  pl.BlockSpec((1, d), lambda i: (0, 0)),
                  pl.BlockSpec((d, LANES), lambda i: (0, 0))],
        out_specs=[pl.BlockSpec((tm, LANES), lambda i: (i, 0)),
                   pl.BlockSpec((tm, d), lambda i: (i, 0))],
        out_shape=[jax.ShapeDtypeStruct((m, LANES), F32),
                   jax.ShapeDtypeStruct((m, d), F32)],
        compiler_params=_params("parallel"),
        name="moe_route",
    )(x, gain.reshape(1, d), wr)


def _row_copy(src_hbm, row, dst_ref, r, sem):
    return pltpu.make_async_copy(src_hbm.at[pl.ds(row, 1), :], dst_ref.at[pl.ds(r, 1), :], sem)


GATHER_UNROLL = 8


def _start_row_gather(idx_ref, base, src_hbm, dst_ref, sem):
    def start(r, carry):
        _row_copy(src_hbm, idx_ref[base + r], dst_ref, r, sem).start()
        return carry
    lax.fori_loop(0, dst_ref.shape[0], start, 0, unroll=GATHER_UNROLL)


def _wait_row_gather(idx_ref, base, src_hbm, dst_ref, sem):
    def wait(r, carry):
        _row_copy(src_hbm, idx_ref[base + r], dst_ref, r, sem).wait()
        return carry
    lax.fori_loop(0, dst_ref.shape[0], wait, 0, unroll=GATHER_UNROLL)


def _experts_body(tok_ref, bexp_ref, rows_ref, nused_ref, hn_hbm, wg_ref, wu_ref, wd_ref,
                  ys_ref, xg_ref, xn_ref, wg16_ref, wu16_ref, wd16_ref, sem):
    i = pl.program_id(0)
    j = pl.program_id(1)
    n_used = nused_ref[0]
    used = i < n_used
    acc_ref = ys_ref

    @pl.when(j == 0)
    def _():
        acc_ref[...] = jnp.zeros_like(acc_ref)

    @pl.when(jnp.logical_and(used, j == 0))
    def _():
        @pl.when(i == 0)
        def _():
            _start_row_gather(tok_ref, 0, hn_hbm, xg_ref, sem)
        _wait_row_gather(tok_ref, i * MOE_TM, hn_hbm, xg_ref, sem)
        xn_ref[...] = xg_ref[...].astype(BF16)

    @pl.when(jnp.logical_and(i + 1 < n_used, j == 1))
    def _():
        _start_row_gather(tok_ref, (i + 1) * MOE_TM, hn_hbm, xg_ref, sem)

    n_rows = rows_ref[i]
    n_max = MOE_TM // EPILOGUE_ROWS
    variants = [n for n in (1, 2, 4, 8, 16) if n < n_max] + [n_max]
    lower = 0
    for n in variants:
        upper = n * EPILOGUE_ROWS
        fits = jnp.logical_and(n_rows > lower, n_rows <= upper)

        @pl.when(jnp.logical_and(used, fits))
        def _(n=n):
            _swiglu_accumulate(xn_ref, _cast_once(wg_ref, wg16_ref), _cast_once(wu_ref, wu16_ref),
                               _cast_once(wd_ref, wd16_ref), acc_ref, n_chunks=n)
        lower = upper


def _moe_experts(tok_pad, blk_expert, blk_rows, n_used, hn, w_gu, w_down):
    d = D_MODEL
    n_blocks = tok_pad.shape[0] // MOE_TM
    nj = D_FF // MOE_TF

    def col(i, j, nused):
        return jnp.where(i < nused[0], j, nj - 1)

    return pl.pallas_call(
        _experts_body,
        grid_spec=pltpu.PrefetchScalarGridSpec(
            num_scalar_prefetch=4,
            grid=(n_blocks, nj),
            in_specs=[
                pl.BlockSpec(memory_space=pl.ANY),
                pl.BlockSpec((1, d, MOE_TF), lambda i, j, tok, be, br, nu: (be[i], 0, col(i, j, nu))),
                pl.BlockSpec((1, d, MOE_TF),
                             lambda i, j, tok, be, br, nu: (be[i], 0, nj + col(i, j, nu))),
                pl.BlockSpec((1, MOE_TF, d), lambda i, j, tok, be, br, nu: (be[i], col(i, j, nu), 0)),
            ],
            out_specs=pl.BlockSpec((MOE_TM, d), lambda i, j, tok, be, br, nu: (i, 0)),
            scratch_shapes=[pltpu.VMEM((MOE_TM, d), F32), pltpu.VMEM((MOE_TM, d), BF16),
                            pltpu.VMEM((d, MOE_TF), BF16), pltpu.VMEM((d, MOE_TF), BF16),
                            pltpu.VMEM((MOE_TF, d), BF16), pltpu.SemaphoreType.DMA],
        ),
        out_shape=jax.ShapeDtypeStruct((n_blocks * MOE_TM, d), F32),
        compiler_params=_params("arbitrary", "arbitrary"),
        name="moe_experts",
    )(tok_pad, blk_expert, blk_rows, n_used, hn, w_gu, w_gu, w_down)


def _combine_body(p0_ref, p1_ref, ys_hbm, route_ref, x_ref, g_ref, o_ref, a_ref, b_ref, sem):
    tm = x_ref.shape[0]
    base = pl.program_id(0) * tm
    _start_row_gather(p0_ref, base, ys_hbm, a_ref, sem.at[0])
    _start_row_gather(p1_ref, base, ys_hbm, b_ref, sem.at[1])
    _wait_row_gather(p0_ref, base, ys_hbm, a_ref, sem.at[0])
    _wait_row_gather(p1_ref, base, ys_hbm, b_ref, sem.at[1])
    gate0 = route_ref[:, TOP_K:TOP_K + 1]
    gate1 = route_ref[:, TOP_K + 1:TOP_K + 2]
    a_ref[...] = a_ref[...] * gate0 + b_ref[...] * gate1
    _residual_norm_rows(x_ref, a_ref, g_ref, o_ref)


def _moe_combine(pos0, pos1, ys, route, x, gain, *, tm=256):
    m, d = x.shape
    return pl.pallas_call(
        _combine_body,
        grid_spec=pltpu.PrefetchScalarGridSpec(
            num_scalar_prefetch=2,
            grid=(m // tm,),
            in_specs=[
                pl.BlockSpec(memory_space=pl.ANY),
                pl.BlockSpec((tm, LANES), lambda i, p0, p1: (i, 0)),
                pl.BlockSpec((tm, d), lambda i, p0, p1: (i, 0)),
                pl.BlockSpec((1, d), lambda i, p0, p1: (0, 0)),
            ],
            out_specs=pl.BlockSpec((tm, d), lambda i, p0, p1: (i, 0)),
            scratch_shapes=[pltpu.VMEM((tm, d), F32), pltpu.VMEM((tm, d), F32),
                            pltpu.SemaphoreType.DMA((2,))],
        ),
        out_shape=jax.ShapeDtypeStruct((m, d), F32),
        compiler_params=_params("arbitrary"),
        name="moe_combine",
    )(pos0, pos1, ys, route, x, gain.reshape(1, d))


def _dispatch_plan(route):
    n = route.shape[0]
    experts = route[:, :TOP_K].astype(jnp.int32).reshape(-1)
    onehot = (experts[:, None] == jnp.arange(N_EXPERTS)[None, :]).astype(jnp.int32)
    rank = jnp.take_along_axis(jnp.cumsum(onehot, axis=0) - onehot, experts[:, None], axis=1)[:, 0]
    counts = jnp.sum(onehot, axis=0)
    blocks_per_expert = (counts + MOE_TM - 1) // MOE_TM
    block_end = jnp.cumsum(blocks_per_expert)
    block_start = block_end - blocks_per_expert
    dest = block_start[experts] * MOE_TM + rank
    n_blocks = (n * TOP_K) // MOE_TM + N_EXPERTS
    tok_pad = jnp.zeros((n_blocks * MOE_TM,), jnp.int32).at[dest].set(jnp.arange(n * TOP_K) // TOP_K)
    blocks = jnp.arange(n_blocks)
    blk_expert = jnp.clip(jnp.searchsorted(block_end, blocks, side='right'),
                          0, N_EXPERTS - 1).astype(jnp.int32)
    blk_rows = jnp.clip(counts[blk_expert] - (blocks - block_start[blk_expert]) * MOE_TM,
                        0, MOE_TM).astype(jnp.int32)
    n_used = block_end[-1].astype(jnp.int32)
    blk_expert = jnp.where(blocks < n_used, blk_expert, blk_expert[jnp.maximum(n_used - 1, 0)])
    pos = dest.reshape(n, TOP_K)
    return tok_pad, blk_expert, blk_rows, n_used.reshape(1), pos[:, 0], pos[:, 1]


def kernel(x, norm_gains, a_w_in, a_b_in, a_norm_v, a_w_s, a_b_s, a_w_out, kv_norm, w_kv,
           b_w_q, b_w_o, ffn_w_gu, ffn_w_down, moe_router, moe_w_gu, moe_w_down):
    batch, seq, d = x.shape
    h = x.reshape(batch * seq, d)
    bf = lambda w: w.astype(BF16)

    g = norm_gains[0]
    (z, ssq), (w_out, w_gu, w_down) = _gmlp_in(
        h, g[0], bf(a_w_in[0]), a_b_in[0], [a_w_out[0], ffn_w_gu[0], ffn_w_down[0]])
    h, (w_kv16, w_q) = _gmlp_out(z, ssq, a_norm_v[0], a_w_s[0], a_b_s[0], h, g[1], w_out,
                                 [_whole_rider(w, 64) for w in (w_kv, b_w_q[0])])
    h, (w_o,) = _ffn_dense(h, g[2], w_gu, w_down, g[3], [_whole_rider(b_w_o[0], 112)])

    g = norm_gains[1]
    kv, _ = _norm_proj(h, kv_norm, w_kv16, [], tm=1024, tn=1024, name="proj_kv")
    q, _ = _norm_proj(h, g[0], w_q, [], tm=1024, tn=1024, name="proj_q")
    base = jnp.exp2(-8.0 * jnp.arange(1, N_KV_HEADS + 1, dtype=F32) / N_KV_HEADS)
    o, _ = _attention(base, q, kv, batch, seq, [])
    h = _attn_out(o, h, g[1], w_o)

    route, hn = _moe_route(h, g[2], moe_router[0])
    tok_pad, blk_expert, blk_rows, n_used, pos0, pos1 = _dispatch_plan(route)
    ys = _moe_experts(tok_pad, blk_expert, blk_rows, n_used, hn, moe_w_gu[0], moe_w_down[0])
    h = _moe_combine(pos0, pos1, ys, route, h, g[3])
    return h.reshape(batch, seq, d)
```

```python
import functools
from typing import NamedTuple

import jax
import jax.numpy as jnp
from jax import lax
from jax.experimental import pallas as pl
from jax.experimental.pallas import tpu as pltpu

F32 = jnp.float32
BF16 = jnp.bfloat16

D_MODEL = 2048
EPS = 1e-6
CHUNK = 128
GMLP_HALF = 2 * D_MODEL
GMLP_GROUPS = 8
GMLP_GROUP_CH = GMLP_HALF // GMLP_GROUPS
HEAD_DIM = 128
N_KV_HEADS = D_MODEL // HEAD_DIM
DILATED_GROUPS = ((128, 1), (512, 4), (2048, 16))
BLOCK = 128
NEG = -1e30
D_FF = 7168
N_EXPERTS = 8
TOP_K = 2

LANES = 128
VMEM_LIMIT_BYTES = 56 * 1024 * 1024

NORM_ROWS = 128
EPILOGUE_ROWS = 128
MOE_TM = 1152
MOE_TF = 512


def _params(*sem):
    return pltpu.CompilerParams(dimension_semantics=sem, vmem_limit_bytes=VMEM_LIMIT_BYTES)


def _rms_scale(x):
    return lax.rsqrt(jnp.mean(x * x, axis=-1, keepdims=True) + EPS)


def _norm_rows_into(x_ref, g_ref, out_ref):
    def body(c, carry):
        rows = pl.ds(pl.multiple_of(c * NORM_ROWS, NORM_ROWS), NORM_ROWS)
        x = x_ref[rows, :]
        out_ref[rows, :] = ((x * _rms_scale(x)) * g_ref[...]).astype(out_ref.dtype)
        return carry
    lax.fori_loop(0, x_ref.shape[0] // NORM_ROWS, body, 0)


def _residual_norm_rows(res_ref, acc_ref, g_ref, out_ref):
    def body(c, carry):
        rows = pl.ds(pl.multiple_of(c * NORM_ROWS, NORM_ROWS), NORM_ROWS)
        a = acc_ref[rows, :]
        out_ref[rows, :] = res_ref[rows, :] + (a * _rms_scale(a)) * g_ref[...]
        return carry
    lax.fori_loop(0, acc_ref.shape[0] // NORM_ROWS, body, 0)


BF16_SUBLANES = 16


class _Rider(NamedTuple):
    src: jax.Array
    dst: jax.Array | None
    first_row: int
    steps: int
    rows: int


def _whole_rider(w, host_steps):
    total = w.shape[0]
    for steps in range(host_steps, 0, -1):
        if total % steps == 0 and (total // steps) % BF16_SUBLANES == 0:
            return _Rider(w, None, 0, steps, total // steps)
    raise ValueError(f"no chunking of {w.shape} over {host_steps} steps")


def _part_rider(w, dst, first_row, steps, rows):
    assert rows % BF16_SUBLANES == 0 and first_row % rows == 0
    assert first_row + steps * rows <= w.shape[0]
    return _Rider(w, dst, first_row, steps, rows)


def _riding_body(body, n_in, n_out, n_riders, n_alias, *refs):
    ins, refs = refs[:n_in], refs[n_in:]
    r_in, refs = refs[:n_riders], refs[n_riders + n_alias:]
    outs, refs = refs[:n_out], refs[n_out:]
    r_out, scratch = refs[:n_riders], refs[n_riders:]

    def cast_part(part, n_parts):
        for src, dst in zip(r_in, r_out):
            width = src.shape[1] // n_parts
            assert width % LANES == 0 and width * n_parts == src.shape[1]
            cols = slice(part * width, (part + 1) * width)
            dst[:, cols] = src[:, cols].astype(BF16)
    body(cast_part, *ins, *outs, *scratch)


def _hosted_call(body, *, grid, in_specs, out_specs, out_shape, scratch_shapes, args, riders,
                 semantics, name, num_scalar_prefetch=0):
    n_grid = len(grid)
    strides = [1] * n_grid
    for ax in range(n_grid - 2, -1, -1):
        strides[ax] = strides[ax + 1] * grid[ax + 1]
    assert all(r.steps <= strides[0] * grid[0] for r in riders)

    r_in, r_alias, r_out, r_shapes = [], [], [], []
    for r in riders:
        def index(*g, r=r):
            step = sum(g[ax] * strides[ax] for ax in range(n_grid))
            return r.first_row // r.rows + jnp.minimum(step, r.steps - 1), 0
        block = (r.rows, r.src.shape[1])
        r_in.append(pl.BlockSpec(block, index))
        r_out.append(pl.BlockSpec(block, index))
        r_shapes.append(jax.ShapeDtypeStruct(r.src.shape, BF16))
        if r.dst is not None:
            r_alias.append(r.dst)

    n_in = num_scalar_prefetch + len(in_specs)
    n_out = len(out_specs)
    aliases, k = {}, 0
    for ridx, r in enumerate(riders):
        if r.dst is not None:
            aliases[n_in + len(riders) + k] = n_out + ridx
            k += 1
    all_in = list(in_specs) + r_in + [pl.BlockSpec(memory_space=pl.ANY)] * len(r_alias)
    all_out = list(out_specs) + r_out
    kernel_fn = functools.partial(_riding_body, body, n_in, n_out, len(riders), len(r_alias))
    if num_scalar_prefetch:
        spec = dict(grid_spec=pltpu.PrefetchScalarGridSpec(
            num_scalar_prefetch=num_scalar_prefetch, grid=grid, in_specs=all_in,
            out_specs=all_out, scratch_shapes=scratch_shapes))
    else:
        spec = dict(grid=grid, in_specs=all_in, out_specs=all_out, scratch_shapes=scratch_shapes)
    outs = pl.pallas_call(
        kernel_fn,
        out_shape=list(out_shape) + r_shapes,
        input_output_aliases=aliases,
        compiler_params=_params(*semantics),
        name=name,
        **spec,
    )(*args, *[r.src for r in riders], *r_alias)
    return outs[:n_out], outs[n_out:]


def _gelu_tanh(x):
    cdf = 0.5 * (1.0 + jnp.tanh(0.7978845608028654 * (x + 0.044715 * (x * x * x))))
    return x * cdf


def _row_chunks(ref, chunk=EPILOGUE_ROWS):
    n = ref.shape[0] // chunk
    return [(c, n, slice(c * chunk, (c + 1) * chunk)) for c in range(n)]


def _proj_body(cast_part, x_ref, g_ref, w_ref, o_ref, xn_ref):
    @pl.when(pl.program_id(1) == 0)
    def _():
        _norm_rows_into(x_ref, g_ref, xn_ref)
    for c, n, rows in _row_chunks(xn_ref):
        o_ref[rows, :] = jnp.dot(xn_ref[rows, :], w_ref[...],
                                 preferred_element_type=F32).astype(o_ref.dtype)
        cast_part(c, n)


def _norm_proj(x, gain, w, riders, *, tm, tn, name):
    m, k = x.shape
    n = w.shape[1]
    (out,), casts = _hosted_call(
        _proj_body,
        grid=(m // tm, n // tn),
        in_specs=[
            pl.BlockSpec((tm, k), lambda i, j: (i, 0)),
            pl.BlockSpec((1, k), lambda i, j: (0, 0)),
            pl.BlockSpec((k, tn), lambda i, j: (0, j)),
        ],
        out_specs=[pl.BlockSpec((tm, tn), lambda i, j: (i, j))],
        out_shape=[jax.ShapeDtypeStruct((m, n), BF16)],
        scratch_shapes=[pltpu.VMEM((tm, k), BF16)],
        args=(x, gain.reshape(1, k), w),
        riders=riders,
        semantics=("arbitrary", "arbitrary"),
        name=name,
    )
    return out, casts


def _gmlp_in_body(cast_part, x_ref, g_ref, w_ref, b_ref, z_ref, ssq_ref, xn_ref, ss_ref):
    j = pl.program_id(1)
    nj = pl.num_programs(1)

    @pl.when(j == 0)
    def _():
        _norm_rows_into(x_ref, g_ref, xn_ref)
        ss_ref[...] = jnp.zeros_like(ss_ref)

    is_v = (j >= nj // 2).astype(F32)
    for c, n, rows in _row_chunks(xn_ref, 2 * EPILOGUE_ROWS):
        z = jnp.dot(xn_ref[rows, :], w_ref[...], preferred_element_type=F32) + b_ref[...]
        z = _gelu_tanh(z)
        z_ref[rows, :] = z.astype(z_ref.dtype)
        ss_ref[rows, :] += is_v * jnp.sum(z * z, axis=-1, keepdims=True)
        cast_part(c, n)

    @pl.when(j == nj - 1)
    def _():
        ssq_ref[...] = jnp.broadcast_to(ss_ref[...], ssq_ref.shape)


def _gmlp_in(x, gain, w_in, b_in, later_weights, *, tm=1024, tn=512):
    m, k = x.shape
    n = w_in.shape[1]
    grid = (m // tm, n // tn)
    return _hosted_call(
        _gmlp_in_body,
        grid=grid,
        in_specs=[
            pl.BlockSpec((tm, k), lambda i, j: (i, 0)),
            pl.BlockSpec((1, k), lambda i, j: (0, 0)),
            pl.BlockSpec((k, tn), lambda i, j: (0, j)),
            pl.BlockSpec((1, tn), lambda i, j: (0, j)),
        ],
        out_specs=[
            pl.BlockSpec((tm, tn), lambda i, j: (i, j)),
            pl.BlockSpec((tm, LANES), lambda i, j: (i, 0)),
        ],
        out_shape=[
            jax.ShapeDtypeStruct((m, n), BF16),
            jax.ShapeDtypeStruct((m, LANES), F32),
        ],
        scratch_shapes=[pltpu.VMEM((tm, k), BF16), pltpu.VMEM((tm, 1), F32)],
        args=(x, gain.reshape(1, k), w_in, b_in.reshape(1, n)),
        riders=[_whole_rider(w, grid[0] * grid[1]) for w in later_weights],
        semantics=("arbitrary", "arbitrary"),
        name="gmlp_in",
    )


def _gmlp_out_body(cast_part, u_ref, v_ref, ssq_ref, gv_ref, ws_ref, bs_ref, x_ref, g_ref, wo_ref,
                   o_ref):
    grp = pl.program_id(1)
    tm = u_ref.shape[0]
    acc_ref = o_ref

    @pl.when(grp == 0)
    def _():
        acc_ref[...] = jnp.zeros_like(acc_ref)

    row = lax.broadcasted_iota(jnp.int32, (CHUNK, CHUNK), 0)
    col = lax.broadcasted_iota(jnp.int32, (CHUNK, CHUNK), 1)
    ws = jnp.where(row >= col, ws_ref[0], 0.0).astype(BF16)
    r = lax.rsqrt(ssq_ref[:, 0:1] * (1.0 / GMLP_HALF) + EPS)
    def project(rows, gated):
        acc_ref[rows, :] += jnp.dot(gated, wo_ref[...], preferred_element_type=F32)

    pending = None
    n = tm // CHUNK
    for c in range(n):
        rows = slice(c * CHUNK, (c + 1) * CHUNK)
        vn = (v_ref[rows, :].astype(F32) * r[rows, :]) * gv_ref[...]
        mixed = jnp.dot(ws, vn.astype(BF16), preferred_element_type=F32) + bs_ref[0]
        gated = (u_ref[rows, :].astype(F32) * mixed).astype(BF16)
        if pending is not None:
            project(*pending)
        pending = (rows, gated)
        cast_part(c, n)
    project(*pending)

    @pl.when(grp == pl.num_programs(1) - 1)
    def _():
        _residual_norm_rows(x_ref, acc_ref, g_ref, o_ref)


def _gmlp_out(z, ssq, norm_v, w_s, b_s, x, gain, w_out, riders, *, tm=1024):
    m, d = x.shape
    gc = GMLP_GROUP_CH
    (out,), casts = _hosted_call(
        _gmlp_out_body,
        grid=(m // tm, GMLP_GROUPS),
        in_specs=[
            pl.BlockSpec((tm, gc), lambda i, g: (i, g)),
            pl.BlockSpec((tm, gc), lambda i, g: (i, GMLP_GROUPS + g)),
            pl.BlockSpec((tm, LANES), lambda i, g: (i, 0)),
            pl.BlockSpec((1, gc), lambda i, g: (0, g)),
            pl.BlockSpec((1, CHUNK, CHUNK), lambda i, g: (g, 0, 0)),
            pl.BlockSpec((1, CHUNK, 1), lambda i, g: (g, 0, 0)),
            pl.BlockSpec((tm, d), lambda i, g: (i, 0)),
            pl.BlockSpec((1, d), lambda i, g: (0, 0)),
            pl.BlockSpec((gc, d), lambda i, g: (g, 0)),
        ],
        out_specs=[pl.BlockSpec((tm, d), lambda i, g: (i, 0))],
        out_shape=[jax.ShapeDtypeStruct((m, d), F32)],
        scratch_shapes=[],
        args=(z, z, ssq, norm_v.reshape(1, GMLP_HALF), w_s, b_s[:, :, None], x,
              gain.reshape(1, d), w_out),
        riders=riders,
        semantics=("arbitrary", "arbitrary"),
        name="gmlp_out",
    )
    return out, casts


def _swiglu_accumulate(xn_ref, wg, wu, wd, acc_ref, cast_part=None, n_chunks=None):
    n = n_chunks or xn_ref.shape[0] // EPILOGUE_ROWS

    def rows(c):
        return slice(c * EPILOGUE_ROWS, (c + 1) * EPILOGUE_ROWS)

    def down(c, a):
        acc_ref[rows(c), :] += jnp.dot(a, wd(), preferred_element_type=F32)

    pending = None
    for c in range(n):
        x = xn_ref[rows(c), :]
        g = jnp.dot(x, wg(), preferred_element_type=F32)
        u = jnp.dot(x, wu(), preferred_element_type=F32)
        a = ((g * jax.nn.sigmoid(g)) * u).astype(BF16)
        if pending is not None:
            down(*pending)
        pending = (c, a)
        if cast_part is not None:
            cast_part(c, n)
    down(*pending)


def _cast_once(src_ref, dst_ref):
    done = []

    def get():
        if not done:
            dst_ref[...] = src_ref[0].astype(BF16)
            done.append(True)
        return dst_ref[...]
    return get


def _ffn_body(cast_part, x_ref, g_in_ref, wg_ref, wu_ref, wd_ref, g_out_ref, o_ref, xn_ref):
    j = pl.program_id(1)
    acc_ref = o_ref

    @pl.when(j == 0)
    def _():
        _norm_rows_into(x_ref, g_in_ref, xn_ref)
        acc_ref[...] = jnp.zeros_like(acc_ref)

    _swiglu_accumulate(xn_ref, lambda: wg_ref[...], lambda: wu_ref[...], lambda: wd_ref[...],
                       acc_ref, cast_part)

    @pl.when(j == pl.num_programs(1) - 1)
    def _():
        _residual_norm_rows(x_ref, acc_ref, g_out_ref, o_ref)


def _ffn_dense(x, g_in, w_gu, w_down, g_out, riders, *, tm=1024, tf=512):
    m, d = x.shape
    nj = D_FF // tf
    once = dict(pipeline_mode=pl.Buffered(1))
    (out,), casts = _hosted_call(
        _ffn_body,
        grid=(m // tm, nj),
        in_specs=[
            pl.BlockSpec((tm, d), lambda i, j: (i, 0), **once),
            pl.BlockSpec((1, d), lambda i, j: (0, 0)),
            pl.BlockSpec((d, tf), lambda i, j: (0, j)),
            pl.BlockSpec((d, tf), lambda i, j: (0, nj + j)),
            pl.BlockSpec((tf, d), lambda i, j: (j, 0)),
            pl.BlockSpec((1, d), lambda i, j: (0, 0)),
        ],
        out_specs=[pl.BlockSpec((tm, d), lambda i, j: (i, 0), **once)],
        out_shape=[jax.ShapeDtypeStruct((m, d), F32)],
        scratch_shapes=[pltpu.VMEM((tm, d), BF16)],
        args=(x, g_in.reshape(1, d), w_gu, w_gu, w_down, g_out.reshape(1, d)),
        riders=riders,
        semantics=("arbitrary", "arbitrary"),
        name="ffn_dense",
    )
    return out, casts


SUB = 4
ATTN_UNROLL = 8


def _deinterleave(src_ref, dst_ref, span):
    part = span // SUB
    for base in range(0, src_ref.shape[0], span):
        for r in range(SUB):
            dst_ref[base + r * part: base + (r + 1) * part, :] = (
                src_ref[pl.ds(base + r, part, stride=SUB), :].astype(dst_ref.dtype))


def _interleave(src_ref, dst_ref, span):
    part = span // SUB
    for base in range(0, src_ref.shape[0], span):
        for r in range(SUB):
            dst_ref[pl.ds(base + r, part, stride=SUB), :] = (
                src_ref[base + r * part: base + (r + 1) * part, :])


def _window_attention(q_ref, k_ref, v_ref, slope, dilation, blocks_per_seq,
                      bias_c_ref, bias_p_ref, pc_ref, pp_ref, den_ref, o_ref, lse_ref):
    jq = lax.broadcasted_iota(jnp.int32, (BLOCK, BLOCK), 0)
    kk = lax.broadcasted_iota(jnp.int32, (BLOCK, BLOCK), 1)
    dist_c = ((jq - kk) * dilation).astype(F32)
    dist_p = ((BLOCK + jq - kk) * dilation).astype(F32)
    bias_c_ref[...] = jnp.where(kk <= jq, -(slope * dist_c), NEG)
    bias_p_ref[...] = jnp.where(kk >= jq, -(slope * dist_p), NEG)
    scale = HEAD_DIM ** -0.5
    contract_last = (((1,), (1,)), ((), ()))
    with_prev = blocks_per_seq > 1

    def block_rows(n):
        rows = pl.ds(pl.multiple_of(n * BLOCK, BLOCK), BLOCK)
        prev = pl.ds(pl.multiple_of(jnp.maximum(n - 1, 0) * BLOCK, BLOCK), BLOCK)
        return rows, prev

    def probabilities(n, carry):
        rows, prev = block_rows(n)
        q = q_ref[rows, :]
        s_c = lax.dot_general(q, k_ref[rows, :], contract_last, preferred_element_type=F32)
        s_c = s_c * scale + bias_c_ref[...]
        if with_prev:
            has_prev = (n % blocks_per_seq) != 0
            s_p = lax.dot_general(q, k_ref[prev, :], contract_last, preferred_element_type=F32)
            s_p = jnp.where(has_prev, s_p * scale + bias_p_ref[...], NEG)
            m = jnp.max(jnp.maximum(s_c, s_p), axis=-1, keepdims=True)
            p_c = jnp.exp(s_c - m)
            p_p = jnp.exp(s_p - m)
            den = jnp.sum(p_c + p_p, axis=-1, keepdims=True)
            pp_ref[rows, :] = p_p.astype(BF16)
        else:
            m = jnp.max(s_c, axis=-1, keepdims=True)
            p_c = jnp.exp(s_c - m)
            den = jnp.sum(p_c, axis=-1, keepdims=True)
        pc_ref[rows, :] = p_c.astype(BF16)
        den_ref[rows, :] = den
        lse_ref[rows, :] = jnp.broadcast_to(m + jnp.log(den), (BLOCK, LANES))
        return carry

    def values(n, carry):
        rows, prev = block_rows(n)
        o = jnp.dot(pc_ref[rows, :], v_ref[rows, :], preferred_element_type=F32)
        if with_prev:
            o = o + jnp.dot(pp_ref[rows, :], v_ref[prev, :], preferred_element_type=F32)
        o_ref[rows, :] = o / den_ref[rows, :]
        return carry

    n_blocks = q_ref.shape[0] // BLOCK
    lax.fori_loop(0, n_blocks, probabilities, 0, unroll=ATTN_UNROLL)
    lax.fori_loop(0, n_blocks, values, 0, unroll=ATTN_UNROLL)


def _attn_body(cast_part, base_ref, q1_ref, q4_ref, q16_ref, k_ref, v_ref, out_ref,
               stage_ref, tmp_ref, k4f_ref, v4f_ref,
               qp_ref, k4_ref, v4_ref, k16_ref, v16_ref,
               o1_ref, l1_ref, o4_ref, l4_ref, o16_ref, l16_ref, op_ref, lp_ref,
               bias_c_ref, bias_p_ref, pc_ref, pp_ref, den_ref):
    seq = k_ref.shape[0]
    base = base_ref[pl.program_id(1)]
    (_, d1), (_, d4), (_, d16) = DILATED_GROUPS
    cast_part(0, 1)

    stage_ref[...] = k_ref[...].astype(F32)
    _deinterleave(stage_ref, k4f_ref, seq)
    k4_ref[...] = k4f_ref[...].astype(BF16)
    _deinterleave(k4f_ref, k16_ref, seq // SUB)
    stage_ref[...] = v_ref[...].astype(F32)
    _deinterleave(stage_ref, v4f_ref, seq)
    v4_ref[...] = v4f_ref[...].astype(BF16)
    _deinterleave(v4f_ref, v16_ref, seq // SUB)

    _window_attention(q1_ref, k_ref, v_ref, base / d1, d1, seq // BLOCK,
                      bias_c_ref, bias_p_ref, pc_ref, pp_ref, den_ref, o1_ref, l1_ref)

    stage_ref[...] = q4_ref[...].astype(F32)
    _deinterleave(stage_ref, qp_ref, seq)
    _window_attention(qp_ref, k4_ref, v4_ref, base / d4, d4, seq // d4 // BLOCK,
                      bias_c_ref, bias_p_ref, pc_ref, pp_ref, den_ref, op_ref, lp_ref)
    _interleave(op_ref, o4_ref, seq)
    _interleave(lp_ref, l4_ref, seq)

    stage_ref[...] = q16_ref[...].astype(F32)
    _deinterleave(stage_ref, tmp_ref, seq)
    _deinterleave(tmp_ref, qp_ref, seq // SUB)
    _window_attention(qp_ref, k16_ref, v16_ref, base / d16, d16, seq // d16 // BLOCK,
                      bias_c_ref, bias_p_ref, pc_ref, pp_ref, den_ref, op_ref, lp_ref)
    _interleave(op_ref, tmp_ref, seq // SUB)
    _interleave(tmp_ref, o16_ref, seq)
    _interleave(lp_ref, tmp_ref, seq // SUB)
    _interleave(tmp_ref, l16_ref, seq)

    chunk = 2 * BLOCK

    def merge(c, carry):
        rows = pl.ds(pl.multiple_of(c * chunk, chunk), chunk)
        la, lb, lc = l1_ref[rows, :], l4_ref[rows, :], l16_ref[rows, :]
        mx = jnp.maximum(jnp.maximum(la, lb), lc)
        ea, eb, ec = jnp.exp(la - mx), jnp.exp(lb - mx), jnp.exp(lc - mx)
        den = ea + eb + ec
        o = (ea / den) * o1_ref[rows, :] + (eb / den) * o4_ref[rows, :] + (ec / den) * o16_ref[rows, :]
        out_ref[rows, :] = o.astype(out_ref.dtype)
        return carry
    lax.fori_loop(0, seq // chunk, merge, 0)


def _attention(base, q, kv, batch, seq, riders):
    for window, dilation in DILATED_GROUPS:
        assert window // dilation == BLOCK, "keys per query must span exactly one previous block"
    assert [d for _, d in DILATED_GROUPS] == [1, SUB, SUB * SUB]
    assert seq % (SUB * SUB * BLOCK) == 0
    h = N_KV_HEADS
    blk = (seq, HEAD_DIM)
    f32buf = pltpu.VMEM((seq, HEAD_DIM), F32)
    bf16buf = pltpu.VMEM((seq, HEAD_DIM), BF16)
    (out,), casts = _hosted_call(
        _attn_body,
        num_scalar_prefetch=1,
        grid=(batch, h),
        in_specs=[
            pl.BlockSpec(blk, lambda b, hh, base: (b, hh)),
            pl.BlockSpec(blk, lambda b, hh, base: (b, h + hh)),
            pl.BlockSpec(blk, lambda b, hh, base: (b, 2 * h + hh)),
            pl.BlockSpec(blk, lambda b, hh, base: (b, hh)),
            pl.BlockSpec(blk, lambda b, hh, base: (b, h + hh)),
        ],
        out_specs=[pl.BlockSpec(blk, lambda b, hh, base: (b, hh))],
        out_shape=[jax.ShapeDtypeStruct((batch * seq, D_MODEL), BF16)],
        scratch_shapes=[f32buf] * 4 + [bf16buf] * 5 + [f32buf] * 8
                       + [pltpu.VMEM((BLOCK, BLOCK), F32)] * 2
                       + [bf16buf] * 2 + [pltpu.VMEM((seq, 1), F32)],
        args=(base, q, q, q, kv, kv),
        riders=riders,
        semantics=("arbitrary", "arbitrary"),
        name="attn",
    )
    return out, casts


def _attn_out_body(o_ref, x_ref, g_ref, wo_ref, out_ref, acc_ref):
    acc_ref[...] = jnp.dot(o_ref[...], wo_ref[...], preferred_element_type=F32)
    _residual_norm_rows(x_ref, acc_ref, g_ref, out_ref)


def _attn_out(o, x, gain, w_o, *, tm=512):
    m, d = x.shape
    row_blk = pl.BlockSpec((tm, d), lambda i: (i, 0))
    return pl.pallas_call(
        _attn_out_body,
        grid=(m // tm,),
        in_specs=[row_blk, row_blk,
                  pl.BlockSpec((1, d), lambda i: (0, 0)),
                  pl.BlockSpec((d, d), lambda i: (0, 0))],
        out_specs=row_blk,
        out_shape=jax.ShapeDtypeStruct((m, d), F32),
        scratch_shapes=[pltpu.VMEM((tm, d), F32)],
        compiler_params=_params("parallel"),
        name="attn_out",
    )(o, x, gain.reshape(1, d), w_o)


def _route_body(x_ref, g_ref, wr_ref, o_ref, xn_ref):
    _norm_rows_into(x_ref, g_ref, xn_ref)
    logits = jnp.dot(xn_ref[...], wr_ref[...], preferred_element_type=F32,
                     precision=lax.Precision.HIGHEST)
    lane = lax.broadcasted_iota(jnp.int32, logits.shape, 1)
    logits = jnp.where(lane < N_EXPERTS, logits, -jnp.inf)
    m1 = jnp.max(logits, axis=-1, keepdims=True)
    i1 = jnp.min(jnp.where(logits == m1, lane, LANES), axis=-1, keepdims=True)
    rest = jnp.where(lane == i1, -jnp.inf, logits)
    m2 = jnp.max(rest, axis=-1, keepdims=True)
    i2 = jnp.min(jnp.where(rest == m2, lane, LANES), axis=-1, keepdims=True)
    t = jnp.exp(m2 - m1)
    den = 1.0 + t
    out = jnp.where(lane == 0, i1.astype(F32), 0.0)
    out = jnp.where(lane == 1, i2.astype(F32), out)
    out = jnp.where(lane == 2, 1.0 / den, out)
    out = jnp.where(lane == 3, t / den, out)
    o_ref[...] = out


def _moe_route(x, gain, w_router, *, tm=512):
    m, d = x.shape
    wr = jnp.zeros((d, LANES), F32).at[:, :N_EXPERTS].set(w_router)
    return pl.pallas_call(
        _route_body,
        grid=(m // tm,),
        in_specs=[pl.BlockSpec((tm, d), lambda i: (i, 0)),
                  pl.BlockSpec((1, d), lambda i: (0, 0)),
                  pl.BlockSpec((d, LANES), lambda i: (0, 0))],
        out_specs=[pl.BlockSpec((tm, LANES), lambda i: (i, 0)),
                   pl.BlockSpec((tm, d), lambda i: (i, 0))],
        out_shape=[jax.ShapeDtypeStruct((m, LANES), F32),
                   jax.ShapeDtypeStruct((m, d), F32)],
        compiler_params=_params("parallel"),
        name="moe_route",
    )(x, gain.reshape(1, d), wr)


def _row_copy(src_hbm, row, dst_ref, r, sem):
    return pltpu.make_async_copy(src_hbm.at[pl.ds(row, 1), :], dst_ref.at[pl.ds(r, 1), :], sem)


GATHER_UNROLL = 8


def _start_row_gather(idx_ref, base, src_hbm, dst_ref, sem):
    def start(r, carry):
        _row_copy(src_hbm, idx_ref[base + r], dst_ref, r, sem).start()
        return carry
    lax.fori_loop(0, dst_ref.shape[0], start, 0, unroll=GATHER_UNROLL)


def _wait_row_gather(idx_ref, base, src_hbm, dst_ref, sem):
    def wait(r, carry):
        _row_copy(src_hbm, idx_ref[base + r], dst_ref, r, sem).wait()
        return carry
    lax.fori_loop(0, dst_ref.shape[0], wait, 0, unroll=GATHER_UNROLL)


def _experts_body(tok_ref, bexp_ref, rows_ref, nused_ref, hn_hbm, wg_ref, wu_ref, wd_ref,
                  ys_ref, xg_ref, xn_ref, wg16_ref, wu16_ref, wd16_ref, sem):
    i = pl.program_id(0)
    j = pl.program_id(1)
    n_used = nused_ref[0]
    used = i < n_used
    acc_ref = ys_ref

    @pl.when(j == 0)
    def _():
        acc_ref[...] = jnp.zeros_like(acc_ref)

    @pl.when(jnp.logical_and(used, j == 0))
    def _():
        @pl.when(i == 0)
        def _():
            _start_row_gather(tok_ref, 0, hn_hbm, xg_ref, sem)
        _wait_row_gather(tok_ref, i * MOE_TM, hn_hbm, xg_ref, sem)
        xn_ref[...] = xg_ref[...].astype(BF16)

    @pl.when(jnp.logical_and(i + 1 < n_used, j == 1))
    def _():
        _start_row_gather(tok_ref, (i + 1) * MOE_TM, hn_hbm, xg_ref, sem)

    n_rows = rows_ref[i]
    n_max = MOE_TM // EPILOGUE_ROWS
    variants = [n for n in (1, 2, 4, 8, 16) if n < n_max] + [n_max]
    lower = 0
    for n in variants:
        upper = n * EPILOGUE_ROWS
        fits = jnp.logical_and(n_rows > lower, n_rows <= upper)

        @pl.when(jnp.logical_and(used, fits))
        def _(n=n):
            _swiglu_accumulate(xn_ref, _cast_once(wg_ref, wg16_ref), _cast_once(wu_ref, wu16_ref),
                               _cast_once(wd_ref, wd16_ref), acc_ref, n_chunks=n)
        lower = upper


def _moe_experts(tok_pad, blk_expert, blk_rows, n_used, hn, w_gu, w_down):
    d = D_MODEL
    n_blocks = tok_pad.shape[0] // MOE_TM
    nj = D_FF // MOE_TF

    def col(i, j, nused):
        return jnp.where(i < nused[0], j, nj - 1)

    return pl.pallas_call(
        _experts_body,
        grid_spec=pltpu.PrefetchScalarGridSpec(
            num_scalar_prefetch=4,
            grid=(n_blocks, nj),
            in_specs=[
                pl.BlockSpec(memory_space=pl.ANY),
                pl.BlockSpec((1, d, MOE_TF), lambda i, j, tok, be, br, nu: (be[i], 0, col(i, j, nu))),
                pl.BlockSpec((1, d, MOE_TF),
                             lambda i, j, tok, be, br, nu: (be[i], 0, nj + col(i, j, nu))),
                pl.BlockSpec((1, MOE_TF, d), lambda i, j, tok, be, br, nu: (be[i], col(i, j, nu), 0)),
            ],
            out_specs=pl.BlockSpec((MOE_TM, d), lambda i, j, tok, be, br, nu: (i, 0),
                                   pipeline_mode=pl.Buffered(1)),
            scratch_shapes=[pltpu.VMEM((MOE_TM, d), F32), pltpu.VMEM((MOE_TM, d), BF16),
                            pltpu.VMEM((d, MOE_TF), BF16), pltpu.VMEM((d, MOE_TF), BF16),
                            pltpu.VMEM((MOE_TF, d), BF16), pltpu.SemaphoreType.DMA],
        ),
        out_shape=jax.ShapeDtypeStruct((n_blocks * MOE_TM, d), F32),
        compiler_params=_params("arbitrary", "arbitrary"),
        name="moe_experts",
    )(tok_pad, blk_expert, blk_rows, n_used, hn, w_gu, w_gu, w_down)


def _combine_body(p0_ref, p1_ref, ys_hbm, route_ref, x_ref, g_ref, o_ref, a_ref, b_ref, sem):
    tm = x_ref.shape[0]
    base = pl.program_id(0) * tm
    _start_row_gather(p0_ref, base, ys_hbm, a_ref, sem.at[0])
    _start_row_gather(p1_ref, base, ys_hbm, b_ref, sem.at[1])
    _wait_row_gather(p0_ref, base, ys_hbm, a_ref, sem.at[0])
    _wait_row_gather(p1_ref, base, ys_hbm, b_ref, sem.at[1])
    gate0 = route_ref[:, TOP_K:TOP_K + 1]
    gate1 = route_ref[:, TOP_K + 1:TOP_K + 2]
    a_ref[...] = a_ref[...] * gate0 + b_ref[...] * gate1
    _residual_norm_rows(x_ref, a_ref, g_ref, o_ref)


def _moe_combine(pos0, pos1, ys, route, x, gain, *, tm=256):
    m, d = x.shape
    return pl.pallas_call(
        _combine_body,
        grid_spec=pltpu.PrefetchScalarGridSpec(
            num_scalar_prefetch=2,
            grid=(m // tm,),
            in_specs=[
                pl.BlockSpec(memory_space=pl.ANY),
                pl.BlockSpec((tm, LANES), lambda i, p0, p1: (i, 0)),
                pl.BlockSpec((tm, d), lambda i, p0, p1: (i, 0)),
                pl.BlockSpec((1, d), lambda i, p0, p1: (0, 0)),
            ],
            out_specs=pl.BlockSpec((tm, d), lambda i, p0, p1: (i, 0)),
            scratch_shapes=[pltpu.VMEM((tm, d), F32), pltpu.VMEM((tm, d), F32),
                            pltpu.SemaphoreType.DMA((2,))],
        ),
        out_shape=jax.ShapeDtypeStruct((m, d), F32),
        compiler_params=_params("arbitrary"),
        name="moe_combine",
    )(pos0, pos1, ys, route, x, gain.reshape(1, d))


def _dispatch_plan(route):
    n = route.shape[0]
    experts = route[:, :TOP_K].astype(jnp.int32).reshape(-1)
    onehot = (experts[:, None] == jnp.arange(N_EXPERTS)[None, :]).astype(jnp.int32)
    rank = jnp.take_along_axis(jnp.cumsum(onehot, axis=0) - onehot, experts[:, None], axis=1)[:, 0]
    counts = jnp.sum(onehot, axis=0)
    blocks_per_expert = (counts + MOE_TM - 1) // MOE_TM
    block_end = jnp.cumsum(blocks_per_expert)
    block_start = block_end - blocks_per_expert
    dest = block_start[experts] * MOE_TM + rank
    n_blocks = (n * TOP_K) // MOE_TM + N_EXPERTS
    tok_pad = jnp.zeros((n_blocks * MOE_TM,), jnp.int32).at[dest].set(jnp.arange(n * TOP_K) // TOP_K)
    blocks = jnp.arange(n_blocks)
    blk_expert = jnp.clip(jnp.searchsorted(block_end, blocks, side='right'),
                          0, N_EXPERTS - 1).astype(jnp.int32)
    blk_rows = jnp.clip(counts[blk_expert] - (blocks - block_start[blk_expert]) * MOE_TM,
                        0, MOE_TM).astype(jnp.int32)
    n_used = block_end[-1].astype(jnp.int32)
    blk_expert = jnp.where(blocks < n_used, blk_expert, blk_expert[jnp.maximum(n_used - 1, 0)])
    pos = dest.reshape(n, TOP_K)
    return tok_pad, blk_expert, blk_rows, n_used.reshape(1), pos[:, 0], pos[:, 1]


def kernel(x, norm_gains, a_w_in, a_b_in, a_norm_v, a_w_s, a_b_s, a_w_out, kv_norm, w_kv,
           b_w_q, b_w_o, ffn_w_gu, ffn_w_down, moe_router, moe_w_gu, moe_w_down):
    batch, seq, d = x.shape
    h = x.reshape(batch * seq, d)
    bf = lambda w: w.astype(BF16)

    g = norm_gains[0]
    (z, ssq), (w_out, w_gu, w_down) = _gmlp_in(
        h, g[0], bf(a_w_in[0]), a_b_in[0], [a_w_out[0], ffn_w_gu[0], ffn_w_down[0]])
    h, (w_kv16, w_q) = _gmlp_out(z, ssq, a_norm_v[0], a_w_s[0], a_b_s[0], h, g[1], w_out,
                                 [_whole_rider(w, 64) for w in (w_kv, b_w_q[0])])
    h, (w_o,) = _ffn_dense(h, g[2], w_gu, w_down, g[3], [_whole_rider(b_w_o[0], 112)])

    g = norm_gains[1]
    kv, _ = _norm_proj(h, kv_norm, w_kv16, [], tm=1024, tn=1024, name="proj_kv")
    q, _ = _norm_proj(h, g[0], w_q, [], tm=1024, tn=1024, name="proj_q")
    base = jnp.exp2(-8.0 * jnp.arange(1, N_KV_HEADS + 1, dtype=F32) / N_KV_HEADS)
    o, _ = _attention(base, q, kv, batch, seq, [])
    h = _attn_out(o, h, g[1], w_o)

    route, hn = _moe_route(h, g[2], moe_router[0])
    tok_pad, blk_expert, blk_rows, n_used, pos0, pos1 = _dispatch_plan(route)
    ys = _moe_experts(tok_pad, blk_expert, blk_rows, n_used, hn, moe_w_gu[0], moe_w_down[0])
    h = _moe_combine(pos0, pos1, ys, route, h, g[3])
    return h.reshape(batch, seq, d)
```

```python
import functools
from typing import NamedTuple

import jax
import jax.numpy as jnp
from jax import lax
from jax.experimental import pallas as pl
from jax.experimental.pallas import tpu as pltpu

F32 = jnp.float32
BF16 = jnp.bfloat16

D_MODEL = 2048
EPS = 1e-6
CHUNK = 128
GMLP_HALF = 2 * D_MODEL
GMLP_GROUPS = 8
GMLP_GROUP_CH = GMLP_HALF // GMLP_GROUPS
HEAD_DIM = 128
N_KV_HEADS = D_MODEL // HEAD_DIM
DILATED_GROUPS = ((128, 1), (512, 4), (2048, 16))
BLOCK = 128
NEG = -1e30
D_FF = 7168
N_EXPERTS = 8
TOP_K = 2

LANES = 128
VMEM_LIMIT_BYTES = 56 * 1024 * 1024

NORM_ROWS = 128
EPILOGUE_ROWS = 128
MOE_TM = 1152
MOE_TF = 512


def _params(*sem):
    return pltpu.CompilerParams(dimension_semantics=sem, vmem_limit_bytes=VMEM_LIMIT_BYTES)


def _rms_scale(x):
    return lax.rsqrt(jnp.mean(x * x, axis=-1, keepdims=True) + EPS)


def _norm_rows_into(x_ref, g_ref, out_ref):
    def body(c, carry):
        rows = pl.ds(pl.multiple_of(c * NORM_ROWS, NORM_ROWS), NORM_ROWS)
        x = x_ref[rows, :]
        out_ref[rows, :] = ((x * _rms_scale(x)) * g_ref[...]).astype(out_ref.dtype)
        return carry
    lax.fori_loop(0, x_ref.shape[0] // NORM_ROWS, body, 0)


def _residual_norm_rows(res_ref, acc_ref, g_ref, out_ref):
    def body(c, carry):
        rows = pl.ds(pl.multiple_of(c * NORM_ROWS, NORM_ROWS), NORM_ROWS)
        a = acc_ref[rows, :]
        out_ref[rows, :] = res_ref[rows, :] + (a * _rms_scale(a)) * g_ref[...]
        return carry
    lax.fori_loop(0, acc_ref.shape[0] // NORM_ROWS, body, 0)


BF16_SUBLANES = 16


class _Rider(NamedTuple):
    src: jax.Array
    dst: jax.Array | None
    first_row: int
    steps: int
    rows: int


def _whole_rider(w, host_steps):
    total = w.shape[0]
    for steps in range(host_steps, 0, -1):
        if total % steps == 0 and (total // steps) % BF16_SUBLANES == 0:
            return _Rider(w, None, 0, steps, total // steps)
    raise ValueError(f"no chunking of {w.shape} over {host_steps} steps")


def _part_rider(w, dst, first_row, steps, rows):
    assert rows % BF16_SUBLANES == 0 and first_row % rows == 0
    assert first_row + steps * rows <= w.shape[0]
    return _Rider(w, dst, first_row, steps, rows)


def _riding_body(body, n_in, n_out, n_riders, n_alias, *refs):
    ins, refs = refs[:n_in], refs[n_in:]
    r_in, refs = refs[:n_riders], refs[n_riders + n_alias:]
    outs, refs = refs[:n_out], refs[n_out:]
    r_out, scratch = refs[:n_riders], refs[n_riders:]

    def cast_part(part, n_parts):
        for src, dst in zip(r_in, r_out):
            width = src.shape[1] // n_parts
            assert width % LANES == 0 and width * n_parts == src.shape[1]
            cols = slice(part * width, (part + 1) * width)
            dst[:, cols] = src[:, cols].astype(BF16)
    body(cast_part, *ins, *outs, *scratch)


def _hosted_call(body, *, grid, in_specs, out_specs, out_shape, scratch_shapes, args, riders,
                 semantics, name, num_scalar_prefetch=0):
    n_grid = len(grid)
    strides = [1] * n_grid
    for ax in range(n_grid - 2, -1, -1):
        strides[ax] = strides[ax + 1] * grid[ax + 1]
    assert all(r.steps <= strides[0] * grid[0] for r in riders)

    r_in, r_alias, r_out, r_shapes = [], [], [], []
    for r in riders:
        def index(*g, r=r):
            step = sum(g[ax] * strides[ax] for ax in range(n_grid))
            return r.first_row // r.rows + jnp.minimum(step, r.steps - 1), 0
        block = (r.rows, r.src.shape[1])
        r_in.append(pl.BlockSpec(block, index))
        r_out.append(pl.BlockSpec(block, index))
        r_shapes.append(jax.ShapeDtypeStruct(r.src.shape, BF16))
        if r.dst is not None:
            r_alias.append(r.dst)

    n_in = num_scalar_prefetch + len(in_specs)
    n_out = len(out_specs)
    aliases, k = {}, 0
    for ridx, r in enumerate(riders):
        if r.dst is not None:
            aliases[n_in + len(riders) + k] = n_out + ridx
            k += 1
    all_in = list(in_specs) + r_in + [pl.BlockSpec(memory_space=pl.ANY)] * len(r_alias)
    all_out = list(out_specs) + r_out
    kernel_fn = functools.partial(_riding_body, body, n_in, n_out, len(riders), len(r_alias))
    if num_scalar_prefetch:
        spec = dict(grid_spec=pltpu.PrefetchScalarGridSpec(
            num_scalar_prefetch=num_scalar_prefetch, grid=grid, in_specs=all_in,
            out_specs=all_out, scratch_shapes=scratch_shapes))
    else:
        spec = dict(grid=grid, in_specs=all_in, out_specs=all_out, scratch_shapes=scratch_shapes)
    outs = pl.pallas_call(
        kernel_fn,
        out_shape=list(out_shape) + r_shapes,
        input_output_aliases=aliases,
        compiler_params=_params(*semantics),
        name=name,
        **spec,
    )(*args, *[r.src for r in riders], *r_alias)
    return outs[:n_out], outs[n_out:]


def _gelu_tanh(x):
    cdf = 0.5 * (1.0 + jnp.tanh(0.7978845608028654 * (x + 0.044715 * (x * x * x))))
    return x * cdf


def _row_chunks(ref, chunk=EPILOGUE_ROWS):
    n = ref.shape[0] // chunk
    return [(c, n, slice(c * chunk, (c + 1) * chunk)) for c in range(n)]


def _proj_body(cast_part, x_ref, g_ref, w_ref, o_ref, xn_ref):
    @pl.when(pl.program_id(1) == 0)
    def _():
        _norm_rows_into(x_ref, g_ref, xn_ref)
    for c, n, rows in _row_chunks(xn_ref):
        o_ref[rows, :] = jnp.dot(xn_ref[rows, :], w_ref[...],
                                 preferred_element_type=F32).astype(o_ref.dtype)
        cast_part(c, n)


def _norm_proj(x, gain, w, riders, *, tm, tn, name):
    m, k = x.shape
    n = w.shape[1]
    (out,), casts = _hosted_call(
        _proj_body,
        grid=(m // tm, n // tn),
        in_specs=[
            pl.BlockSpec((tm, k), lambda i, j: (i, 0)),
            pl.BlockSpec((1, k), lambda i, j: (0, 0)),
            pl.BlockSpec((k, tn), lambda i, j: (0, j)),
        ],
        out_specs=[pl.BlockSpec((tm, tn), lambda i, j: (i, j))],
        out_shape=[jax.ShapeDtypeStruct((m, n), BF16)],
        scratch_shapes=[pltpu.VMEM((tm, k), BF16)],
        args=(x, gain.reshape(1, k), w),
        riders=riders,
        semantics=("arbitrary", "arbitrary"),
        name=name,
    )
    return out, casts


def _gmlp_in_body(cast_part, x_ref, g_ref, w_ref, b_ref, z_ref, ssq_ref, xn_ref, ss_ref):
    j = pl.program_id(1)
    nj = pl.num_programs(1)

    @pl.when(j == 0)
    def _():
        _norm_rows_into(x_ref, g_ref, xn_ref)
        ss_ref[...] = jnp.zeros_like(ss_ref)

    is_v = (j >= nj // 2).astype(F32)
    for c, n, rows in _row_chunks(xn_ref, 2 * EPILOGUE_ROWS):
        z = jnp.dot(xn_ref[rows, :], w_ref[...], preferred_element_type=F32) + b_ref[...]
        z = _gelu_tanh(z)
        z_ref[rows, :] = z.astype(z_ref.dtype)
        ss_ref[rows, :] += is_v * jnp.sum(z * z, axis=-1, keepdims=True)
        cast_part(c, n)

    @pl.when(j == nj - 1)
    def _():
        ssq_ref[...] = jnp.broadcast_to(ss_ref[...], ssq_ref.shape)


def _gmlp_in(x, gain, w_in, b_in, later_weights, *, tm=1024, tn=512):
    m, k = x.shape
    n = w_in.shape[1]
    grid = (m // tm, n // tn)
    return _hosted_call(
        _gmlp_in_body,
        grid=grid,
        in_specs=[
            pl.BlockSpec((tm, k), lambda i, j: (i, 0)),
            pl.BlockSpec((1, k), lambda i, j: (0, 0)),
            pl.BlockSpec((k, tn), lambda i, j: (0, j)),
            pl.BlockSpec((1, tn), lambda i, j: (0, j)),
        ],
        out_specs=[
            pl.BlockSpec((tm, tn), lambda i, j: (i, j)),
            pl.BlockSpec((tm, LANES), lambda i, j: (i, 0)),
        ],
        out_shape=[
            jax.ShapeDtypeStruct((m, n), BF16),
            jax.ShapeDtypeStruct((m, LANES), F32),
        ],
        scratch_shapes=[pltpu.VMEM((tm, k), BF16), pltpu.VMEM((tm, 1), F32)],
        args=(x, gain.reshape(1, k), w_in, b_in.reshape(1, n)),
        riders=[_whole_rider(w, grid[0] * grid[1]) for w in later_weights],
        semantics=("arbitrary", "arbitrary"),
        name="gmlp_in",
    )


def _gmlp_out_body(cast_part, u_ref, v_ref, ssq_ref, gv_ref, ws_ref, bs_ref, x_ref, g_ref, wo_ref,
                   o_ref):
    grp = pl.program_id(1)
    tm = u_ref.shape[0]
    acc_ref = o_ref

    @pl.when(grp == 0)
    def _():
        acc_ref[...] = jnp.zeros_like(acc_ref)

    row = lax.broadcasted_iota(jnp.int32, (CHUNK, CHUNK), 0)
    col = lax.broadcasted_iota(jnp.int32, (CHUNK, CHUNK), 1)
    ws = jnp.where(row >= col, ws_ref[0], 0.0).astype(BF16)
    r = lax.rsqrt(ssq_ref[:, 0:1] * (1.0 / GMLP_HALF) + EPS)
    def project(rows, gated):
        acc_ref[rows, :] += jnp.dot(gated, wo_ref[...], preferred_element_type=F32)

    pending = None
    n = tm // CHUNK
    for c in range(n):
        rows = slice(c * CHUNK, (c + 1) * CHUNK)
        vn = (v_ref[rows, :].astype(F32) * r[rows, :]) * gv_ref[...]
        mixed = jnp.dot(ws, vn.astype(BF16), preferred_element_type=F32) + bs_ref[0]
        gated = (u_ref[rows, :].astype(F32) * mixed).astype(BF16)
        if pending is not None:
            project(*pending)
        pending = (rows, gated)
        cast_part(c, n)
    project(*pending)

    @pl.when(grp == pl.num_programs(1) - 1)
    def _():
        _residual_norm_rows(x_ref, acc_ref, g_ref, o_ref)


def _gmlp_out(z, ssq, norm_v, w_s, b_s, x, gain, w_out, riders, *, tm=1024):
    m, d = x.shape
    gc = GMLP_GROUP_CH
    (out,), casts = _hosted_call(
        _gmlp_out_body,
        grid=(m // tm, GMLP_GROUPS),
        in_specs=[
            pl.BlockSpec((tm, gc), lambda i, g: (i, g)),
            pl.BlockSpec((tm, gc), lambda i, g: (i, GMLP_GROUPS + g)),
            pl.BlockSpec((tm, LANES), lambda i, g: (i, 0)),
            pl.BlockSpec((1, gc), lambda i, g: (0, g)),
            pl.BlockSpec((1, CHUNK, CHUNK), lambda i, g: (g, 0, 0)),
            pl.BlockSpec((1, CHUNK, 1), lambda i, g: (g, 0, 0)),
            pl.BlockSpec((tm, d), lambda i, g: (i, 0)),
            pl.BlockSpec((1, d), lambda i, g: (0, 0)),
            pl.BlockSpec((gc, d), lambda i, g: (g, 0)),
        ],
        out_specs=[pl.BlockSpec((tm, d), lambda i, g: (i, 0))],
        out_shape=[jax.ShapeDtypeStruct((m, d), F32)],
        scratch_shapes=[],
        args=(z, z, ssq, norm_v.reshape(1, GMLP_HALF), w_s, b_s[:, :, None], x,
              gain.reshape(1, d), w_out),
        riders=riders,
        semantics=("arbitrary", "arbitrary"),
        name="gmlp_out",
    )
    return out, casts


SWIGLU_ROWS = 256


def _chunks_covering(n_rows):
    assert n_rows % EPILOGUE_ROWS == 0
    bounds = list(range(0, n_rows, SWIGLU_ROWS)) + [n_rows]
    return [slice(lo, hi) for lo, hi in zip(bounds[:-1], bounds[1:])]


def _swiglu_accumulate(xn_ref, wg, wu, wd, acc_ref, cast_part=None, n_rows=None):
    chunks = _chunks_covering(n_rows or xn_ref.shape[0])

    def down(rows, a):
        acc_ref[rows, :] += jnp.dot(a, wd(), preferred_element_type=F32)

    pending = None
    for c, rows in enumerate(chunks):
        x = xn_ref[rows, :]
        g = jnp.dot(x, wg(), preferred_element_type=F32)
        u = jnp.dot(x, wu(), preferred_element_type=F32)
        a = ((g * jax.nn.sigmoid(g)) * u).astype(BF16)
        if pending is not None:
            down(*pending)
        pending = (rows, a)
        if cast_part is not None:
            cast_part(c, len(chunks))
    down(*pending)


def _cast_once(src_ref, dst_ref):
    done = []

    def get():
        if not done:
            dst_ref[...] = src_ref[0].astype(BF16)
            done.append(True)
        return dst_ref[...]
    return get


def _ffn_body(cast_part, x_ref, g_in_ref, wg_ref, wu_ref, wd_ref, g_out_ref, o_ref, xn_ref):
    j = pl.program_id(1)
    acc_ref = o_ref

    @pl.when(j == 0)
    def _():
        _norm_rows_into(x_ref, g_in_ref, xn_ref)
        acc_ref[...] = jnp.zeros_like(acc_ref)

    _swiglu_accumulate(xn_ref, lambda: wg_ref[...], lambda: wu_ref[...], lambda: wd_ref[...],
                       acc_ref, cast_part)

    @pl.when(j == pl.num_programs(1) - 1)
    def _():
        _residual_norm_rows(x_ref, acc_ref, g_out_ref, o_ref)


def _ffn_dense(x, g_in, w_gu, w_down, g_out, riders, *, tm=1024, tf=512):
    m, d = x.shape
    nj = D_FF // tf
    once = dict(pipeline_mode=pl.Buffered(1))
    (out,), casts = _hosted_call(
        _ffn_body,
        grid=(m // tm, nj),
        in_specs=[
            pl.BlockSpec((tm, d), lambda i, j: (i, 0), **once),
            pl.BlockSpec((1, d), lambda i, j: (0, 0)),
            pl.BlockSpec((d, tf), lambda i, j: (0, j)),
            pl.BlockSpec((d, tf), lambda i, j: (0, nj + j)),
            pl.BlockSpec((tf, d), lambda i, j: (j, 0)),
            pl.BlockSpec((1, d), lambda i, j: (0, 0)),
        ],
        out_specs=[pl.BlockSpec((tm, d), lambda i, j: (i, 0), **once)],
        out_shape=[jax.ShapeDtypeStruct((m, d), F32)],
        scratch_shapes=[pltpu.VMEM((tm, d), BF16)],
        args=(x, g_in.reshape(1, d), w_gu, w_gu, w_down, g_out.reshape(1, d)),
        riders=riders,
        semantics=("arbitrary", "arbitrary"),
        name="ffn_dense",
    )
    return out, casts


SUB = 4
ATTN_UNROLL = 8


def _deinterleave(src_ref, dst_ref, span):
    part = span // SUB
    for base in range(0, src_ref.shape[0], span):
        for r in range(SUB):
            dst_ref[base + r * part: base + (r + 1) * part, :] = (
                src_ref[pl.ds(base + r, part, stride=SUB), :].astype(dst_ref.dtype))


def _interleave(src_ref, dst_ref, span):
    part = span // SUB
    for base in range(0, src_ref.shape[0], span):
        for r in range(SUB):
            dst_ref[pl.ds(base + r, part, stride=SUB), :] = (
                src_ref[base + r * part: base + (r + 1) * part, :])


def _window_attention(q_ref, k_ref, v_ref, slope, dilation, blocks_per_seq,
                      bias_c_ref, bias_p_ref, pc_ref, pp_ref, den_ref, o_ref, lse_ref):
    jq = lax.broadcasted_iota(jnp.int32, (BLOCK, BLOCK), 0)
    kk = lax.broadcasted_iota(jnp.int32, (BLOCK, BLOCK), 1)
    dist_c = ((jq - kk) * dilation).astype(F32)
    dist_p = ((BLOCK + jq - kk) * dilation).astype(F32)
    bias_c_ref[...] = jnp.where(kk <= jq, -(slope * dist_c), NEG)
    bias_p_ref[...] = jnp.where(kk >= jq, -(slope * dist_p), NEG)
    scale = HEAD_DIM ** -0.5
    contract_last = (((1,), (1,)), ((), ()))
    with_prev = blocks_per_seq > 1

    def block_rows(n):
        rows = pl.ds(pl.multiple_of(n * BLOCK, BLOCK), BLOCK)
        prev = pl.ds(pl.multiple_of(jnp.maximum(n - 1, 0) * BLOCK, BLOCK), BLOCK)
        return rows, prev

    def probabilities(n, carry):
        rows, prev = block_rows(n)
        q = q_ref[rows, :]
        s_c = lax.dot_general(q, k_ref[rows, :], contract_last, preferred_element_type=F32)
        s_c = s_c * scale + bias_c_ref[...]
        if with_prev:
            has_prev = (n % blocks_per_seq) != 0
            s_p = lax.dot_general(q, k_ref[prev, :], contract_last, preferred_element_type=F32)
            s_p = jnp.where(has_prev, s_p * scale + bias_p_ref[...], NEG)
            m = jnp.max(jnp.maximum(s_c, s_p), axis=-1, keepdims=True)
            p_c = jnp.exp(s_c - m)
            p_p = jnp.exp(s_p - m)
            den = jnp.sum(p_c + p_p, axis=-1, keepdims=True)
            pp_ref[rows, :] = p_p.astype(BF16)
        else:
            m = jnp.max(s_c, axis=-1, keepdims=True)
            p_c = jnp.exp(s_c - m)
            den = jnp.sum(p_c, axis=-1, keepdims=True)
        pc_ref[rows, :] = p_c.astype(BF16)
        den_ref[rows, :] = den
        lse_ref[rows, :] = jnp.broadcast_to(m + jnp.log(den), (BLOCK, LANES))
        return carry

    def values(n, carry):
        rows, prev = block_rows(n)
        o = jnp.dot(pc_ref[rows, :], v_ref[rows, :], preferred_element_type=F32)
        if with_prev:
            o = o + jnp.dot(pp_ref[rows, :], v_ref[prev, :], preferred_element_type=F32)
        o_ref[rows, :] = o / den_ref[rows, :]
        return carry

    n_blocks = q_ref.shape[0] // BLOCK
    lax.fori_loop(0, n_blocks, probabilities, 0, unroll=ATTN_UNROLL)
    lax.fori_loop(0, n_blocks, values, 0, unroll=ATTN_UNROLL)


def _attn_body(cast_part, base_ref, q1_ref, q4_ref, q16_ref, k_ref, v_ref, out_ref,
               stage_ref, tmp_ref, k4f_ref, v4f_ref,
               qp_ref, k4_ref, v4_ref, k16_ref, v16_ref,
               o1_ref, l1_ref, o4_ref, l4_ref, o16_ref, l16_ref, op_ref, lp_ref,
               bias_c_ref, bias_p_ref, pc_ref, pp_ref, den_ref):
    seq = k_ref.shape[0]
    base = base_ref[pl.program_id(1)]
    (_, d1), (_, d4), (_, d16) = DILATED_GROUPS
    cast_part(0, 1)

    stage_ref[...] = k_ref[...].astype(F32)
    _deinterleave(stage_ref, k4f_ref, seq)
    k4_ref[...] = k4f_ref[...].astype(BF16)
    _deinterleave(k4f_ref, k16_ref, seq // SUB)
    stage_ref[...] = v_ref[...].astype(F32)
    _deinterleave(stage_ref, v4f_ref, seq)
    v4_ref[...] = v4f_ref[...].astype(BF16)
    _deinterleave(v4f_ref, v16_ref, seq // SUB)

    _window_attention(q1_ref, k_ref, v_ref, base / d1, d1, seq // BLOCK,
                      bias_c_ref, bias_p_ref, pc_ref, pp_ref, den_ref, o1_ref, l1_ref)

    stage_ref[...] = q4_ref[...].astype(F32)
    _deinterleave(stage_ref, qp_ref, seq)
    _window_attention(qp_ref, k4_ref, v4_ref, base / d4, d4, seq // d4 // BLOCK,
                      bias_c_ref, bias_p_ref, pc_ref, pp_ref, den_ref, op_ref, lp_ref)
    _interleave(op_ref, o4_ref, seq)
    _interleave(lp_ref, l4_ref, seq)

    stage_ref[...] = q16_ref[...].astype(F32)
    _deinterleave(stage_ref, tmp_ref, seq)
    _deinterleave(tmp_ref, qp_ref, seq // SUB)
    _window_attention(qp_ref, k16_ref, v16_ref, base / d16, d16, seq // d16 // BLOCK,
                      bias_c_ref, bias_p_ref, pc_ref, pp_ref, den_ref, op_ref, lp_ref)
    _interleave(op_ref, tmp_ref, seq // SUB)
    _interleave(tmp_ref, o16_ref, seq)
    _interleave(lp_ref, tmp_ref, seq // SUB)
    _interleave(tmp_ref, l16_ref, seq)

    chunk = 2 * BLOCK

    def merge(c, carry):
        rows = pl.ds(pl.multiple_of(c * chunk, chunk), chunk)
        la, lb, lc = l1_ref[rows, :], l4_ref[rows, :], l16_ref[rows, :]
        mx = jnp.maximum(jnp.maximum(la, lb), lc)
        ea, eb, ec = jnp.exp(la - mx), jnp.exp(lb - mx), jnp.exp(lc - mx)
        den = ea + eb + ec
        o = (ea / den) * o1_ref[rows, :] + (eb / den) * o4_ref[rows, :] + (ec / den) * o16_ref[rows, :]
        out_ref[rows, :] = o.astype(out_ref.dtype)
        return carry
    lax.fori_loop(0, seq // chunk, merge, 0)


def _attention(base, q, kv, batch, seq, riders):
    for window, dilation in DILATED_GROUPS:
        assert window // dilation == BLOCK, "keys per query must span exactly one previous block"
    assert [d for _, d in DILATED_GROUPS] == [1, SUB, SUB * SUB]
    assert seq % (SUB * SUB * BLOCK) == 0
    h = N_KV_HEADS
    blk = (seq, HEAD_DIM)
    f32buf = pltpu.VMEM((seq, HEAD_DIM), F32)
    bf16buf = pltpu.VMEM((seq, HEAD_DIM), BF16)
    (out,), casts = _hosted_call(
        _attn_body,
        num_scalar_prefetch=1,
        grid=(batch, h),
        in_specs=[
            pl.BlockSpec(blk, lambda b, hh, base: (b, hh)),
            pl.BlockSpec(blk, lambda b, hh, base: (b, h + hh)),
            pl.BlockSpec(blk, lambda b, hh, base: (b, 2 * h + hh)),
            pl.BlockSpec(blk, lambda b, hh, base: (b, hh)),
            pl.BlockSpec(blk, lambda b, hh, base: (b, h + hh)),
        ],
        out_specs=[pl.BlockSpec(blk, lambda b, hh, base: (b, hh))],
        out_shape=[jax.ShapeDtypeStruct((batch * seq, D_MODEL), BF16)],
        scratch_shapes=[f32buf] * 4 + [bf16buf] * 5 + [f32buf] * 8
                       + [pltpu.VMEM((BLOCK, BLOCK), F32)] * 2
                       + [bf16buf] * 2 + [pltpu.VMEM((seq, 1), F32)],
        args=(base, q, q, q, kv, kv),
        riders=riders,
        semantics=("arbitrary", "arbitrary"),
        name="attn",
    )
    return out, casts


def _attn_out_body(o_ref, x_ref, g_ref, wo_ref, out_ref, acc_ref):
    acc_ref[...] = jnp.dot(o_ref[...], wo_ref[...], preferred_element_type=F32)
    _residual_norm_rows(x_ref, acc_ref, g_ref, out_ref)


def _attn_out(o, x, gain, w_o, *, tm=512):
    m, d = x.shape
    row_blk = pl.BlockSpec((tm, d), lambda i: (i, 0))
    return pl.pallas_call(
        _attn_out_body,
        grid=(m // tm,),
        in_specs=[row_blk, row_blk,
                  pl.BlockSpec((1, d), lambda i: (0, 0)),
                  pl.BlockSpec((d, d), lambda i: (0, 0))],
        out_specs=row_blk,
        out_shape=jax.ShapeDtypeStruct((m, d), F32),
        scratch_shapes=[pltpu.VMEM((tm, d), F32)],
        compiler_params=_params("parallel"),
        name="attn_out",
    )(o, x, gain.reshape(1, d), w_o)


def _route_body(x_ref, g_ref, wr_ref, o_ref, xn_ref):
    _norm_rows_into(x_ref, g_ref, xn_ref)
    logits = jnp.dot(xn_ref[...], wr_ref[...], preferred_element_type=F32,
                     precision=lax.Precision.HIGHEST)
    lane = lax.broadcasted_iota(jnp.int32, logits.shape, 1)
    logits = jnp.where(lane < N_EXPERTS, logits, -jnp.inf)
    m1 = jnp.max(logits, axis=-1, keepdims=True)
    i1 = jnp.min(jnp.where(logits == m1, lane, LANES), axis=-1, keepdims=True)
    rest = jnp.where(lane == i1, -jnp.inf, logits)
    m2 = jnp.max(rest, axis=-1, keepdims=True)
    i2 = jnp.min(jnp.where(rest == m2, lane, LANES), axis=-1, keepdims=True)
    t = jnp.exp(m2 - m1)
    den = 1.0 + t
    out = jnp.where(lane == 0, i1.astype(F32), 0.0)
    out = jnp.where(lane == 1, i2.astype(F32), out)
    out = jnp.where(lane == 2, 1.0 / den, out)
    out = jnp.where(lane == 3, t / den, out)
    o_ref[...] = out


def _moe_route(x, gain, w_router, *, tm=512):
    m, d = x.shape
    wr = jnp.zeros((d, LANES), F32).at[:, :N_EXPERTS].set(w_router)
    return pl.pallas_call(
        _route_body,
        grid=(m // tm,),
        in_specs=[pl.BlockSpec((tm, d), lambda i: (i, 0)),
                  pl.BlockSpec((1, d), lambda i: (0, 0)),
                  pl.BlockSpec((d, LANES), lambda i: (0, 0))],
        out_specs=[pl.BlockSpec((tm, LANES), lambda i: (i, 0)),
                   pl.BlockSpec((tm, d), lambda i: (i, 0))],
        out_shape=[jax.ShapeDtypeStruct((m, LANES), F32),
                   jax.ShapeDtypeStruct((m, d), F32)],
        compiler_params=_params("parallel"),
        name="moe_route",
    )(x, gain.reshape(1, d), wr)


def _row_copy(src_hbm, row, dst_ref, r, sem):
    return pltpu.make_async_copy(src_hbm.at[pl.ds(row, 1), :], dst_ref.at[pl.ds(r, 1), :], sem)


GATHER_UNROLL = 8


def _start_row_gather(idx_ref, base, src_hbm, dst_ref, sem):
    def start(r, carry):
        _row_copy(src_hbm, idx_ref[base + r], dst_ref, r, sem).start()
        return carry
    lax.fori_loop(0, dst_ref.shape[0], start, 0, unroll=GATHER_UNROLL)


def _wait_row_gather(idx_ref, base, src_hbm, dst_ref, sem):
    def wait(r, carry):
        _row_copy(src_hbm, idx_ref[base + r], dst_ref, r, sem).wait()
        return carry
    lax.fori_loop(0, dst_ref.shape[0], wait, 0, unroll=GATHER_UNROLL)


def _experts_body(tok_ref, bexp_ref, rows_ref, nused_ref, hn_hbm, wg_ref, wu_ref, wd_ref,
                  ys_ref, xg_ref, xn_ref, wg16_ref, wu16_ref, wd16_ref, sem):
    i = pl.program_id(0)
    j = pl.program_id(1)
    n_used = nused_ref[0]
    used = i < n_used
    acc_ref = ys_ref

    @pl.when(j == 0)
    def _():
        acc_ref[...] = jnp.zeros_like(acc_ref)

    @pl.when(jnp.logical_and(used, j == 0))
    def _():
        @pl.when(i == 0)
        def _():
            _start_row_gather(tok_ref, 0, hn_hbm, xg_ref, sem)
        _wait_row_gather(tok_ref, i * MOE_TM, hn_hbm, xg_ref, sem)
        xn_ref[...] = xg_ref[...].astype(BF16)

    @pl.when(jnp.logical_and(i + 1 < n_used, j == 1))
    def _():
        _start_row_gather(tok_ref, (i + 1) * MOE_TM, hn_hbm, xg_ref, sem)

    n_rows = rows_ref[i]
    variants = [r for r in (128, 256, 512, 1024, 2048) if r < MOE_TM] + [MOE_TM]
    lower = 0
    for upper in variants:
        fits = jnp.logical_and(n_rows > lower, n_rows <= upper)

        @pl.when(jnp.logical_and(used, fits))
        def _(upper=upper):
            _swiglu_accumulate(xn_ref, _cast_once(wg_ref, wg16_ref), _cast_once(wu_ref, wu16_ref),
                               _cast_once(wd_ref, wd16_ref), acc_ref, n_rows=upper)
        lower = upper


def _moe_experts(tok_pad, blk_expert, blk_rows, n_used, hn, w_gu, w_down):
    d = D_MODEL
    n_blocks = tok_pad.shape[0] // MOE_TM
    nj = D_FF // MOE_TF

    def col(i, j, nused):
        return jnp.where(i < nused[0], j, nj - 1)

    return pl.pallas_call(
        _experts_body,
        grid_spec=pltpu.PrefetchScalarGridSpec(
            num_scalar_prefetch=4,
            grid=(n_blocks, nj),
            in_specs=[
                pl.BlockSpec(memory_space=pl.ANY),
                pl.BlockSpec((1, d, MOE_TF), lambda i, j, tok, be, br, nu: (be[i], 0, col(i, j, nu))),
                pl.BlockSpec((1, d, MOE_TF),
                             lambda i, j, tok, be, br, nu: (be[i], 0, nj + col(i, j, nu))),
                pl.BlockSpec((1, MOE_TF, d), lambda i, j, tok, be, br, nu: (be[i], col(i, j, nu), 0)),
            ],
            out_specs=pl.BlockSpec((MOE_TM, d), lambda i, j, tok, be, br, nu: (i, 0),
                                   pipeline_mode=pl.Buffered(1)),
            scratch_shapes=[pltpu.VMEM((MOE_TM, d), F32), pltpu.VMEM((MOE_TM, d), BF16),
                            pltpu.VMEM((d, MOE_TF), BF16), pltpu.VMEM((d, MOE_TF), BF16),
                            pltpu.VMEM((MOE_TF, d), BF16), pltpu.SemaphoreType.DMA],
        ),
        out_shape=jax.ShapeDtypeStruct((n_blocks * MOE_TM, d), F32),
        compiler_params=_params("arbitrary", "arbitrary"),
        name="moe_experts",
    )(tok_pad, blk_expert, blk_rows, n_used, hn, w_gu, w_gu, w_down)


def _combine_body(p0_ref, p1_ref, ys_hbm, route_ref, x_ref, g_ref, o_ref, a_ref, b_ref, sem):
    tm = x_ref.shape[0]
    base = pl.program_id(0) * tm
    _start_row_gather(p0_ref, base, ys_hbm, a_ref, sem.at[0])
    _start_row_gather(p1_ref, base, ys_hbm, b_ref, sem.at[1])
    _wait_row_gather(p0_ref, base, ys_hbm, a_ref, sem.at[0])
    _wait_row_gather(p1_ref, base, ys_hbm, b_ref, sem.at[1])
    gate0 = route_ref[:, TOP_K:TOP_K + 1]
    gate1 = route_ref[:, TOP_K + 1:TOP_K + 2]
    a_ref[...] = a_ref[...] * gate0 + b_ref[...] * gate1
    _residual_norm_rows(x_ref, a_ref, g_ref, o_ref)


def _moe_combine(pos0, pos1, ys, route, x, gain, *, tm=256):
    m, d = x.shape
    return pl.pallas_call(
        _combine_body,
        grid_spec=pltpu.PrefetchScalarGridSpec(
            num_scalar_prefetch=2,
            grid=(m // tm,),
            in_specs=[
                pl.BlockSpec(memory_space=pl.ANY),
                pl.BlockSpec((tm, LANES), lambda i, p0, p1: (i, 0)),
                pl.BlockSpec((tm, d), lambda i, p0, p1: (i, 0)),
                pl.BlockSpec((1, d), lambda i, p0, p1: (0, 0)),
            ],
            out_specs=pl.BlockSpec((tm, d), lambda i, p0, p1: (i, 0)),
            scratch_shapes=[pltpu.VMEM((tm, d), F32), pltpu.VMEM((tm, d), F32),
                            pltpu.SemaphoreType.DMA((2,))],
        ),
        out_shape=jax.ShapeDtypeStruct((m, d), F32),
        compiler_params=_params("arbitrary"),
        name="moe_combine",
    )(pos0, pos1, ys, route, x, gain.reshape(1, d))


def _dispatch_plan(route):
    n = route.shape[0]
    experts = route[:, :TOP_K].astype(jnp.int32).reshape(-1)
    onehot = (experts[:, None] == jnp.arange(N_EXPERTS)[None, :]).astype(jnp.int32)
    rank = jnp.take_along_axis(jnp.cumsum(onehot, axis=0) - onehot, experts[:, None], axis=1)[:, 0]
    counts = jnp.sum(onehot, axis=0)
    blocks_per_expert = (counts + MOE_TM - 1) // MOE_TM
    block_end = jnp.cumsum(blocks_per_expert)
    block_start = block_end - blocks_per_expert
    dest = block_start[experts] * MOE_TM + rank
    n_blocks = (n * TOP_K) // MOE_TM + N_EXPERTS
    tok_pad = jnp.zeros((n_blocks * MOE_TM,), jnp.int32).at[dest].set(jnp.arange(n * TOP_K) // TOP_K)
    blocks = jnp.arange(n_blocks)
    blk_expert = jnp.clip(jnp.searchsorted(block_end, blocks, side='right'),
                          0, N_EXPERTS - 1).astype(jnp.int32)
    blk_rows = jnp.clip(counts[blk_expert] - (blocks - block_start[blk_expert]) * MOE_TM,
                        0, MOE_TM).astype(jnp.int32)
    n_used = block_end[-1].astype(jnp.int32)
    blk_expert = jnp.where(blocks < n_used, blk_expert, blk_expert[jnp.maximum(n_used - 1, 0)])
    pos = dest.reshape(n, TOP_K)
    return tok_pad, blk_expert, blk_rows, n_used.reshape(1), pos[:, 0], pos[:, 1]


def kernel(x, norm_gains, a_w_in, a_b_in, a_norm_v, a_w_s, a_b_s, a_w_out, kv_norm, w_kv,
           b_w_q, b_w_o, ffn_w_gu, ffn_w_down, moe_router, moe_w_gu, moe_w_down):
    batch, seq, d = x.shape
    h = x.reshape(batch * seq, d)
    bf = lambda w: w.astype(BF16)

    g = norm_gains[0]
    (z, ssq), (w_out, w_gu, w_down) = _gmlp_in(
        h, g[0], bf(a_w_in[0]), a_b_in[0], [a_w_out[0], ffn_w_gu[0], ffn_w_down[0]])
    h, (w_kv16, w_q) = _gmlp_out(z, ssq, a_norm_v[0], a_w_s[0], a_b_s[0], h, g[1], w_out,
                                 [_whole_rider(w, 64) for w in (w_kv, b_w_q[0])])
    h, (w_o,) = _ffn_dense(h, g[2], w_gu, w_down, g[3], [_whole_rider(b_w_o[0], 112)])

    g = norm_gains[1]
    kv, _ = _norm_proj(h, kv_norm, w_kv16, [], tm=1024, tn=1024, name="proj_kv")
    q, _ = _norm_proj(h, g[0], w_q, [], tm=1024, tn=1024, name="proj_q")
    base = jnp.exp2(-8.0 * jnp.arange(1, N_KV_HEADS + 1, dtype=F32) / N_KV_HEADS)
    o, _ = _attention(base, q, kv, batch, seq, [])
    h = _attn_out(o, h, g[1], w_o)

    route, hn = _moe_route(h, g[2], moe_router[0])
    tok_pad, blk_expert, blk_rows, n_used, pos0, pos1 = _dispatch_plan(route)
    ys = _moe_experts(tok_pad, blk_expert, blk_rows, n_used, hn, moe_w_gu[0], moe_w_down[0])
    h = _moe_combine(pos0, pos1, ys, route, h, g[3])
    return h.reshape(batch, seq, d)
```

```python
import functools
from typing import NamedTuple

import jax
import jax.numpy as jnp
from jax import lax
from jax.experimental import pallas as pl
from jax.experimental.pallas import tpu as pltpu

F32 = jnp.float32
BF16 = jnp.bfloat16

D_MODEL = 2048
EPS = 1e-6
CHUNK = 128
GMLP_HALF = 2 * D_MODEL
GMLP_GROUPS = 8
GMLP_GROUP_CH = GMLP_HALF // GMLP_GROUPS
HEAD_DIM = 128
N_KV_HEADS = D_MODEL // HEAD_DIM
DILATED_GROUPS = ((128, 1), (512, 4), (2048, 16))
BLOCK = 128
NEG = -1e30
D_FF = 7168
N_EXPERTS = 8
TOP_K = 2

LANES = 128
VMEM_LIMIT_BYTES = 56 * 1024 * 1024

NORM_ROWS = 128
EPILOGUE_ROWS = 128
MOE_TM = 1152
MOE_TF = 512


def _params(*sem):
    return pltpu.CompilerParams(dimension_semantics=sem, vmem_limit_bytes=VMEM_LIMIT_BYTES)


def _rms_scale(x):
    return lax.rsqrt(jnp.mean(x * x, axis=-1, keepdims=True) + EPS)


def _norm_rows_into(x_ref, g_ref, out_ref):
    def body(c, carry):
        rows = pl.ds(pl.multiple_of(c * NORM_ROWS, NORM_ROWS), NORM_ROWS)
        x = x_ref[rows, :]
        out_ref[rows, :] = ((x * _rms_scale(x)) * g_ref[...]).astype(out_ref.dtype)
        return carry
    lax.fori_loop(0, x_ref.shape[0] // NORM_ROWS, body, 0)


def _residual_norm_rows(res_ref, acc_ref, g_ref, out_ref):
    def body(c, carry):
        rows = pl.ds(pl.multiple_of(c * NORM_ROWS, NORM_ROWS), NORM_ROWS)
        a = acc_ref[rows, :]
        out_ref[rows, :] = res_ref[rows, :] + (a * _rms_scale(a)) * g_ref[...]
        return carry
    lax.fori_loop(0, acc_ref.shape[0] // NORM_ROWS, body, 0)


BF16_SUBLANES = 16


class _Rider(NamedTuple):
    src: jax.Array
    dst: jax.Array | None
    first_row: int
    steps: int
    rows: int


def _whole_rider(w, host_steps):
    total = w.shape[0]
    for steps in range(host_steps, 0, -1):
        if total % steps == 0 and (total // steps) % BF16_SUBLANES == 0:
            return _Rider(w, None, 0, steps, total // steps)
    raise ValueError(f"no chunking of {w.shape} over {host_steps} steps")


def _part_rider(w, dst, first_row, steps, rows):
    assert rows % BF16_SUBLANES == 0 and first_row % rows == 0
    assert first_row + steps * rows <= w.shape[0]
    return _Rider(w, dst, first_row, steps, rows)


def _riding_body(body, n_in, n_out, n_riders, n_alias, *refs):
    ins, refs = refs[:n_in], refs[n_in:]
    r_in, refs = refs[:n_riders], refs[n_riders + n_alias:]
    outs, refs = refs[:n_out], refs[n_out:]
    r_out, scratch = refs[:n_riders], refs[n_riders:]

    def cast_part(part, n_parts):
        for src, dst in zip(r_in, r_out):
            width = src.shape[1] // n_parts
            assert width % LANES == 0 and width * n_parts == src.shape[1]
            cols = slice(part * width, (part + 1) * width)
            dst[:, cols] = src[:, cols].astype(BF16)
    body(cast_part, *ins, *outs, *scratch)


def _hosted_call(body, *, grid, in_specs, out_specs, out_shape, scratch_shapes, args, riders,
                 semantics, name, num_scalar_prefetch=0):
    n_grid = len(grid)
    strides = [1] * n_grid
    for ax in range(n_grid - 2, -1, -1):
        strides[ax] = strides[ax + 1] * grid[ax + 1]
    assert all(r.steps <= strides[0] * grid[0] for r in riders)

    r_in, r_alias, r_out, r_shapes = [], [], [], []
    for r in riders:
        def index(*g, r=r):
            step = sum(g[ax] * strides[ax] for ax in range(n_grid))
            return r.first_row // r.rows + jnp.minimum(step, r.steps - 1), 0
        block = (r.rows, r.src.shape[1])
        r_in.append(pl.BlockSpec(block, index))
        r_out.append(pl.BlockSpec(block, index))
        r_shapes.append(jax.ShapeDtypeStruct(r.src.shape, BF16))
        if r.dst is not None:
            r_alias.append(r.dst)

    n_in = num_scalar_prefetch + len(in_specs)
    n_out = len(out_specs)
    aliases, k = {}, 0
    for ridx, r in enumerate(riders):
        if r.dst is not None:
            aliases[n_in + len(riders) + k] = n_out + ridx
            k += 1
    all_in = list(in_specs) + r_in + [pl.BlockSpec(memory_space=pl.ANY)] * len(r_alias)
    all_out = list(out_specs) + r_out
    kernel_fn = functools.partial(_riding_body, body, n_in, n_out, len(riders), len(r_alias))
    if num_scalar_prefetch:
        spec = dict(grid_spec=pltpu.PrefetchScalarGridSpec(
            num_scalar_prefetch=num_scalar_prefetch, grid=grid, in_specs=all_in,
            out_specs=all_out, scratch_shapes=scratch_shapes))
    else:
        spec = dict(grid=grid, in_specs=all_in, out_specs=all_out, scratch_shapes=scratch_shapes)
    outs = pl.pallas_call(
        kernel_fn,
        out_shape=list(out_shape) + r_shapes,
        input_output_aliases=aliases,
        compiler_params=_params(*semantics),
        name=name,
        **spec,
    )(*args, *[r.src for r in riders], *r_alias)
    return outs[:n_out], outs[n_out:]


def _gelu_tanh(x):
    cdf = 0.5 * (1.0 + jnp.tanh(0.7978845608028654 * (x + 0.044715 * (x * x * x))))
    return x * cdf


def _row_chunks(ref, chunk=EPILOGUE_ROWS):
    n = ref.shape[0] // chunk
    return [(c, n, slice(c * chunk, (c + 1) * chunk)) for c in range(n)]


def _proj_body(cast_part, x_ref, g_ref, w_ref, o_ref, xn_ref):
    @pl.when(pl.program_id(1) == 0)
    def _():
        _norm_rows_into(x_ref, g_ref, xn_ref)
    for c, n, rows in _row_chunks(xn_ref, 4 * EPILOGUE_ROWS):
        o_ref[rows, :] = jnp.dot(xn_ref[rows, :], w_ref[...],
                                 preferred_element_type=F32).astype(o_ref.dtype)
        cast_part(c, n)


def _norm_proj(x, gain, w, riders, *, tm, tn, name):
    m, k = x.shape
    n = w.shape[1]
    (out,), casts = _hosted_call(
        _proj_body,
        grid=(m // tm, n // tn),
        in_specs=[
            pl.BlockSpec((tm, k), lambda i, j: (i, 0)),
            pl.BlockSpec((1, k), lambda i, j: (0, 0)),
            pl.BlockSpec((k, tn), lambda i, j: (0, j)),
        ],
        out_specs=[pl.BlockSpec((tm, tn), lambda i, j: (i, j))],
        out_shape=[jax.ShapeDtypeStruct((m, n), BF16)],
        scratch_shapes=[pltpu.VMEM((tm, k), BF16)],
        args=(x, gain.reshape(1, k), w),
        riders=riders,
        semantics=("arbitrary", "arbitrary"),
        name=name,
    )
    return out, casts


def _gmlp_in_body(cast_part, x_ref, g_ref, w_ref, b_ref, z_ref, ssq_ref, xn_ref, ss_ref):
    j = pl.program_id(1)
    nj = pl.num_programs(1)

    @pl.when(j == 0)
    def _():
        _norm_rows_into(x_ref, g_ref, xn_ref)
        ss_ref[...] = jnp.zeros_like(ss_ref)

    is_v = (j >= nj // 2).astype(F32)
    for c, n, rows in _row_chunks(xn_ref, 2 * EPILOGUE_ROWS):
        z = jnp.dot(xn_ref[rows, :], w_ref[...], preferred_element_type=F32) + b_ref[...]
        z = _gelu_tanh(z)
        z_ref[rows, :] = z.astype(z_ref.dtype)
        ss_ref[rows, :] += is_v * jnp.sum(z * z, axis=-1, keepdims=True)
        cast_part(c, n)

    @pl.when(j == nj - 1)
    def _():
        ssq_ref[...] = jnp.broadcast_to(ss_ref[...], ssq_ref.shape)


def _gmlp_in(x, gain, w_in, b_in, later_weights, *, tm=1024, tn=512):
    m, k = x.shape
    n = w_in.shape[1]
    grid = (m // tm, n // tn)
    return _hosted_call(
        _gmlp_in_body,
        grid=grid,
        in_specs=[
            pl.BlockSpec((tm, k), lambda i, j: (i, 0)),
            pl.BlockSpec((1, k), lambda i, j: (0, 0)),
            pl.BlockSpec((k, tn), lambda i, j: (0, j)),
            pl.BlockSpec((1, tn), lambda i, j: (0, j)),
        ],
        out_specs=[
            pl.BlockSpec((tm, tn), lambda i, j: (i, j)),
            pl.BlockSpec((tm, LANES), lambda i, j: (i, 0)),
        ],
        out_shape=[
            jax.ShapeDtypeStruct((m, n), BF16),
            jax.ShapeDtypeStruct((m, LANES), F32),
        ],
        scratch_shapes=[pltpu.VMEM((tm, k), BF16), pltpu.VMEM((tm, 1), F32)],
        args=(x, gain.reshape(1, k), w_in, b_in.reshape(1, n)),
        riders=[_whole_rider(w, grid[0] * grid[1]) for w in later_weights],
        semantics=("arbitrary", "arbitrary"),
        name="gmlp_in",
    )


def _gmlp_out_body(cast_part, u_ref, v_ref, ssq_ref, gv_ref, ws_ref, bs_ref, x_ref, g_ref, wo_ref,
                   o_ref):
    grp = pl.program_id(1)
    tm = u_ref.shape[0]
    acc_ref = o_ref

    @pl.when(grp == 0)
    def _():
        acc_ref[...] = jnp.zeros_like(acc_ref)

    row = lax.broadcasted_iota(jnp.int32, (CHUNK, CHUNK), 0)
    col = lax.broadcasted_iota(jnp.int32, (CHUNK, CHUNK), 1)
    ws = jnp.where(row >= col, ws_ref[0], 0.0).astype(BF16)
    r = lax.rsqrt(ssq_ref[:, 0:1] * (1.0 / GMLP_HALF) + EPS)
    def project(rows, gated):
        acc_ref[rows, :] += jnp.dot(gated, wo_ref[...], preferred_element_type=F32)

    pending = None
    n = tm // CHUNK
    for c in range(n):
        rows = slice(c * CHUNK, (c + 1) * CHUNK)
        vn = (v_ref[rows, :].astype(F32) * r[rows, :]) * gv_ref[...]
        mixed = jnp.dot(ws, vn.astype(BF16), preferred_element_type=F32) + bs_ref[0]
        gated = (u_ref[rows, :].astype(F32) * mixed).astype(BF16)
        if pending is not None:
            project(*pending)
        pending = (rows, gated)
        cast_part(c, n)
    project(*pending)

    @pl.when(grp == pl.num_programs(1) - 1)
    def _():
        _residual_norm_rows(x_ref, acc_ref, g_ref, o_ref)


def _gmlp_out(z, ssq, norm_v, w_s, b_s, x, gain, w_out, riders, *, tm=1024):
    m, d = x.shape
    gc = GMLP_GROUP_CH
    (out,), casts = _hosted_call(
        _gmlp_out_body,
        grid=(m // tm, GMLP_GROUPS),
        in_specs=[
            pl.BlockSpec((tm, gc), lambda i, g: (i, g)),
            pl.BlockSpec((tm, gc), lambda i, g: (i, GMLP_GROUPS + g)),
            pl.BlockSpec((tm, LANES), lambda i, g: (i, 0)),
            pl.BlockSpec((1, gc), lambda i, g: (0, g)),
            pl.BlockSpec((1, CHUNK, CHUNK), lambda i, g: (g, 0, 0)),
            pl.BlockSpec((1, CHUNK, 1), lambda i, g: (g, 0, 0)),
            pl.BlockSpec((tm, d), lambda i, g: (i, 0)),
            pl.BlockSpec((1, d), lambda i, g: (0, 0)),
            pl.BlockSpec((gc, d), lambda i, g: (g, 0)),
        ],
        out_specs=[pl.BlockSpec((tm, d), lambda i, g: (i, 0))],
        out_shape=[jax.ShapeDtypeStruct((m, d), F32)],
        scratch_shapes=[],
        args=(z, z, ssq, norm_v.reshape(1, GMLP_HALF), w_s, b_s[:, :, None], x,
              gain.reshape(1, d), w_out),
        riders=riders,
        semantics=("arbitrary", "arbitrary"),
        name="gmlp_out",
    )
    return out, casts


SWIGLU_ROWS = 256


def _chunks_covering(n_rows):
    assert n_rows % EPILOGUE_ROWS == 0
    bounds = list(range(0, n_rows, SWIGLU_ROWS)) + [n_rows]
    return [slice(lo, hi) for lo, hi in zip(bounds[:-1], bounds[1:])]


def _swiglu_accumulate(xn_ref, wg, wu, wd, acc_ref, cast_part=None, n_rows=None):
    chunks = _chunks_covering(n_rows or xn_ref.shape[0])

    def down(rows, a):
        acc_ref[rows, :] += jnp.dot(a, wd(), preferred_element_type=F32)

    pending = None
    for c, rows in enumerate(chunks):
        x = xn_ref[rows, :]
        g = jnp.dot(x, wg(), preferred_element_type=F32)
        u = jnp.dot(x, wu(), preferred_element_type=F32)
        a = ((g * jax.nn.sigmoid(g)) * u).astype(BF16)
        if pending is not None:
            down(*pending)
        pending = (rows, a)
        if cast_part is not None:
            cast_part(c, len(chunks))
    down(*pending)


def _cast_once(src_ref, dst_ref):
    done = []

    def get():
        if not done:
            dst_ref[...] = src_ref[0].astype(BF16)
            done.append(True)
        return dst_ref[...]
    return get


def _ffn_body(cast_part, x_ref, g_in_ref, wg_ref, wu_ref, wd_ref, g_out_ref, o_ref, xn_ref):
    j = pl.program_id(1)
    acc_ref = o_ref

    @pl.when(j == 0)
    def _():
        _norm_rows_into(x_ref, g_in_ref, xn_ref)
        acc_ref[...] = jnp.zeros_like(acc_ref)

    _swiglu_accumulate(xn_ref, lambda: wg_ref[...], lambda: wu_ref[...], lambda: wd_ref[...],
                       acc_ref, cast_part)

    @pl.when(j == pl.num_programs(1) - 1)
    def _():
        _residual_norm_rows(x_ref, acc_ref, g_out_ref, o_ref)


def _ffn_dense(x, g_in, w_gu, w_down, g_out, riders, *, tm=1024, tf=512):
    m, d = x.shape
    nj = D_FF // tf
    once = dict(pipeline_mode=pl.Buffered(1))
    (out,), casts = _hosted_call(
        _ffn_body,
        grid=(m // tm, nj),
        in_specs=[
            pl.BlockSpec((tm, d), lambda i, j: (i, 0), **once),
            pl.BlockSpec((1, d), lambda i, j: (0, 0)),
            pl.BlockSpec((d, tf), lambda i, j: (0, j)),
            pl.BlockSpec((d, tf), lambda i, j: (0, nj + j)),
            pl.BlockSpec((tf, d), lambda i, j: (j, 0)),
            pl.BlockSpec((1, d), lambda i, j: (0, 0)),
        ],
        out_specs=[pl.BlockSpec((tm, d), lambda i, j: (i, 0), **once)],
        out_shape=[jax.ShapeDtypeStruct((m, d), F32)],
        scratch_shapes=[pltpu.VMEM((tm, d), BF16)],
        args=(x, g_in.reshape(1, d), w_gu, w_gu, w_down, g_out.reshape(1, d)),
        riders=riders,
        semantics=("arbitrary", "arbitrary"),
        name="ffn_dense",
    )
    return out, casts


SUB = 4
ATTN_UNROLL = 8


def _deinterleave(src_ref, dst_ref, span):
    part = span // SUB
    for base in range(0, src_ref.shape[0], span):
        for r in range(SUB):
            dst_ref[base + r * part: base + (r + 1) * part, :] = (
                src_ref[pl.ds(base + r, part, stride=SUB), :].astype(dst_ref.dtype))


def _interleave(src_ref, dst_ref, span):
    part = span // SUB
    for base in range(0, src_ref.shape[0], span):
        for r in range(SUB):
            dst_ref[pl.ds(base + r, part, stride=SUB), :] = (
                src_ref[base + r * part: base + (r + 1) * part, :])


def _window_attention(q_ref, k_ref, v_ref, slope, dilation, blocks_per_seq,
                      bias_c_ref, bias_p_ref, pc_ref, pp_ref, den_ref, o_ref, lse_ref):
    jq = lax.broadcasted_iota(jnp.int32, (BLOCK, BLOCK), 0)
    kk = lax.broadcasted_iota(jnp.int32, (BLOCK, BLOCK), 1)
    dist_c = ((jq - kk) * dilation).astype(F32)
    dist_p = ((BLOCK + jq - kk) * dilation).astype(F32)
    bias_c_ref[...] = jnp.where(kk <= jq, -(slope * dist_c), NEG)
    bias_p_ref[...] = jnp.where(kk >= jq, -(slope * dist_p), NEG)
    scale = HEAD_DIM ** -0.5
    contract_last = (((1,), (1,)), ((), ()))
    with_prev = blocks_per_seq > 1

    def block_rows(n):
        rows = pl.ds(pl.multiple_of(n * BLOCK, BLOCK), BLOCK)
        prev = pl.ds(pl.multiple_of(jnp.maximum(n - 1, 0) * BLOCK, BLOCK), BLOCK)
        return rows, prev

    def probabilities(n, carry):
        rows, prev = block_rows(n)
        q = q_ref[rows, :]
        s_c = lax.dot_general(q, k_ref[rows, :], contract_last, preferred_element_type=F32)
        s_c = s_c * scale + bias_c_ref[...]
        if with_prev:
            has_prev = (n % blocks_per_seq) != 0
            s_p = lax.dot_general(q, k_ref[prev, :], contract_last, preferred_element_type=F32)
            s_p = jnp.where(has_prev, s_p * scale + bias_p_ref[...], NEG)
            m = jnp.max(jnp.maximum(s_c, s_p), axis=-1, keepdims=True)
            p_c = jnp.exp(s_c - m)
            p_p = jnp.exp(s_p - m)
            den = jnp.sum(p_c + p_p, axis=-1, keepdims=True)
            pp_ref[rows, :] = p_p.astype(BF16)
        else:
            m = jnp.max(s_c, axis=-1, keepdims=True)
            p_c = jnp.exp(s_c - m)
            den = jnp.sum(p_c, axis=-1, keepdims=True)
        pc_ref[rows, :] = p_c.astype(BF16)
        den_ref[rows, :] = den
        lse_ref[rows, :] = jnp.broadcast_to(m + jnp.log(den), (BLOCK, LANES))
        return carry

    def values(n, carry):
        rows, prev = block_rows(n)
        o = jnp.dot(pc_ref[rows, :], v_ref[rows, :], preferred_element_type=F32)
        if with_prev:
            o = o + jnp.dot(pp_ref[rows, :], v_ref[prev, :], preferred_element_type=F32)
        o_ref[rows, :] = o / den_ref[rows, :]
        return carry

    n_blocks = q_ref.shape[0] // BLOCK
    lax.fori_loop(0, n_blocks, probabilities, 0, unroll=ATTN_UNROLL)
    lax.fori_loop(0, n_blocks, values, 0, unroll=ATTN_UNROLL)


def _attn_body(cast_part, base_ref, q1_ref, q4_ref, q16_ref, k_ref, v_ref, out_ref,
               stage_ref, tmp_ref, k4f_ref, v4f_ref,
               qp_ref, k4_ref, v4_ref, k16_ref, v16_ref,
               o1_ref, l1_ref, o4_ref, l4_ref, o16_ref, l16_ref, op_ref, lp_ref,
               bias_c_ref, bias_p_ref, pc_ref, pp_ref, den_ref):
    seq = k_ref.shape[0]
    base = base_ref[pl.program_id(1)]
    (_, d1), (_, d4), (_, d16) = DILATED_GROUPS
    cast_part(0, 1)

    stage_ref[...] = k_ref[...].astype(F32)
    _deinterleave(stage_ref, k4f_ref, seq)
    k4_ref[...] = k4f_ref[...].astype(BF16)
    _deinterleave(k4f_ref, k16_ref, seq // SUB)
    stage_ref[...] = v_ref[...].astype(F32)
    _deinterleave(stage_ref, v4f_ref, seq)
    v4_ref[...] = v4f_ref[...].astype(BF16)
    _deinterleave(v4f_ref, v16_ref, seq // SUB)

    _window_attention(q1_ref, k_ref, v_ref, base / d1, d1, seq // BLOCK,
                      bias_c_ref, bias_p_ref, pc_ref, pp_ref, den_ref, o1_ref, l1_ref)

    stage_ref[...] = q4_ref[...].astype(F32)
    _deinterleave(stage_ref, qp_ref, seq)
    _window_attention(qp_ref, k4_ref, v4_ref, base / d4, d4, seq // d4 // BLOCK,
                      bias_c_ref, bias_p_ref, pc_ref, pp_ref, den_ref, op_ref, lp_ref)
    _interleave(op_ref, o4_ref, seq)
    _interleave(lp_ref, l4_ref, seq)

    stage_ref[...] = q16_ref[...].astype(F32)
    _deinterleave(stage_ref, tmp_ref, seq)
    _deinterleave(tmp_ref, qp_ref, seq // SUB)
    _window_attention(qp_ref, k16_ref, v16_ref, base / d16, d16, seq // d16 // BLOCK,
                      bias_c_ref, bias_p_ref, pc_ref, pp_ref, den_ref, op_ref, lp_ref)
    _interleave(op_ref, tmp_ref, seq // SUB)
    _interleave(tmp_ref, o16_ref, seq)
    _interleave(lp_ref, tmp_ref, seq // SUB)
    _interleave(tmp_ref, l16_ref, seq)

    chunk = 2 * BLOCK

    def merge(c, carry):
        rows = pl.ds(pl.multiple_of(c * chunk, chunk), chunk)
        la, lb, lc = l1_ref[rows, :], l4_ref[rows, :], l16_ref[rows, :]
        mx = jnp.maximum(jnp.maximum(la, lb), lc)
        ea, eb, ec = jnp.exp(la - mx), jnp.exp(lb - mx), jnp.exp(lc - mx)
        den = ea + eb + ec
        o = (ea / den) * o1_ref[rows, :] + (eb / den) * o4_ref[rows, :] + (ec / den) * o16_ref[rows, :]
        out_ref[rows, :] = o.astype(out_ref.dtype)
        return carry
    lax.fori_loop(0, seq // chunk, merge, 0)


def _attention(base, q, kv, batch, seq, riders):
    for window, dilation in DILATED_GROUPS:
        assert window // dilation == BLOCK, "keys per query must span exactly one previous block"
    assert [d for _, d in DILATED_GROUPS] == [1, SUB, SUB * SUB]
    assert seq % (SUB * SUB * BLOCK) == 0
    h = N_KV_HEADS
    blk = (seq, HEAD_DIM)
    f32buf = pltpu.VMEM((seq, HEAD_DIM), F32)
    bf16buf = pltpu.VMEM((seq, HEAD_DIM), BF16)
    (out,), casts = _hosted_call(
        _attn_body,
        num_scalar_prefetch=1,
        grid=(batch, h),
        in_specs=[
            pl.BlockSpec(blk, lambda b, hh, base: (b, hh)),
            pl.BlockSpec(blk, lambda b, hh, base: (b, h + hh)),
            pl.BlockSpec(blk, lambda b, hh, base: (b, 2 * h + hh)),
            pl.BlockSpec(blk, lambda b, hh, base: (b, hh)),
            pl.BlockSpec(blk, lambda b, hh, base: (b, h + hh)),
        ],
        out_specs=[pl.BlockSpec(blk, lambda b, hh, base: (b, hh))],
        out_shape=[jax.ShapeDtypeStruct((batch * seq, D_MODEL), BF16)],
        scratch_shapes=[f32buf] * 4 + [bf16buf] * 5 + [f32buf] * 8
                       + [pltpu.VMEM((BLOCK, BLOCK), F32)] * 2
                       + [bf16buf] * 2 + [pltpu.VMEM((seq, 1), F32)],
        args=(base, q, q, q, kv, kv),
        riders=riders,
        semantics=("arbitrary", "arbitrary"),
        name="attn",
    )
    return out, casts


def _attn_out_body(o_ref, x_ref, g_ref, wo_ref, out_ref, acc_ref):
    acc_ref[...] = jnp.dot(o_ref[...], wo_ref[...], preferred_element_type=F32)
    _residual_norm_rows(x_ref, acc_ref, g_ref, out_ref)


def _attn_out(o, x, gain, w_o, *, tm=512):
    m, d = x.shape
    row_blk = pl.BlockSpec((tm, d), lambda i: (i, 0))
    return pl.pallas_call(
        _attn_out_body,
        grid=(m // tm,),
        in_specs=[row_blk, row_blk,
                  pl.BlockSpec((1, d), lambda i: (0, 0)),
                  pl.BlockSpec((d, d), lambda i: (0, 0))],
        out_specs=row_blk,
        out_shape=jax.ShapeDtypeStruct((m, d), F32),
        scratch_shapes=[pltpu.VMEM((tm, d), F32)],
        compiler_params=_params("parallel"),
        name="attn_out",
    )(o, x, gain.reshape(1, d), w_o)


def _route_body(x_ref, g_ref, wr_ref, o_ref, xn_ref):
    _norm_rows_into(x_ref, g_ref, xn_ref)
    logits = jnp.dot(xn_ref[...], wr_ref[...], preferred_element_type=F32,
                     precision=lax.Precision.HIGHEST)
    lane = lax.broadcasted_iota(jnp.int32, logits.shape, 1)
    logits = jnp.where(lane < N_EXPERTS, logits, -jnp.inf)
    m1 = jnp.max(logits, axis=-1, keepdims=True)
    i1 = jnp.min(jnp.where(logits == m1, lane, LANES), axis=-1, keepdims=True)
    rest = jnp.where(lane == i1, -jnp.inf, logits)
    m2 = jnp.max(rest, axis=-1, keepdims=True)
    i2 = jnp.min(jnp.where(rest == m2, lane, LANES), axis=-1, keepdims=True)
    t = jnp.exp(m2 - m1)
    den = 1.0 + t
    out = jnp.where(lane == 0, i1.astype(F32), 0.0)
    out = jnp.where(lane == 1, i2.astype(F32), out)
    out = jnp.where(lane == 2, 1.0 / den, out)
    out = jnp.where(lane == 3, t / den, out)
    o_ref[...] = out


def _moe_route(x, gain, w_router, *, tm=512):
    m, d = x.shape
    wr = jnp.zeros((d, LANES), F32).at[:, :N_EXPERTS].set(w_router)
    return pl.pallas_call(
        _route_body,
        grid=(m // tm,),
        in_specs=[pl.BlockSpec((tm, d), lambda i: (i, 0)),
                  pl.BlockSpec((1, d), lambda i: (0, 0)),
                  pl.BlockSpec((d, LANES), lambda i: (0, 0))],
        out_specs=[pl.BlockSpec((tm, LANES), lambda i: (i, 0)),
                   pl.BlockSpec((tm, d), lambda i: (i, 0))],
        out_shape=[jax.ShapeDtypeStruct((m, LANES), F32),
                   jax.ShapeDtypeStruct((m, d), F32)],
        compiler_params=_params("parallel"),
        name="moe_route",
    )(x, gain.reshape(1, d), wr)


def _row_copy(src_hbm, row, dst_ref, r, sem):
    return pltpu.make_async_copy(src_hbm.at[pl.ds(row, 1), :], dst_ref.at[pl.ds(r, 1), :], sem)


GATHER_UNROLL = 8
GATHER_QUEUE = 1
GATHER_SLICES = 12


def _start_row_gather(idx_ref, base, src_hbm, dst_ref, sem, priority, first=0, count=None):
    count = dst_ref.shape[0] if count is None else count

    def start(k, carry):
        r = first + k
        _row_copy(src_hbm, idx_ref[base + r], dst_ref, r, sem).start(priority=priority)
        return carry
    lax.fori_loop(0, count, start, 0, unroll=GATHER_UNROLL)


def _wait_row_gather(idx_ref, base, src_hbm, dst_ref, sem):
    def wait(r, carry):
        _row_copy(src_hbm, idx_ref[base + r], dst_ref, r, sem).wait()
        return carry
    lax.fori_loop(0, dst_ref.shape[0], wait, 0, unroll=GATHER_UNROLL)


def _experts_body(tok_ref, bexp_ref, rows_ref, nused_ref, hn_hbm, wg_ref, wu_ref, wd_ref,
                  ys_ref, xg_ref, xn_ref, wg16_ref, wu16_ref, wd16_ref, sem):
    i = pl.program_id(0)
    j = pl.program_id(1)
    n_used = nused_ref[0]
    used = i < n_used
    acc_ref = ys_ref

    @pl.when(j == 0)
    def _():
        acc_ref[...] = jnp.zeros_like(acc_ref)

    @pl.when(jnp.logical_and(used, j == 0))
    def _():
        @pl.when(i == 0)
        def _():
            _start_row_gather(tok_ref, 0, hn_hbm, xg_ref, sem, GATHER_QUEUE)
        _wait_row_gather(tok_ref, i * MOE_TM, hn_hbm, xg_ref, sem)
        xn_ref[...] = xg_ref[...].astype(BF16)

    per_step = MOE_TM // GATHER_SLICES
    assert GATHER_SLICES < D_FF // MOE_TF and per_step * GATHER_SLICES == MOE_TM

    @pl.when(jnp.logical_and(i + 1 < n_used, jnp.logical_and(j >= 1, j <= GATHER_SLICES)))
    def _():
        _start_row_gather(tok_ref, (i + 1) * MOE_TM, hn_hbm, xg_ref, sem, GATHER_QUEUE,
                          first=(j - 1) * per_step, count=per_step)

    n_rows = rows_ref[i]
    variants = [r for r in (128, 256, 512, 1024, 2048) if r < MOE_TM] + [MOE_TM]
    lower = 0
    for upper in variants:
        fits = jnp.logical_and(n_rows > lower, n_rows <= upper)

        @pl.when(jnp.logical_and(used, fits))
        def _(upper=upper):
            _swiglu_accumulate(xn_ref, _cast_once(wg_ref, wg16_ref), _cast_once(wu_ref, wu16_ref),
                               _cast_once(wd_ref, wd16_ref), acc_ref, n_rows=upper)
        lower = upper


def _moe_experts(tok_pad, blk_expert, blk_rows, n_used, hn, w_gu, w_down):
    d = D_MODEL
    n_blocks = tok_pad.shape[0] // MOE_TM
    nj = D_FF // MOE_TF

    def col(i, j, nused):
        return jnp.where(i < nused[0], j, nj - 1)

    return pl.pallas_call(
        _experts_body,
        grid_spec=pltpu.PrefetchScalarGridSpec(
            num_scalar_prefetch=4,
            grid=(n_blocks, nj),
            in_specs=[
                pl.BlockSpec(memory_space=pl.ANY),
                pl.BlockSpec((1, d, MOE_TF), lambda i, j, tok, be, br, nu: (be[i], 0, col(i, j, nu))),
                pl.BlockSpec((1, d, MOE_TF),
                             lambda i, j, tok, be, br, nu: (be[i], 0, nj + col(i, j, nu))),
                pl.BlockSpec((1, MOE_TF, d), lambda i, j, tok, be, br, nu: (be[i], col(i, j, nu), 0)),
            ],
            out_specs=pl.BlockSpec((MOE_TM, d), lambda i, j, tok, be, br, nu: (i, 0),
                                   pipeline_mode=pl.Buffered(1)),
            scratch_shapes=[pltpu.VMEM((MOE_TM, d), F32), pltpu.VMEM((MOE_TM, d), BF16),
                            pltpu.VMEM((d, MOE_TF), BF16), pltpu.VMEM((d, MOE_TF), BF16),
                            pltpu.VMEM((MOE_TF, d), BF16), pltpu.SemaphoreType.DMA],
        ),
        out_shape=jax.ShapeDtypeStruct((n_blocks * MOE_TM, d), F32),
        compiler_params=_params("arbitrary", "arbitrary"),
        name="moe_experts",
    )(tok_pad, blk_expert, blk_rows, n_used, hn, w_gu, w_gu, w_down)


def _combine_body(p0_ref, p1_ref, ys_hbm, route_ref, x_ref, g_ref, o_ref, a_ref, b_ref, sem):
    tm = x_ref.shape[0]
    base = pl.program_id(0) * tm
    _start_row_gather(p0_ref, base, ys_hbm, a_ref, sem.at[0], 0)
    _start_row_gather(p1_ref, base, ys_hbm, b_ref, sem.at[1], 1)
    _wait_row_gather(p0_ref, base, ys_hbm, a_ref, sem.at[0])
    _wait_row_gather(p1_ref, base, ys_hbm, b_ref, sem.at[1])
    gate0 = route_ref[:, TOP_K:TOP_K + 1]
    gate1 = route_ref[:, TOP_K + 1:TOP_K + 2]
    a_ref[...] = a_ref[...] * gate0 + b_ref[...] * gate1
    _residual_norm_rows(x_ref, a_ref, g_ref, o_ref)


def _moe_combine(pos0, pos1, ys, route, x, gain, *, tm=256):
    m, d = x.shape
    return pl.pallas_call(
        _combine_body,
        grid_spec=pltpu.PrefetchScalarGridSpec(
            num_scalar_prefetch=2,
            grid=(m // tm,),
            in_specs=[
                pl.BlockSpec(memory_space=pl.ANY),
                pl.BlockSpec((tm, LANES), lambda i, p0, p1: (i, 0)),
                pl.BlockSpec((tm, d), lambda i, p0, p1: (i, 0)),
                pl.BlockSpec((1, d), lambda i, p0, p1: (0, 0)),
            ],
            out_specs=pl.BlockSpec((tm, d), lambda i, p0, p1: (i, 0)),
            scratch_shapes=[pltpu.VMEM((tm, d), F32), pltpu.VMEM((tm, d), F32),
                            pltpu.SemaphoreType.DMA((2,))],
        ),
        out_shape=jax.ShapeDtypeStruct((m, d), F32),
        compiler_params=_params("arbitrary"),
        name="moe_combine",
    )(pos0, pos1, ys, route, x, gain.reshape(1, d))


def _dispatch_plan(route):
    n = route.shape[0]
    experts = route[:, :TOP_K].astype(jnp.int32).reshape(-1)
    onehot = (experts[:, None] == jnp.arange(N_EXPERTS)[None, :]).astype(jnp.int32)
    rank = jnp.take_along_axis(jnp.cumsum(onehot, axis=0) - onehot, experts[:, None], axis=1)[:, 0]
    counts = jnp.sum(onehot, axis=0)
    blocks_per_expert = (counts + MOE_TM - 1) // MOE_TM
    block_end = jnp.cumsum(blocks_per_expert)
    block_start = block_end - blocks_per_expert
    dest = block_start[experts] * MOE_TM + rank
    n_blocks = (n * TOP_K) // MOE_TM + N_EXPERTS
    tok_pad = jnp.zeros((n_blocks * MOE_TM,), jnp.int32).at[dest].set(jnp.arange(n * TOP_K) // TOP_K)
    blocks = jnp.arange(n_blocks)
    blk_expert = jnp.clip(jnp.searchsorted(block_end, blocks, side='right'),
                          0, N_EXPERTS - 1).astype(jnp.int32)
    blk_rows = jnp.clip(counts[blk_expert] - (blocks - block_start[blk_expert]) * MOE_TM,
                        0, MOE_TM).astype(jnp.int32)
    n_used = block_end[-1].astype(jnp.int32)
    blk_expert = jnp.where(blocks < n_used, blk_expert, blk_expert[jnp.maximum(n_used - 1, 0)])
    pos = dest.reshape(n, TOP_K)
    return tok_pad, blk_expert, blk_rows, n_used.reshape(1), pos[:, 0], pos[:, 1]


def kernel(x, norm_gains, a_w_in, a_b_in, a_norm_v, a_w_s, a_b_s, a_w_out, kv_norm, w_kv,
           b_w_q, b_w_o, ffn_w_gu, ffn_w_down, moe_router, moe_w_gu, moe_w_down):
    batch, seq, d = x.shape
    h = x.reshape(batch * seq, d)
    bf = lambda w: w.astype(BF16)

    g = norm_gains[0]
    (z, ssq), (w_out, w_gu, w_down) = _gmlp_in(
        h, g[0], bf(a_w_in[0]), a_b_in[0], [a_w_out[0], ffn_w_gu[0], ffn_w_down[0]])
    h, (w_kv16, w_q) = _gmlp_out(z, ssq, a_norm_v[0], a_w_s[0], a_b_s[0], h, g[1], w_out,
                                 [_whole_rider(w, 64) for w in (w_kv, b_w_q[0])])
    h, (w_o,) = _ffn_dense(h, g[2], w_gu, w_down, g[3], [_whole_rider(b_w_o[0], 112)])

    g = norm_gains[1]
    kv, _ = _norm_proj(h, kv_norm, w_kv16, [], tm=1024, tn=1024, name="proj_kv")
    q, _ = _norm_proj(h, g[0], w_q, [], tm=1024, tn=1024, name="proj_q")
    base = jnp.exp2(-8.0 * jnp.arange(1, N_KV_HEADS + 1, dtype=F32) / N_KV_HEADS)
    o, _ = _attention(base, q, kv, batch, seq, [])
    h = _attn_out(o, h, g[1], w_o)

    route, hn = _moe_route(h, g[2], moe_router[0])
    tok_pad, blk_expert, blk_rows, n_used, pos0, pos1 = _dispatch_plan(route)
    ys = _moe_experts(tok_pad, blk_expert, blk_rows, n_used, hn, moe_w_gu[0], moe_w_down[0])
    h = _moe_combine(pos0, pos1, ys, route, h, g[3])
    return h.reshape(batch, seq, d)
```

```python
import functools
from typing import NamedTuple

import jax
import jax.numpy as jnp
from jax import lax
from jax.experimental import pallas as pl
from jax.experimental.pallas import tpu as pltpu

F32 = jnp.float32
BF16 = jnp.bfloat16

D_MODEL = 2048
EPS = 1e-6
CHUNK = 128
GMLP_HALF = 2 * D_MODEL
GMLP_GROUPS = 8
GMLP_GROUP_CH = GMLP_HALF // GMLP_GROUPS
HEAD_DIM = 128
N_KV_HEADS = D_MODEL // HEAD_DIM
DILATED_GROUPS = ((128, 1), (512, 4), (2048, 16))
BLOCK = 128
NEG = -1e30
D_FF = 7168
N_EXPERTS = 8
TOP_K = 2

LANES = 128
VMEM_LIMIT_BYTES = 56 * 1024 * 1024

NORM_ROWS = 128
EPILOGUE_ROWS = 128
MOE_TM = 1152
MOE_TF = 512


def _params(*sem):
    return pltpu.CompilerParams(dimension_semantics=sem, vmem_limit_bytes=VMEM_LIMIT_BYTES)


def _rms_scale(x):
    return lax.rsqrt(jnp.mean(x * x, axis=-1, keepdims=True) + EPS)


def _norm_rows_into(x_ref, g_ref, out_ref):
    def body(c, carry):
        rows = pl.ds(pl.multiple_of(c * NORM_ROWS, NORM_ROWS), NORM_ROWS)
        x = x_ref[rows, :]
        out_ref[rows, :] = ((x * _rms_scale(x)) * g_ref[...]).astype(out_ref.dtype)
        return carry
    lax.fori_loop(0, x_ref.shape[0] // NORM_ROWS, body, 0)


def _residual_norm_rows(res_ref, acc_ref, g_ref, out_ref):
    def body(c, carry):
        rows = pl.ds(pl.multiple_of(c * NORM_ROWS, NORM_ROWS), NORM_ROWS)
        a = acc_ref[rows, :]
        out_ref[rows, :] = res_ref[rows, :] + (a * _rms_scale(a)) * g_ref[...]
        return carry
    lax.fori_loop(0, acc_ref.shape[0] // NORM_ROWS, body, 0)


BF16_SUBLANES = 16


class _Rider(NamedTuple):
    src: jax.Array
    dst: jax.Array | None
    first_row: int
    steps: int
    rows: int


def _whole_rider(w, host_steps):
    total = w.shape[0]
    for steps in range(host_steps, 0, -1):
        if total % steps == 0 and (total // steps) % BF16_SUBLANES == 0:
            return _Rider(w, None, 0, steps, total // steps)
    raise ValueError(f"no chunking of {w.shape} over {host_steps} steps")


def _part_rider(w, dst, first_row, steps, rows):
    assert rows % BF16_SUBLANES == 0 and first_row % rows == 0
    assert first_row + steps * rows <= w.shape[0]
    return _Rider(w, dst, first_row, steps, rows)


def _riding_body(body, n_in, n_out, n_riders, n_alias, *refs):
    ins, refs = refs[:n_in], refs[n_in:]
    r_in, refs = refs[:n_riders], refs[n_riders + n_alias:]
    outs, refs = refs[:n_out], refs[n_out:]
    r_out, scratch = refs[:n_riders], refs[n_riders:]

    def cast_part(part, n_parts):
        for src, dst in zip(r_in, r_out):
            width = src.shape[1] // n_parts
            assert width % LANES == 0 and width * n_parts == src.shape[1]
            cols = slice(part * width, (part + 1) * width)
            dst[:, cols] = src[:, cols].astype(BF16)
    body(cast_part, *ins, *outs, *scratch)


def _hosted_call(body, *, grid, in_specs, out_specs, out_shape, scratch_shapes, args, riders,
                 semantics, name, num_scalar_prefetch=0):
    n_grid = len(grid)
    strides = [1] * n_grid
    for ax in range(n_grid - 2, -1, -1):
        strides[ax] = strides[ax + 1] * grid[ax + 1]
    assert all(r.steps <= strides[0] * grid[0] for r in riders)

    r_in, r_alias, r_out, r_shapes = [], [], [], []
    for r in riders:
        def index(*g, r=r):
            step = sum(g[ax] * strides[ax] for ax in range(n_grid))
            return r.first_row // r.rows + jnp.minimum(step, r.steps - 1), 0
        block = (r.rows, r.src.shape[1])
        r_in.append(pl.BlockSpec(block, index))
        r_out.append(pl.BlockSpec(block, index))
        r_shapes.append(jax.ShapeDtypeStruct(r.src.shape, BF16))
        if r.dst is not None:
            r_alias.append(r.dst)

    n_in = num_scalar_prefetch + len(in_specs)
    n_out = len(out_specs)
    aliases, k = {}, 0
    for ridx, r in enumerate(riders):
        if r.dst is not None:
            aliases[n_in + len(riders) + k] = n_out + ridx
            k += 1
    all_in = list(in_specs) + r_in + [pl.BlockSpec(memory_space=pl.ANY)] * len(r_alias)
    all_out = list(out_specs) + r_out
    kernel_fn = functools.partial(_riding_body, body, n_in, n_out, len(riders), len(r_alias))
    if num_scalar_prefetch:
        spec = dict(grid_spec=pltpu.PrefetchScalarGridSpec(
            num_scalar_prefetch=num_scalar_prefetch, grid=grid, in_specs=all_in,
            out_specs=all_out, scratch_shapes=scratch_shapes))
    else:
        spec = dict(grid=grid, in_specs=all_in, out_specs=all_out, scratch_shapes=scratch_shapes)
    outs = pl.pallas_call(
        kernel_fn,
        out_shape=list(out_shape) + r_shapes,
        input_output_aliases=aliases,
        compiler_params=_params(*semantics),
        name=name,
        **spec,
    )(*args, *[r.src for r in riders], *r_alias)
    return outs[:n_out], outs[n_out:]


def _gelu_tanh(x):
    cdf = 0.5 * (1.0 + jnp.tanh(0.7978845608028654 * (x + 0.044715 * (x * x * x))))
    return x * cdf


def _row_chunks(ref, chunk=EPILOGUE_ROWS):
    n = ref.shape[0] // chunk
    return [(c, n, slice(c * chunk, (c + 1) * chunk)) for c in range(n)]


def _proj_body(cast_part, x_ref, g_ref, w_ref, o_ref, xn_ref):
    @pl.when(pl.program_id(1) == 0)
    def _():
        _norm_rows_into(x_ref, g_ref, xn_ref)
    for c, n, rows in _row_chunks(xn_ref, 4 * EPILOGUE_ROWS):
        o_ref[rows, :] = jnp.dot(xn_ref[rows, :], w_ref[...],
                                 preferred_element_type=F32).astype(o_ref.dtype)
        cast_part(c, n)


def _norm_proj(x, gain, w, riders, *, tm, tn, name):
    m, k = x.shape
    n = w.shape[1]
    (out,), casts = _hosted_call(
        _proj_body,
        grid=(m // tm, n // tn),
        in_specs=[
            pl.BlockSpec((tm, k), lambda i, j: (i, 0), pipeline_mode=pl.Buffered(1)),
            pl.BlockSpec((1, k), lambda i, j: (0, 0)),
            pl.BlockSpec((k, tn), lambda i, j: (0, j)),
        ],
        out_specs=[pl.BlockSpec((tm, tn), lambda i, j: (i, j))],
        out_shape=[jax.ShapeDtypeStruct((m, n), BF16)],
        scratch_shapes=[pltpu.VMEM((tm, k), BF16)],
        args=(x, gain.reshape(1, k), w),
        riders=riders,
        semantics=("arbitrary", "arbitrary"),
        name=name,
    )
    return out, casts


def _gmlp_in_body(cast_part, x_ref, g_ref, w_ref, b_ref, z_ref, ssq_ref, xn_ref, ss_ref):
    j = pl.program_id(1)
    nj = pl.num_programs(1)

    @pl.when(j == 0)
    def _():
        _norm_rows_into(x_ref, g_ref, xn_ref)
        ss_ref[...] = jnp.zeros_like(ss_ref)

    is_v = (j >= nj // 2).astype(F32)
    for c, n, rows in _row_chunks(xn_ref, 2 * EPILOGUE_ROWS):
        z = jnp.dot(xn_ref[rows, :], w_ref[...], preferred_element_type=F32) + b_ref[...]
        z = _gelu_tanh(z)
        z_ref[rows, :] = z.astype(z_ref.dtype)
        ss_ref[rows, :] += is_v * jnp.sum(z * z, axis=-1, keepdims=True)
        cast_part(c, n)

    @pl.when(j == nj - 1)
    def _():
        ssq_ref[...] = jnp.broadcast_to(ss_ref[...], ssq_ref.shape)


def _gmlp_in(x, gain, w_in, b_in, later_weights, *, tm=2048, tn=512):
    m, k = x.shape
    n = w_in.shape[1]
    grid = (m // tm, n // tn)
    return _hosted_call(
        _gmlp_in_body,
        grid=grid,
        in_specs=[
            pl.BlockSpec((tm, k), lambda i, j: (i, 0), pipeline_mode=pl.Buffered(1)),
            pl.BlockSpec((1, k), lambda i, j: (0, 0)),
            pl.BlockSpec((k, tn), lambda i, j: (0, j)),
            pl.BlockSpec((1, tn), lambda i, j: (0, j)),
        ],
        out_specs=[
            pl.BlockSpec((tm, tn), lambda i, j: (i, j)),
            pl.BlockSpec((tm, LANES), lambda i, j: (i, 0)),
        ],
        out_shape=[
            jax.ShapeDtypeStruct((m, n), BF16),
            jax.ShapeDtypeStruct((m, LANES), F32),
        ],
        scratch_shapes=[pltpu.VMEM((tm, k), BF16), pltpu.VMEM((tm, 1), F32)],
        args=(x, gain.reshape(1, k), w_in, b_in.reshape(1, n)),
        riders=[_whole_rider(w, grid[0] * grid[1]) for w in later_weights],
        semantics=("arbitrary", "arbitrary"),
        name="gmlp_in",
    )


def _gmlp_out_body(cast_part, u_ref, v_ref, ssq_ref, gv_ref, ws_ref, bs_ref, x_ref, g_ref, wo_ref,
                   o_ref):
    grp = pl.program_id(1)
    tm = u_ref.shape[0]
    acc_ref = o_ref

    @pl.when(grp == 0)
    def _():
        acc_ref[...] = jnp.zeros_like(acc_ref)

    row = lax.broadcasted_iota(jnp.int32, (CHUNK, CHUNK), 0)
    col = lax.broadcasted_iota(jnp.int32, (CHUNK, CHUNK), 1)
    ws = jnp.where(row >= col, ws_ref[0], 0.0).astype(BF16)
    r = lax.rsqrt(ssq_ref[:, 0:1] * (1.0 / GMLP_HALF) + EPS)
    def project(rows, gated):
        acc_ref[rows, :] += jnp.dot(gated, wo_ref[...], preferred_element_type=F32)

    pending = None
    n = tm // CHUNK
    for c in range(n):
        rows = slice(c * CHUNK, (c + 1) * CHUNK)
        vn = (v_ref[rows, :].astype(F32) * r[rows, :]) * gv_ref[...]
        mixed = jnp.dot(ws, vn.astype(BF16), preferred_element_type=F32) + bs_ref[0]
        gated = (u_ref[rows, :].astype(F32) * mixed).astype(BF16)
        if pending is not None:
            project(*pending)
        pending = (rows, gated)
        cast_part(c, n)
    project(*pending)

    @pl.when(grp == pl.num_programs(1) - 1)
    def _():
        _residual_norm_rows(x_ref, acc_ref, g_ref, o_ref)


def _gmlp_out(z, ssq, norm_v, w_s, b_s, x, gain, w_out, riders, *, tm=1024):
    m, d = x.shape
    gc = GMLP_GROUP_CH
    (out,), casts = _hosted_call(
        _gmlp_out_body,
        grid=(m // tm, GMLP_GROUPS),
        in_specs=[
            pl.BlockSpec((tm, gc), lambda i, g: (i, g)),
            pl.BlockSpec((tm, gc), lambda i, g: (i, GMLP_GROUPS + g)),
            pl.BlockSpec((tm, LANES), lambda i, g: (i, 0)),
            pl.BlockSpec((1, gc), lambda i, g: (0, g)),
            pl.BlockSpec((1, CHUNK, CHUNK), lambda i, g: (g, 0, 0)),
            pl.BlockSpec((1, CHUNK, 1), lambda i, g: (g, 0, 0)),
            pl.BlockSpec((tm, d), lambda i, g: (i, 0)),
            pl.BlockSpec((1, d), lambda i, g: (0, 0)),
            pl.BlockSpec((gc, d), lambda i, g: (g, 0)),
        ],
        out_specs=[pl.BlockSpec((tm, d), lambda i, g: (i, 0))],
        out_shape=[jax.ShapeDtypeStruct((m, d), F32)],
        scratch_shapes=[],
        args=(z, z, ssq, norm_v.reshape(1, GMLP_HALF), w_s, b_s[:, :, None], x,
              gain.reshape(1, d), w_out),
        riders=riders,
        semantics=("arbitrary", "arbitrary"),
        name="gmlp_out",
    )
    return out, casts


SWIGLU_ROWS = 256


def _chunks_covering(n_rows):
    assert n_rows % EPILOGUE_ROWS == 0
    bounds = list(range(0, n_rows, SWIGLU_ROWS)) + [n_rows]
    return [slice(lo, hi) for lo, hi in zip(bounds[:-1], bounds[1:])]


def _swiglu_accumulate(xn_ref, wg, wu, wd, acc_ref, cast_part=None, n_rows=None):
    chunks = _chunks_covering(n_rows or xn_ref.shape[0])

    def down(rows, a):
        acc_ref[rows, :] += jnp.dot(a, wd(), preferred_element_type=F32)

    pending = None
    for c, rows in enumerate(chunks):
        x = xn_ref[rows, :]
        g = jnp.dot(x, wg(), preferred_element_type=F32)
        u = jnp.dot(x, wu(), preferred_element_type=F32)
        a = ((g * jax.nn.sigmoid(g)) * u).astype(BF16)
        if pending is not None:
            down(*pending)
        pending = (rows, a)
        if cast_part is not None:
            cast_part(c, len(chunks))
    down(*pending)


def _cast_once(src_ref, dst_ref):
    done = []

    def get():
        if not done:
            dst_ref[...] = src_ref[0].astype(BF16)
            done.append(True)
        return dst_ref[...]
    return get


def _ffn_body(cast_part, x_ref, g_in_ref, wg_ref, wu_ref, wd_ref, g_out_ref, o_ref, xn_ref):
    j = pl.program_id(1)
    acc_ref = o_ref

    @pl.when(j == 0)
    def _():
        _norm_rows_into(x_ref, g_in_ref, xn_ref)
        acc_ref[...] = jnp.zeros_like(acc_ref)

    _swiglu_accumulate(xn_ref, lambda: wg_ref[...], lambda: wu_ref[...], lambda: wd_ref[...],
                       acc_ref, cast_part)

    @pl.when(j == pl.num_programs(1) - 1)
    def _():
        _residual_norm_rows(x_ref, acc_ref, g_out_ref, o_ref)


def _ffn_dense(x, g_in, w_gu, w_down, g_out, riders, *, tm=1024, tf=1024):
    m, d = x.shape
    nj = D_FF // tf
    once = dict(pipeline_mode=pl.Buffered(1))
    (out,), casts = _hosted_call(
        _ffn_body,
        grid=(m // tm, nj),
        in_specs=[
            pl.BlockSpec((tm, d), lambda i, j: (i, 0), **once),
            pl.BlockSpec((1, d), lambda i, j: (0, 0)),
            pl.BlockSpec((d, tf), lambda i, j: (0, j)),
            pl.BlockSpec((d, tf), lambda i, j: (0, nj + j)),
            pl.BlockSpec((tf, d), lambda i, j: (j, 0)),
            pl.BlockSpec((1, d), lambda i, j: (0, 0)),
        ],
        out_specs=[pl.BlockSpec((tm, d), lambda i, j: (i, 0), **once)],
        out_shape=[jax.ShapeDtypeStruct((m, d), F32)],
        scratch_shapes=[pltpu.VMEM((tm, d), BF16)],
        args=(x, g_in.reshape(1, d), w_gu, w_gu, w_down, g_out.reshape(1, d)),
        riders=riders,
        semantics=("arbitrary", "arbitrary"),
        name="ffn_dense",
    )
    return out, casts


SUB = 4
ATTN_UNROLL = 8


def _deinterleave(src_ref, dst_ref, span):
    part = span // SUB
    for base in range(0, src_ref.shape[0], span):
        for r in range(SUB):
            dst_ref[base + r * part: base + (r + 1) * part, :] = (
                src_ref[pl.ds(base + r, part, stride=SUB), :].astype(dst_ref.dtype))


def _interleave(src_ref, dst_ref, span):
    part = span // SUB
    for base in range(0, src_ref.shape[0], span):
        for r in range(SUB):
            dst_ref[pl.ds(base + r, part, stride=SUB), :] = (
                src_ref[base + r * part: base + (r + 1) * part, :])


def _window_attention(q_ref, k_ref, v_ref, slope, dilation, blocks_per_seq,
                      bias_c_ref, bias_p_ref, pc_ref, pp_ref, den_ref, o_ref, lse_ref):
    jq = lax.broadcasted_iota(jnp.int32, (BLOCK, BLOCK), 0)
    kk = lax.broadcasted_iota(jnp.int32, (BLOCK, BLOCK), 1)
    dist_c = ((jq - kk) * dilation).astype(F32)
    dist_p = ((BLOCK + jq - kk) * dilation).astype(F32)
    bias_c_ref[...] = jnp.where(kk <= jq, -(slope * dist_c), NEG)
    bias_p_ref[...] = jnp.where(kk >= jq, -(slope * dist_p), NEG)
    scale = HEAD_DIM ** -0.5
    contract_last = (((1,), (1,)), ((), ()))
    with_prev = blocks_per_seq > 1

    def block_rows(n):
        rows = pl.ds(pl.multiple_of(n * BLOCK, BLOCK), BLOCK)
        prev = pl.ds(pl.multiple_of(jnp.maximum(n - 1, 0) * BLOCK, BLOCK), BLOCK)
        return rows, prev

    def probabilities(n, carry):
        rows, prev = block_rows(n)
        q = q_ref[rows, :]
        s_c = lax.dot_general(q, k_ref[rows, :], contract_last, preferred_element_type=F32)
        s_c = s_c * scale + bias_c_ref[...]
        if with_prev:
            has_prev = (n % blocks_per_seq) != 0
            s_p = lax.dot_general(q, k_ref[prev, :], contract_last, preferred_element_type=F32)
            s_p = jnp.where(has_prev, s_p * scale + bias_p_ref[...], NEG)
            m = jnp.max(jnp.maximum(s_c, s_p), axis=-1, keepdims=True)
            p_c = jnp.exp(s_c - m)
            p_p = jnp.exp(s_p - m)
            den = jnp.sum(p_c + p_p, axis=-1, keepdims=True)
            pp_ref[rows, :] = p_p.astype(BF16)
        else:
            m = jnp.max(s_c, axis=-1, keepdims=True)
            p_c = jnp.exp(s_c - m)
            den = jnp.sum(p_c, axis=-1, keepdims=True)
        pc_ref[rows, :] = p_c.astype(BF16)
        den_ref[rows, :] = den
        lse_ref[rows, :] = jnp.broadcast_to(m + jnp.log(den), (BLOCK, LANES))
        return carry

    def values(n, carry):
        rows, prev = block_rows(n)
        o = jnp.dot(pc_ref[rows, :], v_ref[rows, :], preferred_element_type=F32)
        if with_prev:
            o = o + jnp.dot(pp_ref[rows, :], v_ref[prev, :], preferred_element_type=F32)
        o_ref[rows, :] = o / den_ref[rows, :]
        return carry

    n_blocks = q_ref.shape[0] // BLOCK
    lax.fori_loop(0, n_blocks, probabilities, 0, unroll=ATTN_UNROLL)
    lax.fori_loop(0, n_blocks, values, 0, unroll=ATTN_UNROLL)


def _attn_body(cast_part, base_ref, q1_ref, q4_ref, q16_ref, k_ref, v_ref, out_ref,
               stage_ref, tmp_ref, k4f_ref, v4f_ref,
               qp_ref, k4_ref, v4_ref, k16_ref, v16_ref,
               o1_ref, l1_ref, o4_ref, l4_ref, o16_ref, l16_ref, op_ref, lp_ref,
               bias_c_ref, bias_p_ref, pc_ref, pp_ref, den_ref):
    seq = k_ref.shape[0]
    base = base_ref[pl.program_id(1)]
    (_, d1), (_, d4), (_, d16) = DILATED_GROUPS
    cast_part(0, 1)

    stage_ref[...] = k_ref[...].astype(F32)
    _deinterleave(stage_ref, k4f_ref, seq)
    k4_ref[...] = k4f_ref[...].astype(BF16)
    _deinterleave(k4f_ref, k16_ref, seq // SUB)
    stage_ref[...] = v_ref[...].astype(F32)
    _deinterleave(stage_ref, v4f_ref, seq)
    v4_ref[...] = v4f_ref[...].astype(BF16)
    _deinterleave(v4f_ref, v16_ref, seq // SUB)

    _window_attention(q1_ref, k_ref, v_ref, base / d1, d1, seq // BLOCK,
                      bias_c_ref, bias_p_ref, pc_ref, pp_ref, den_ref, o1_ref, l1_ref)

    stage_ref[...] = q4_ref[...].astype(F32)
    _deinterleave(stage_ref, qp_ref, seq)
    _window_attention(qp_ref, k4_ref, v4_ref, base / d4, d4, seq // d4 // BLOCK,
                      bias_c_ref, bias_p_ref, pc_ref, pp_ref, den_ref, op_ref, lp_ref)
    _interleave(op_ref, o4_ref, seq)
    _interleave(lp_ref, l4_ref, seq)

    stage_ref[...] = q16_ref[...].astype(F32)
    _deinterleave(stage_ref, tmp_ref, seq)
    _deinterleave(tmp_ref, qp_ref, seq // SUB)
    _window_attention(qp_ref, k16_ref, v16_ref, base / d16, d16, seq // d16 // BLOCK,
                      bias_c_ref, bias_p_ref, pc_ref, pp_ref, den_ref, op_ref, lp_ref)
    _interleave(op_ref, tmp_ref, seq // SUB)
    _interleave(tmp_ref, o16_ref, seq)
    _interleave(lp_ref, tmp_ref, seq // SUB)
    _interleave(tmp_ref, l16_ref, seq)

    chunk = 2 * BLOCK

    def merge(c, carry):
        rows = pl.ds(pl.multiple_of(c * chunk, chunk), chunk)
        la, lb, lc = l1_ref[rows, :], l4_ref[rows, :], l16_ref[rows, :]
        mx = jnp.maximum(jnp.maximum(la, lb), lc)
        ea, eb, ec = jnp.exp(la - mx), jnp.exp(lb - mx), jnp.exp(lc - mx)
        den = ea + eb + ec
        o = (ea / den) * o1_ref[rows, :] + (eb / den) * o4_ref[rows, :] + (ec / den) * o16_ref[rows, :]
        out_ref[rows, :] = o.astype(out_ref.dtype)
        return carry
    lax.fori_loop(0, seq // chunk, merge, 0)


def _attention(base, q, kv, batch, seq, riders):
    for window, dilation in DILATED_GROUPS:
        assert window // dilation == BLOCK, "keys per query must span exactly one previous block"
    assert [d for _, d in DILATED_GROUPS] == [1, SUB, SUB * SUB]
    assert seq % (SUB * SUB * BLOCK) == 0
    h = N_KV_HEADS
    blk = (seq, HEAD_DIM)
    f32buf = pltpu.VMEM((seq, HEAD_DIM), F32)
    bf16buf = pltpu.VMEM((seq, HEAD_DIM), BF16)
    (out,), casts = _hosted_call(
        _attn_body,
        num_scalar_prefetch=1,
        grid=(batch, h),
        in_specs=[
            pl.BlockSpec(blk, lambda b, hh, base: (b, hh)),
            pl.BlockSpec(blk, lambda b, hh, base: (b, h + hh)),
            pl.BlockSpec(blk, lambda b, hh, base: (b, 2 * h + hh)),
            pl.BlockSpec(blk, lambda b, hh, base: (b, hh)),
            pl.BlockSpec(blk, lambda b, hh, base: (b, h + hh)),
        ],
        out_specs=[pl.BlockSpec(blk, lambda b, hh, base: (b, hh))],
        out_shape=[jax.ShapeDtypeStruct((batch * seq, D_MODEL), BF16)],
        scratch_shapes=[f32buf] * 4 + [bf16buf] * 5 + [f32buf] * 8
                       + [pltpu.VMEM((BLOCK, BLOCK), F32)] * 2
                       + [bf16buf] * 2 + [pltpu.VMEM((seq, 1), F32)],
        args=(base, q, q, q, kv, kv),
        riders=riders,
        semantics=("arbitrary", "arbitrary"),
        name="attn",
    )
    return out, casts


def _attn_out_body(o_ref, x_ref, g_ref, wo_ref, out_ref, acc_ref):
    acc_ref[...] = jnp.dot(o_ref[...], wo_ref[...], preferred_element_type=F32)
    _residual_norm_rows(x_ref, acc_ref, g_ref, out_ref)


def _attn_out(o, x, gain, w_o, *, tm=512):
    m, d = x.shape
    row_blk = pl.BlockSpec((tm, d), lambda i: (i, 0))
    return pl.pallas_call(
        _attn_out_body,
        grid=(m // tm,),
        in_specs=[row_blk, row_blk,
                  pl.BlockSpec((1, d), lambda i: (0, 0)),
                  pl.BlockSpec((d, d), lambda i: (0, 0))],
        out_specs=row_blk,
        out_shape=jax.ShapeDtypeStruct((m, d), F32),
        scratch_shapes=[pltpu.VMEM((tm, d), F32)],
        compiler_params=_params("parallel"),
        name="attn_out",
    )(o, x, gain.reshape(1, d), w_o)


def _route_body(x_ref, g_ref, wr_ref, o_ref, xn_ref):
    _norm_rows_into(x_ref, g_ref, xn_ref)
    logits = jnp.dot(xn_ref[...], wr_ref[...], preferred_element_type=F32,
                     precision=lax.Precision.HIGHEST)
    lane = lax.broadcasted_iota(jnp.int32, logits.shape, 1)
    logits = jnp.where(lane < N_EXPERTS, logits, -jnp.inf)
    m1 = jnp.max(logits, axis=-1, keepdims=True)
    i1 = jnp.min(jnp.where(logits == m1, lane, LANES), axis=-1, keepdims=True)
    rest = jnp.where(lane == i1, -jnp.inf, logits)
    m2 = jnp.max(rest, axis=-1, keepdims=True)
    i2 = jnp.min(jnp.where(rest == m2, lane, LANES), axis=-1, keepdims=True)
    t = jnp.exp(m2 - m1)
    den = 1.0 + t
    out = jnp.where(lane == 0, i1.astype(F32), 0.0)
    out = jnp.where(lane == 1, i2.astype(F32), out)
    out = jnp.where(lane == 2, 1.0 / den, out)
    out = jnp.where(lane == 3, t / den, out)
    o_ref[...] = out


def _moe_route(x, gain, w_router, *, tm=512):
    m, d = x.shape
    wr = jnp.zeros((d, LANES), F32).at[:, :N_EXPERTS].set(w_router)
    return pl.pallas_call(
        _route_body,
        grid=(m // tm,),
        in_specs=[pl.BlockSpec((tm, d), lambda i: (i, 0)),
                  pl.BlockSpec((1, d), lambda i: (0, 0)),
                  pl.BlockSpec((d, LANES), lambda i: (0, 0))],
        out_specs=[pl.BlockSpec((tm, LANES), lambda i: (i, 0)),
                   pl.BlockSpec((tm, d), lambda i: (i, 0))],
        out_shape=[jax.ShapeDtypeStruct((m, LANES), F32),
                   jax.ShapeDtypeStruct((m, d), F32)],
        compiler_params=_params("parallel"),
        name="moe_route",
    )(x, gain.reshape(1, d), wr)


def _row_copy(src_hbm, row, dst_ref, r, sem):
    return pltpu.make_async_copy(src_hbm.at[pl.ds(row, 1), :], dst_ref.at[pl.ds(r, 1), :], sem)


GATHER_UNROLL = 8
GATHER_QUEUE = 1
GATHER_SLICES = 12


def _start_row_gather(idx_ref, base, src_hbm, dst_ref, sem, priority, first=0, count=None):
    count = dst_ref.shape[0] if count is None else count

    def start(k, carry):
        r = first + k
        _row_copy(src_hbm, idx_ref[base + r], dst_ref, r, sem).start(priority=priority)
        return carry
    lax.fori_loop(0, count, start, 0, unroll=GATHER_UNROLL)


def _wait_row_gather(idx_ref, base, src_hbm, dst_ref, sem):
    def wait(r, carry):
        _row_copy(src_hbm, idx_ref[base + r], dst_ref, r, sem).wait()
        return carry
    lax.fori_loop(0, dst_ref.shape[0], wait, 0, unroll=GATHER_UNROLL)


def _experts_body(tok_ref, bexp_ref, rows_ref, nused_ref, hn_hbm, wg_ref, wu_ref, wd_ref,
                  ys_ref, xg_ref, xn_ref, wg16_ref, wu16_ref, wd16_ref, sem):
    i = pl.program_id(0)
    j = pl.program_id(1)
    n_used = nused_ref[0]
    used = i < n_used
    acc_ref = ys_ref

    @pl.when(j == 0)
    def _():
        acc_ref[...] = jnp.zeros_like(acc_ref)

    @pl.when(jnp.logical_and(used, j == 0))
    def _():
        @pl.when(i == 0)
        def _():
            _start_row_gather(tok_ref, 0, hn_hbm, xg_ref, sem, GATHER_QUEUE)
        _wait_row_gather(tok_ref, i * MOE_TM, hn_hbm, xg_ref, sem)
        xn_ref[...] = xg_ref[...].astype(BF16)

    per_step = MOE_TM // GATHER_SLICES
    assert GATHER_SLICES < D_FF // MOE_TF and per_step * GATHER_SLICES == MOE_TM

    @pl.when(jnp.logical_and(i + 1 < n_used, jnp.logical_and(j >= 1, j <= GATHER_SLICES)))
    def _():
        _start_row_gather(tok_ref, (i + 1) * MOE_TM, hn_hbm, xg_ref, sem, GATHER_QUEUE,
                          first=(j - 1) * per_step, count=per_step)

    n_rows = rows_ref[i]
    variants = sorted({128, 256, MOE_TM - 256, MOE_TM - 128, MOE_TM})
    lower = 0
    for upper in variants:
        fits = jnp.logical_and(n_rows > lower, n_rows <= upper)

        @pl.when(jnp.logical_and(used, fits))
        def _(upper=upper):
            _swiglu_accumulate(xn_ref, _cast_once(wg_ref, wg16_ref), _cast_once(wu_ref, wu16_ref),
                               _cast_once(wd_ref, wd16_ref), acc_ref, n_rows=upper)
        lower = upper


def _moe_experts(tok_pad, blk_expert, blk_rows, n_used, hn, w_gu, w_down):
    d = D_MODEL
    n_blocks = tok_pad.shape[0] // MOE_TM
    nj = D_FF // MOE_TF

    def col(i, j, nused):
        return jnp.where(i < nused[0], j, nj - 1)

    return pl.pallas_call(
        _experts_body,
        grid_spec=pltpu.PrefetchScalarGridSpec(
            num_scalar_prefetch=4,
            grid=(n_blocks, nj),
            in_specs=[
                pl.BlockSpec(memory_space=pl.ANY),
                pl.BlockSpec((1, d, MOE_TF), lambda i, j, tok, be, br, nu: (be[i], 0, col(i, j, nu))),
                pl.BlockSpec((1, d, MOE_TF),
                             lambda i, j, tok, be, br, nu: (be[i], 0, nj + col(i, j, nu))),
                pl.BlockSpec((1, MOE_TF, d), lambda i, j, tok, be, br, nu: (be[i], col(i, j, nu), 0)),
            ],
            out_specs=pl.BlockSpec((MOE_TM, d), lambda i, j, tok, be, br, nu: (i, 0),
                                   pipeline_mode=pl.Buffered(1)),
            scratch_shapes=[pltpu.VMEM((MOE_TM, d), F32), pltpu.VMEM((MOE_TM, d), BF16),
                            pltpu.VMEM((d, MOE_TF), BF16), pltpu.VMEM((d, MOE_TF), BF16),
                            pltpu.VMEM((MOE_TF, d), BF16), pltpu.SemaphoreType.DMA],
        ),
        out_shape=jax.ShapeDtypeStruct((n_blocks * MOE_TM, d), F32),
        compiler_params=_params("arbitrary", "arbitrary"),
        name="moe_experts",
    )(tok_pad, blk_expert, blk_rows, n_used, hn, w_gu, w_gu, w_down)


def _combine_body(p0_ref, p1_ref, ys_hbm, route_ref, x_ref, g_ref, o_ref, a_ref, b_ref, sem):
    tm = x_ref.shape[0]
    base = pl.program_id(0) * tm
    _start_row_gather(p0_ref, base, ys_hbm, a_ref, sem.at[0], 0)
    _start_row_gather(p1_ref, base, ys_hbm, b_ref, sem.at[1], 1)
    _wait_row_gather(p0_ref, base, ys_hbm, a_ref, sem.at[0])
    _wait_row_gather(p1_ref, base, ys_hbm, b_ref, sem.at[1])
    gate0 = route_ref[:, TOP_K:TOP_K + 1]
    gate1 = route_ref[:, TOP_K + 1:TOP_K + 2]
    a_ref[...] = a_ref[...] * gate0 + b_ref[...] * gate1
    _residual_norm_rows(x_ref, a_ref, g_ref, o_ref)


def _moe_combine(pos0, pos1, ys, route, x, gain, *, tm=256):
    m, d = x.shape
    return pl.pallas_call(
        _combine_body,
        grid_spec=pltpu.PrefetchScalarGridSpec(
            num_scalar_prefetch=2,
            grid=(m // tm,),
            in_specs=[
                pl.BlockSpec(memory_space=pl.ANY),
                pl.BlockSpec((tm, LANES), lambda i, p0, p1: (i, 0)),
                pl.BlockSpec((tm, d), lambda i, p0, p1: (i, 0)),
                pl.BlockSpec((1, d), lambda i, p0, p1: (0, 0)),
            ],
            out_specs=pl.BlockSpec((tm, d), lambda i, p0, p1: (i, 0)),
            scratch_shapes=[pltpu.VMEM((tm, d), F32), pltpu.VMEM((tm, d), F32),
                            pltpu.SemaphoreType.DMA((2,))],
        ),
        out_shape=jax.ShapeDtypeStruct((m, d), F32),
        compiler_params=_params("arbitrary"),
        name="moe_combine",
    )(pos0, pos1, ys, route, x, gain.reshape(1, d))


def _dispatch_plan(route):
    n = route.shape[0]
    experts = route[:, :TOP_K].astype(jnp.int32).reshape(-1)
    onehot = (experts[:, None] == jnp.arange(N_EXPERTS)[None, :]).astype(jnp.int32)
    rank = jnp.take_along_axis(jnp.cumsum(onehot, axis=0) - onehot, experts[:, None], axis=1)[:, 0]
    counts = jnp.sum(onehot, axis=0)
    blocks_per_expert = (counts + MOE_TM - 1) // MOE_TM
    block_end = jnp.cumsum(blocks_per_expert)
    block_start = block_end - blocks_per_expert
    dest = block_start[experts] * MOE_TM + rank
    n_blocks = (n * TOP_K) // MOE_TM + N_EXPERTS
    tok_pad = jnp.zeros((n_blocks * MOE_TM,), jnp.int32).at[dest].set(jnp.arange(n * TOP_K) // TOP_K)
    blocks = jnp.arange(n_blocks)
    blk_expert = jnp.clip(jnp.searchsorted(block_end, blocks, side='right'),
                          0, N_EXPERTS - 1).astype(jnp.int32)
    blk_rows = jnp.clip(counts[blk_expert] - (blocks - block_start[blk_expert]) * MOE_TM,
                        0, MOE_TM).astype(jnp.int32)
    n_used = block_end[-1].astype(jnp.int32)
    blk_expert = jnp.where(blocks < n_used, blk_expert, blk_expert[jnp.maximum(n_used - 1, 0)])
    pos = dest.reshape(n, TOP_K)
    return tok_pad, blk_expert, blk_rows, n_used.reshape(1), pos[:, 0], pos[:, 1]


def kernel(x, norm_gains, a_w_in, a_b_in, a_norm_v, a_w_s, a_b_s, a_w_out, kv_norm, w_kv,
           b_w_q, b_w_o, ffn_w_gu, ffn_w_down, moe_router, moe_w_gu, moe_w_down):
    batch, seq, d = x.shape
    h = x.reshape(batch * seq, d)
    bf = lambda w: w.astype(BF16)

    g = norm_gains[0]
    (z, ssq), (w_out, w_gu, w_down) = _gmlp_in(
        h, g[0], bf(a_w_in[0]), a_b_in[0], [a_w_out[0], ffn_w_gu[0], ffn_w_down[0]])
    h, (w_kv16, w_q) = _gmlp_out(z, ssq, a_norm_v[0], a_w_s[0], a_b_s[0], h, g[1], w_out,
                                 [_whole_rider(w, 64) for w in (w_kv, b_w_q[0])])
    h, (w_o,) = _ffn_dense(h, g[2], w_gu, w_down, g[3], [_whole_rider(b_w_o[0], 56)])

    g = norm_gains[1]
    kv, _ = _norm_proj(h, kv_norm, w_kv16, [], tm=2048, tn=1024, name="proj_kv")
    q, _ = _norm_proj(h, g[0], w_q, [], tm=2048, tn=1024, name="proj_q")
    base = jnp.exp2(-8.0 * jnp.arange(1, N_KV_HEADS + 1, dtype=F32) / N_KV_HEADS)
    o, _ = _attention(base, q, kv, batch, seq, [])
    h = _attn_out(o, h, g[1], w_o)

    route, hn = _moe_route(h, g[2], moe_router[0])
    tok_pad, blk_expert, blk_rows, n_used, pos0, pos1 = _dispatch_plan(route)
    ys = _moe_experts(tok_pad, blk_expert, blk_rows, n_used, hn, moe_w_gu[0], moe_w_down[0])
    h = _moe_combine(pos0, pos1, ys, route, h, g[3])
    return h.reshape(batch, seq, d)
```

```python
import functools
from typing import NamedTuple

import jax
import jax.numpy as jnp
from jax import lax
from jax.experimental import pallas as pl
from jax.experimental.pallas import tpu as pltpu

F32 = jnp.float32
BF16 = jnp.bfloat16

D_MODEL = 2048
EPS = 1e-6
CHUNK = 128
GMLP_HALF = 2 * D_MODEL
GMLP_GROUPS = 8
GMLP_GROUP_CH = GMLP_HALF // GMLP_GROUPS
HEAD_DIM = 128
N_KV_HEADS = D_MODEL // HEAD_DIM
DILATED_GROUPS = ((128, 1), (512, 4), (2048, 16))
BLOCK = 128
NEG = -1e30
D_FF = 7168
N_EXPERTS = 8
TOP_K = 2

LANES = 128
VMEM_LIMIT_BYTES = 56 * 1024 * 1024

NORM_ROWS = 128
EPILOGUE_ROWS = 128
MOE_TM = 1152
MOE_TF = 512


def _params(*sem):
    return pltpu.CompilerParams(dimension_semantics=sem, vmem_limit_bytes=VMEM_LIMIT_BYTES)


def _rms_scale(x):
    return lax.rsqrt(jnp.mean(x * x, axis=-1, keepdims=True) + EPS)


def _norm_rows_into(x_ref, g_ref, out_ref):
    def body(c, carry):
        rows = pl.ds(pl.multiple_of(c * NORM_ROWS, NORM_ROWS), NORM_ROWS)
        x = x_ref[rows, :]
        out_ref[rows, :] = ((x * _rms_scale(x)) * g_ref[...]).astype(out_ref.dtype)
        return carry
    lax.fori_loop(0, x_ref.shape[0] // NORM_ROWS, body, 0)


def _residual_norm_rows(res_ref, acc_ref, g_ref, out_ref):
    def body(c, carry):
        rows = pl.ds(pl.multiple_of(c * NORM_ROWS, NORM_ROWS), NORM_ROWS)
        a = acc_ref[rows, :]
        out_ref[rows, :] = res_ref[rows, :] + (a * _rms_scale(a)) * g_ref[...]
        return carry
    lax.fori_loop(0, acc_ref.shape[0] // NORM_ROWS, body, 0)


BF16_SUBLANES = 16


class _Rider(NamedTuple):
    src: jax.Array
    dst: jax.Array | None
    first_row: int
    steps: int
    rows: int


def _whole_rider(w, host_steps):
    total = w.shape[0]
    for steps in range(host_steps, 0, -1):
        if total % steps == 0 and (total // steps) % BF16_SUBLANES == 0:
            return _Rider(w, None, 0, steps, total // steps)
    raise ValueError(f"no chunking of {w.shape} over {host_steps} steps")


def _part_rider(w, dst, first_row, steps, rows):
    assert rows % BF16_SUBLANES == 0 and first_row % rows == 0
    assert first_row + steps * rows <= w.shape[0]
    return _Rider(w, dst, first_row, steps, rows)


def _riding_body(body, n_in, n_out, n_riders, n_alias, *refs):
    ins, refs = refs[:n_in], refs[n_in:]
    r_in, refs = refs[:n_riders], refs[n_riders + n_alias:]
    outs, refs = refs[:n_out], refs[n_out:]
    r_out, scratch = refs[:n_riders], refs[n_riders:]

    def cast_part(part, n_parts):
        for src, dst in zip(r_in, r_out):
            lane_tiles = src.shape[1] // LANES
            assert lane_tiles * LANES == src.shape[1] and lane_tiles >= n_parts
            cols = slice(part * lane_tiles // n_parts * LANES,
                         (part + 1) * lane_tiles // n_parts * LANES)
            dst[:, cols] = src[:, cols].astype(BF16)
    body(cast_part, *ins, *outs, *scratch)


def _hosted_call(body, *, grid, in_specs, out_specs, out_shape, scratch_shapes, args, riders,
                 semantics, name, num_scalar_prefetch=0):
    n_grid = len(grid)
    strides = [1] * n_grid
    for ax in range(n_grid - 2, -1, -1):
        strides[ax] = strides[ax + 1] * grid[ax + 1]
    assert all(r.steps <= strides[0] * grid[0] for r in riders)

    r_in, r_alias, r_out, r_shapes = [], [], [], []
    for r in riders:
        def index(*g, r=r):
            step = sum(g[ax] * strides[ax] for ax in range(n_grid))
            return r.first_row // r.rows + jnp.minimum(step, r.steps - 1), 0
        block = (r.rows, r.src.shape[1])
        r_in.append(pl.BlockSpec(block, index))
        r_out.append(pl.BlockSpec(block, index))
        r_shapes.append(jax.ShapeDtypeStruct(r.src.shape, BF16))
        if r.dst is not None:
            r_alias.append(r.dst)

    n_in = num_scalar_prefetch + len(in_specs)
    n_out = len(out_specs)
    aliases, k = {}, 0
    for ridx, r in enumerate(riders):
        if r.dst is not None:
            aliases[n_in + len(riders) + k] = n_out + ridx
            k += 1
    all_in = list(in_specs) + r_in + [pl.BlockSpec(memory_space=pl.ANY)] * len(r_alias)
    all_out = list(out_specs) + r_out
    kernel_fn = functools.partial(_riding_body, body, n_in, n_out, len(riders), len(r_alias))
    if num_scalar_prefetch:
        spec = dict(grid_spec=pltpu.PrefetchScalarGridSpec(
            num_scalar_prefetch=num_scalar_prefetch, grid=grid, in_specs=all_in,
            out_specs=all_out, scratch_shapes=scratch_shapes))
    else:
        spec = dict(grid=grid, in_specs=all_in, out_specs=all_out, scratch_shapes=scratch_shapes)
    outs = pl.pallas_call(
        kernel_fn,
        out_shape=list(out_shape) + r_shapes,
        input_output_aliases=aliases,
        compiler_params=_params(*semantics),
        name=name,
        **spec,
    )(*args, *[r.src for r in riders], *r_alias)
    return outs[:n_out], outs[n_out:]


def _gelu_tanh(x):
    cdf = 0.5 * (1.0 + jnp.tanh(0.7978845608028654 * (x + 0.044715 * (x * x * x))))
    return x * cdf


def _row_chunks(ref, chunk=EPILOGUE_ROWS):
    n = ref.shape[0] // chunk
    return [(c, n, slice(c * chunk, (c + 1) * chunk)) for c in range(n)]


def _proj_body(cast_part, x_ref, g_ref, w_ref, o_ref, xn_ref):
    @pl.when(pl.program_id(1) == 0)
    def _():
        _norm_rows_into(x_ref, g_ref, xn_ref)
    for c, n, rows in _row_chunks(xn_ref, 4 * EPILOGUE_ROWS):
        o_ref[rows, :] = jnp.dot(xn_ref[rows, :], w_ref[...],
                                 preferred_element_type=F32).astype(o_ref.dtype)
        cast_part(c, n)


def _norm_proj(x, gain, w, riders, *, tm, tn, name):
    m, k = x.shape
    n = w.shape[1]
    (out,), casts = _hosted_call(
        _proj_body,
        grid=(m // tm, n // tn),
        in_specs=[
            pl.BlockSpec((tm, k), lambda i, j: (i, 0), pipeline_mode=pl.Buffered(1)),
            pl.BlockSpec((1, k), lambda i, j: (0, 0)),
            pl.BlockSpec((k, tn), lambda i, j: (0, j)),
        ],
        out_specs=[pl.BlockSpec((tm, tn), lambda i, j: (i, j))],
        out_shape=[jax.ShapeDtypeStruct((m, n), BF16)],
        scratch_shapes=[pltpu.VMEM((tm, k), BF16)],
        args=(x, gain.reshape(1, k), w),
        riders=riders,
        semantics=("arbitrary", "arbitrary"),
        name=name,
    )
    return out, casts


def _gmlp_in_body(cast_part, x_ref, g_ref, w_ref, b_ref, z_ref, ssq_ref, xn_ref, ss_ref):
    j = pl.program_id(1)
    nj = pl.num_programs(1)

    @pl.when(j == 0)
    def _():
        _norm_rows_into(x_ref, g_ref, xn_ref)
        ss_ref[...] = jnp.zeros_like(ss_ref)

    is_v = (j >= nj // 2).astype(F32)
    for c, n, rows in _row_chunks(xn_ref, 2 * EPILOGUE_ROWS):
        z = jnp.dot(xn_ref[rows, :], w_ref[...], preferred_element_type=F32) + b_ref[...]
        z = _gelu_tanh(z)
        z_ref[rows, :] = z.astype(z_ref.dtype)
        ss_ref[rows, :] += is_v * jnp.sum(z * z, axis=-1, keepdims=True)
        cast_part(c, n)

    @pl.when(j == nj - 1)
    def _():
        ssq_ref[...] = jnp.broadcast_to(ss_ref[...], ssq_ref.shape)


def _gmlp_in(x, gain, w_in, b_in, later_weights, *, tm=2048, tn=512):
    m, k = x.shape
    n = w_in.shape[1]
    grid = (m // tm, n // tn)
    return _hosted_call(
        _gmlp_in_body,
        grid=grid,
        in_specs=[
            pl.BlockSpec((tm, k), lambda i, j: (i, 0), pipeline_mode=pl.Buffered(1)),
            pl.BlockSpec((1, k), lambda i, j: (0, 0)),
            pl.BlockSpec((k, tn), lambda i, j: (0, j)),
            pl.BlockSpec((1, tn), lambda i, j: (0, j)),
        ],
        out_specs=[
            pl.BlockSpec((tm, tn), lambda i, j: (i, j)),
            pl.BlockSpec((tm, LANES), lambda i, j: (i, 0)),
        ],
        out_shape=[
            jax.ShapeDtypeStruct((m, n), BF16),
            jax.ShapeDtypeStruct((m, LANES), F32),
        ],
        scratch_shapes=[pltpu.VMEM((tm, k), BF16), pltpu.VMEM((tm, 1), F32)],
        args=(x, gain.reshape(1, k), w_in, b_in.reshape(1, n)),
        riders=[_whole_rider(w, grid[0] * grid[1]) for w in later_weights],
        semantics=("arbitrary", "arbitrary"),
        name="gmlp_in",
    )


def _gmlp_out_body(cast_part, u_ref, v_ref, ssq_ref, gv_ref, ws_ref, bs_ref, x_ref, g_ref, wo_ref,
                   o_ref):
    grp = pl.program_id(1)
    tm = u_ref.shape[0]
    acc_ref = o_ref

    @pl.when(grp == 0)
    def _():
        acc_ref[...] = jnp.zeros_like(acc_ref)

    row = lax.broadcasted_iota(jnp.int32, (CHUNK, CHUNK), 0)
    col = lax.broadcasted_iota(jnp.int32, (CHUNK, CHUNK), 1)
    ws = jnp.where(row >= col, ws_ref[0], 0.0).astype(BF16)
    r = lax.rsqrt(ssq_ref[:, 0:1] * (1.0 / GMLP_HALF) + EPS)
    def project(rows, gated):
        acc_ref[rows, :] += jnp.dot(gated, wo_ref[...], preferred_element_type=F32)

    pending = None
    n = tm // CHUNK
    for c in range(n):
        rows = slice(c * CHUNK, (c + 1) * CHUNK)
        vn = (v_ref[rows, :].astype(F32) * r[rows, :]) * gv_ref[...]
        mixed = jnp.dot(ws, vn.astype(BF16), preferred_element_type=F32) + bs_ref[0]
        gated = (u_ref[rows, :].astype(F32) * mixed).astype(BF16)
        if pending is not None:
            project(*pending)
        pending = (rows, gated)
        cast_part(c, n)
    project(*pending)

    @pl.when(grp == pl.num_programs(1) - 1)
    def _():
        _residual_norm_rows(x_ref, acc_ref, g_ref, o_ref)


def _gmlp_out(z, ssq, norm_v, w_s, b_s, x, gain, w_out, riders, *, tm=1024):
    m, d = x.shape
    gc = GMLP_GROUP_CH
    (out,), casts = _hosted_call(
        _gmlp_out_body,
        grid=(m // tm, GMLP_GROUPS),
        in_specs=[
            pl.BlockSpec((tm, gc), lambda i, g: (i, g)),
            pl.BlockSpec((tm, gc), lambda i, g: (i, GMLP_GROUPS + g)),
            pl.BlockSpec((tm, LANES), lambda i, g: (i, 0)),
            pl.BlockSpec((1, gc), lambda i, g: (0, g)),
            pl.BlockSpec((1, CHUNK, CHUNK), lambda i, g: (g, 0, 0)),
            pl.BlockSpec((1, CHUNK, 1), lambda i, g: (g, 0, 0)),
            pl.BlockSpec((tm, d), lambda i, g: (i, 0)),
            pl.BlockSpec((1, d), lambda i, g: (0, 0)),
            pl.BlockSpec((gc, d), lambda i, g: (g, 0)),
        ],
        out_specs=[pl.BlockSpec((tm, d), lambda i, g: (i, 0))],
        out_shape=[jax.ShapeDtypeStruct((m, d), F32)],
        scratch_shapes=[],
        args=(z, z, ssq, norm_v.reshape(1, GMLP_HALF), w_s, b_s[:, :, None], x,
              gain.reshape(1, d), w_out),
        riders=riders,
        semantics=("arbitrary", "arbitrary"),
        name="gmlp_out",
    )
    return out, casts


SWIGLU_ROWS = 256


def _chunks_covering(n_rows):
    assert n_rows % EPILOGUE_ROWS == 0
    bounds = list(range(0, n_rows, SWIGLU_ROWS)) + [n_rows]
    return [slice(lo, hi) for lo, hi in zip(bounds[:-1], bounds[1:])]


def _swiglu_accumulate(xn_ref, wg, wu, wd, acc_ref, cast_part=None, n_rows=None):
    chunks = _chunks_covering(n_rows or xn_ref.shape[0])

    def down(rows, a):
        acc_ref[rows, :] += jnp.dot(a, wd(), preferred_element_type=F32)

    pending = None
    for c, rows in enumerate(chunks):
        x = xn_ref[rows, :]
        g = jnp.dot(x, wg(), preferred_element_type=F32)
        u = jnp.dot(x, wu(), preferred_element_type=F32)
        a = ((g * jax.nn.sigmoid(g)) * u).astype(BF16)
        if pending is not None:
            down(*pending)
        pending = (rows, a)
        if cast_part is not None:
            cast_part(c, len(chunks))
    down(*pending)


def _cast_once(src_ref, dst_ref):
    done = []

    def get():
        if not done:
            dst_ref[...] = src_ref[0].astype(BF16)
            done.append(True)
        return dst_ref[...]
    return get


def _ffn_body(cast_part, x_ref, g_in_ref, wg_ref, wu_ref, wd_ref, g_out_ref, o_ref, xn_ref):
    j = pl.program_id(1)
    acc_ref = o_ref

    @pl.when(j == 0)
    def _():
        _norm_rows_into(x_ref, g_in_ref, xn_ref)
        acc_ref[...] = jnp.zeros_like(acc_ref)

    _swiglu_accumulate(xn_ref, lambda: wg_ref[...], lambda: wu_ref[...], lambda: wd_ref[...],
                       acc_ref, cast_part)

    @pl.when(j == pl.num_programs(1) - 1)
    def _():
        _residual_norm_rows(x_ref, acc_ref, g_out_ref, o_ref)


def _ffn_dense(x, g_in, w_gu, w_down, g_out, riders, *, tm=1024, tf=1024):
    m, d = x.shape
    nj = D_FF // tf
    once = dict(pipeline_mode=pl.Buffered(1))
    (out,), casts = _hosted_call(
        _ffn_body,
        grid=(m // tm, nj),
        in_specs=[
            pl.BlockSpec((tm, d), lambda i, j: (i, 0), **once),
            pl.BlockSpec((1, d), lambda i, j: (0, 0)),
            pl.BlockSpec((d, tf), lambda i, j: (0, j)),
            pl.BlockSpec((d, tf), lambda i, j: (0, nj + j)),
            pl.BlockSpec((tf, d), lambda i, j: (j, 0)),
            pl.BlockSpec((1, d), lambda i, j: (0, 0)),
        ],
        out_specs=[pl.BlockSpec((tm, d), lambda i, j: (i, 0), **once)],
        out_shape=[jax.ShapeDtypeStruct((m, d), F32)],
        scratch_shapes=[pltpu.VMEM((tm, d), BF16)],
        args=(x, g_in.reshape(1, d), w_gu, w_gu, w_down, g_out.reshape(1, d)),
        riders=riders,
        semantics=("arbitrary", "arbitrary"),
        name="ffn_dense",
    )
    return out, casts


SUB = 4
ATTN_UNROLL = 8


def _deinterleave(src_ref, dst_ref, span):
    part = span // SUB
    for base in range(0, src_ref.shape[0], span):
        for r in range(SUB):
            dst_ref[base + r * part: base + (r + 1) * part, :] = (
                src_ref[pl.ds(base + r, part, stride=SUB), :].astype(dst_ref.dtype))


def _interleave(src_ref, dst_ref, span):
    part = span // SUB
    for base in range(0, src_ref.shape[0], span):
        for r in range(SUB):
            dst_ref[pl.ds(base + r, part, stride=SUB), :] = (
                src_ref[base + r * part: base + (r + 1) * part, :])


def _window_attention(q_ref, k_ref, v_ref, slope, dilation, blocks_per_seq,
                      bias_c_ref, bias_p_ref, pc_ref, pp_ref, den_ref, o_ref, lse_ref):
    jq = lax.broadcasted_iota(jnp.int32, (BLOCK, BLOCK), 0)
    kk = lax.broadcasted_iota(jnp.int32, (BLOCK, BLOCK), 1)
    dist_c = ((jq - kk) * dilation).astype(F32)
    dist_p = ((BLOCK + jq - kk) * dilation).astype(F32)
    bias_c_ref[...] = jnp.where(kk <= jq, -(slope * dist_c), NEG)
    bias_p_ref[...] = jnp.where(kk >= jq, -(slope * dist_p), NEG)
    scale = HEAD_DIM ** -0.5
    contract_last = (((1,), (1,)), ((), ()))
    with_prev = blocks_per_seq > 1

    def block_rows(n):
        rows = pl.ds(pl.multiple_of(n * BLOCK, BLOCK), BLOCK)
        prev = pl.ds(pl.multiple_of(jnp.maximum(n - 1, 0) * BLOCK, BLOCK), BLOCK)
        return rows, prev

    def probabilities(n, carry):
        rows, prev = block_rows(n)
        q = q_ref[rows, :]
        s_c = lax.dot_general(q, k_ref[rows, :], contract_last, preferred_element_type=F32)
        s_c = s_c * scale + bias_c_ref[...]
        if with_prev:
            has_prev = (n % blocks_per_seq) != 0
            s_p = lax.dot_general(q, k_ref[prev, :], contract_last, preferred_element_type=F32)
            s_p = jnp.where(has_prev, s_p * scale + bias_p_ref[...], NEG)
            m = jnp.max(jnp.maximum(s_c, s_p), axis=-1, keepdims=True)
            p_c = jnp.exp(s_c - m)
            p_p = jnp.exp(s_p - m)
            den = jnp.sum(p_c + p_p, axis=-1, keepdims=True)
            pp_ref[rows, :] = p_p.astype(BF16)
        else:
            m = jnp.max(s_c, axis=-1, keepdims=True)
            p_c = jnp.exp(s_c - m)
            den = jnp.sum(p_c, axis=-1, keepdims=True)
        pc_ref[rows, :] = p_c.astype(BF16)
        den_ref[rows, :] = den
        lse_ref[rows, :] = jnp.broadcast_to(m + jnp.log(den), (BLOCK, LANES))
        return carry

    def values(n, carry):
        rows, prev = block_rows(n)
        o = jnp.dot(pc_ref[rows, :], v_ref[rows, :], preferred_element_type=F32)
        if with_prev:
            o = o + jnp.dot(pp_ref[rows, :], v_ref[prev, :], preferred_element_type=F32)
        o_ref[rows, :] = o / den_ref[rows, :]
        return carry

    n_blocks = q_ref.shape[0] // BLOCK
    lax.fori_loop(0, n_blocks, probabilities, 0, unroll=ATTN_UNROLL)
    lax.fori_loop(0, n_blocks, values, 0, unroll=ATTN_UNROLL)


def _attn_body(cast_part, base_ref, q1_ref, q4_ref, q16_ref, k_ref, v_ref, out_ref,
               stage_ref, tmp_ref, k4f_ref, v4f_ref,
               qp_ref, k4_ref, v4_ref, k16_ref, v16_ref,
               o1_ref, l1_ref, o4_ref, l4_ref, o16_ref, l16_ref, op_ref, lp_ref,
               bias_c_ref, bias_p_ref, pc_ref, pp_ref, den_ref):
    seq = k_ref.shape[0]
    base = base_ref[pl.program_id(1)]
    (_, d1), (_, d4), (_, d16) = DILATED_GROUPS
    cast_part(0, 1)

    stage_ref[...] = k_ref[...].astype(F32)
    _deinterleave(stage_ref, k4f_ref, seq)
    k4_ref[...] = k4f_ref[...].astype(BF16)
    _deinterleave(k4f_ref, k16_ref, seq // SUB)
    stage_ref[...] = v_ref[...].astype(F32)
    _deinterleave(stage_ref, v4f_ref, seq)
    v4_ref[...] = v4f_ref[...].astype(BF16)
    _deinterleave(v4f_ref, v16_ref, seq // SUB)

    _window_attention(q1_ref, k_ref, v_ref, base / d1, d1, seq // BLOCK,
                      bias_c_ref, bias_p_ref, pc_ref, pp_ref, den_ref, o1_ref, l1_ref)

    stage_ref[...] = q4_ref[...].astype(F32)
    _deinterleave(stage_ref, qp_ref, seq)
    _window_attention(qp_ref, k4_ref, v4_ref, base / d4, d4, seq // d4 // BLOCK,
                      bias_c_ref, bias_p_ref, pc_ref, pp_ref, den_ref, op_ref, lp_ref)
    _interleave(op_ref, o4_ref, seq)
    _interleave(lp_ref, l4_ref, seq)

    stage_ref[...] = q16_ref[...].astype(F32)
    _deinterleave(stage_ref, tmp_ref, seq)
    _deinterleave(tmp_ref, qp_ref, seq // SUB)
    _window_attention(qp_ref, k16_ref, v16_ref, base / d16, d16, seq // d16 // BLOCK,
                      bias_c_ref, bias_p_ref, pc_ref, pp_ref, den_ref, op_ref, lp_ref)
    _interleave(op_ref, tmp_ref, seq // SUB)
    _interleave(tmp_ref, o16_ref, seq)
    _interleave(lp_ref, tmp_ref, seq // SUB)
    _interleave(tmp_ref, l16_ref, seq)

    chunk = 2 * BLOCK

    def merge(c, carry):
        rows = pl.ds(pl.multiple_of(c * chunk, chunk), chunk)
        la, lb, lc = l1_ref[rows, :], l4_ref[rows, :], l16_ref[rows, :]
        mx = jnp.maximum(jnp.maximum(la, lb), lc)
        ea, eb, ec = jnp.exp(la - mx), jnp.exp(lb - mx), jnp.exp(lc - mx)
        den = ea + eb + ec
        o = (ea / den) * o1_ref[rows, :] + (eb / den) * o4_ref[rows, :] + (ec / den) * o16_ref[rows, :]
        out_ref[rows, :] = o.astype(out_ref.dtype)
        return carry
    lax.fori_loop(0, seq // chunk, merge, 0)


def _attention(base, q, kv, batch, seq, riders):
    for window, dilation in DILATED_GROUPS:
        assert window // dilation == BLOCK, "keys per query must span exactly one previous block"
    assert [d for _, d in DILATED_GROUPS] == [1, SUB, SUB * SUB]
    assert seq % (SUB * SUB * BLOCK) == 0
    h = N_KV_HEADS
    blk = (seq, HEAD_DIM)
    f32buf = pltpu.VMEM((seq, HEAD_DIM), F32)
    bf16buf = pltpu.VMEM((seq, HEAD_DIM), BF16)
    (out,), casts = _hosted_call(
        _attn_body,
        num_scalar_prefetch=1,
        grid=(batch, h),
        in_specs=[
            pl.BlockSpec(blk, lambda b, hh, base: (b, hh)),
            pl.BlockSpec(blk, lambda b, hh, base: (b, h + hh)),
            pl.BlockSpec(blk, lambda b, hh, base: (b, 2 * h + hh)),
            pl.BlockSpec(blk, lambda b, hh, base: (b, hh)),
            pl.BlockSpec(blk, lambda b, hh, base: (b, h + hh)),
        ],
        out_specs=[pl.BlockSpec(blk, lambda b, hh, base: (b, hh))],
        out_shape=[jax.ShapeDtypeStruct((batch * seq, D_MODEL), BF16)],
        scratch_shapes=[f32buf] * 4 + [bf16buf] * 5 + [f32buf] * 8
                       + [pltpu.VMEM((BLOCK, BLOCK), F32)] * 2
                       + [bf16buf] * 2 + [pltpu.VMEM((seq, 1), F32)],
        args=(base, q, q, q, kv, kv),
        riders=riders,
        semantics=("arbitrary", "arbitrary"),
        name="attn",
    )
    return out, casts


def _attn_out_body(o_ref, x_ref, g_ref, wo_ref, out_ref, acc_ref):
    acc_ref[...] = jnp.dot(o_ref[...], wo_ref[...], preferred_element_type=F32)
    _residual_norm_rows(x_ref, acc_ref, g_ref, out_ref)


def _attn_out(o, x, gain, w_o, *, tm=512):
    m, d = x.shape
    row_blk = pl.BlockSpec((tm, d), lambda i: (i, 0))
    return pl.pallas_call(
        _attn_out_body,
        grid=(m // tm,),
        in_specs=[row_blk, row_blk,
                  pl.BlockSpec((1, d), lambda i: (0, 0)),
                  pl.BlockSpec((d, d), lambda i: (0, 0))],
        out_specs=row_blk,
        out_shape=jax.ShapeDtypeStruct((m, d), F32),
        scratch_shapes=[pltpu.VMEM((tm, d), F32)],
        compiler_params=_params("parallel"),
        name="attn_out",
    )(o, x, gain.reshape(1, d), w_o)


def _route_body(x_ref, g_ref, wr_ref, o_ref, xn_ref):
    _norm_rows_into(x_ref, g_ref, xn_ref)
    logits = jnp.dot(xn_ref[...], wr_ref[...], preferred_element_type=F32,
                     precision=lax.Precision.HIGHEST)
    lane = lax.broadcasted_iota(jnp.int32, logits.shape, 1)
    logits = jnp.where(lane < N_EXPERTS, logits, -jnp.inf)
    m1 = jnp.max(logits, axis=-1, keepdims=True)
    i1 = jnp.min(jnp.where(logits == m1, lane, LANES), axis=-1, keepdims=True)
    rest = jnp.where(lane == i1, -jnp.inf, logits)
    m2 = jnp.max(rest, axis=-1, keepdims=True)
    i2 = jnp.min(jnp.where(rest == m2, lane, LANES), axis=-1, keepdims=True)
    t = jnp.exp(m2 - m1)
    den = 1.0 + t
    out = jnp.where(lane == 0, i1.astype(F32), 0.0)
    out = jnp.where(lane == 1, i2.astype(F32), out)
    out = jnp.where(lane == 2, 1.0 / den, out)
    out = jnp.where(lane == 3, t / den, out)
    o_ref[...] = out


def _moe_route(x, gain, w_router, *, tm=512):
    m, d = x.shape
    wr = jnp.zeros((d, LANES), F32).at[:, :N_EXPERTS].set(w_router)
    return pl.pallas_call(
        _route_body,
        grid=(m // tm,),
        in_specs=[pl.BlockSpec((tm, d), lambda i: (i, 0)),
                  pl.BlockSpec((1, d), lambda i: (0, 0)),
                  pl.BlockSpec((d, LANES), lambda i: (0, 0))],
        out_specs=[pl.BlockSpec((tm, LANES), lambda i: (i, 0)),
                   pl.BlockSpec((tm, d), lambda i: (i, 0))],
        out_shape=[jax.ShapeDtypeStruct((m, LANES), F32),
                   jax.ShapeDtypeStruct((m, d), F32)],
        compiler_params=_params("parallel"),
        name="moe_route",
    )(x, gain.reshape(1, d), wr)


def _row_copy(src_hbm, row, dst_ref, r, sem):
    return pltpu.make_async_copy(src_hbm.at[pl.ds(row, 1), :], dst_ref.at[pl.ds(r, 1), :], sem)


GATHER_UNROLL = 8
GATHER_QUEUE = 1
GATHER_SLICES = 12


def _start_row_gather(idx_ref, base, src_hbm, dst_ref, sem, priority, first=0, count=None):
    count = dst_ref.shape[0] if count is None else count

    def start(k, carry):
        r = first + k
        _row_copy(src_hbm, idx_ref[base + r], dst_ref, r, sem).start(priority=priority)
        return carry
    lax.fori_loop(0, count, start, 0, unroll=GATHER_UNROLL)


def _wait_row_gather(idx_ref, base, src_hbm, dst_ref, sem):
    def wait(r, carry):
        _row_copy(src_hbm, idx_ref[base + r], dst_ref, r, sem).wait()
        return carry
    lax.fori_loop(0, dst_ref.shape[0], wait, 0, unroll=GATHER_UNROLL)


def _experts_body(tok_ref, bexp_ref, rows_ref, nused_ref, hn_hbm, wg_ref, wu_ref, wd_ref,
                  ys_ref, xg_ref, xn_ref, wg16_ref, wu16_ref, wd16_ref, sem):
    i = pl.program_id(0)
    j = pl.program_id(1)
    n_used = nused_ref[0]
    used = i < n_used
    acc_ref = ys_ref

    @pl.when(j == 0)
    def _():
        acc_ref[...] = jnp.zeros_like(acc_ref)

    @pl.when(jnp.logical_and(used, j == 0))
    def _():
        @pl.when(i == 0)
        def _():
            _start_row_gather(tok_ref, 0, hn_hbm, xg_ref, sem, GATHER_QUEUE)
        _wait_row_gather(tok_ref, i * MOE_TM, hn_hbm, xg_ref, sem)
        xn_ref[...] = xg_ref[...].astype(BF16)

    per_step = MOE_TM // GATHER_SLICES
    assert GATHER_SLICES < D_FF // MOE_TF and per_step * GATHER_SLICES == MOE_TM

    @pl.when(jnp.logical_and(i + 1 < n_used, jnp.logical_and(j >= 1, j <= GATHER_SLICES)))
    def _():
        _start_row_gather(tok_ref, (i + 1) * MOE_TM, hn_hbm, xg_ref, sem, GATHER_QUEUE,
                          first=(j - 1) * per_step, count=per_step)

    n_rows = rows_ref[i]
    variants = [r for r in (128, 256, 512, 1024, 2048) if r < MOE_TM] + [MOE_TM]
    lower = 0
    for upper in variants:
        fits = jnp.logical_and(n_rows > lower, n_rows <= upper)

        @pl.when(jnp.logical_and(used, fits))
        def _(upper=upper):
            _swiglu_accumulate(xn_ref, _cast_once(wg_ref, wg16_ref), _cast_once(wu_ref, wu16_ref),
                               _cast_once(wd_ref, wd16_ref), acc_ref, n_rows=upper)
        lower = upper


def _moe_experts(tok_pad, blk_expert, blk_rows, n_used, hn, w_gu, w_down):
    d = D_MODEL
    n_blocks = tok_pad.shape[0] // MOE_TM
    nj = D_FF // MOE_TF

    def col(i, j, nused):
        return jnp.where(i < nused[0], j, nj - 1)

    return pl.pallas_call(
        _experts_body,
        grid_spec=pltpu.PrefetchScalarGridSpec(
            num_scalar_prefetch=4,
            grid=(n_blocks, nj),
            in_specs=[
                pl.BlockSpec(memory_space=pl.ANY),
                pl.BlockSpec((1, d, MOE_TF), lambda i, j, tok, be, br, nu: (be[i], 0, col(i, j, nu))),
                pl.BlockSpec((1, d, MOE_TF),
                             lambda i, j, tok, be, br, nu: (be[i], 0, nj + col(i, j, nu))),
                pl.BlockSpec((1, MOE_TF, d), lambda i, j, tok, be, br, nu: (be[i], col(i, j, nu), 0)),
            ],
            out_specs=pl.BlockSpec((MOE_TM, d), lambda i, j, tok, be, br, nu: (i, 0),
                                   pipeline_mode=pl.Buffered(1)),
            scratch_shapes=[pltpu.VMEM((MOE_TM, d), F32), pltpu.VMEM((MOE_TM, d), BF16),
                            pltpu.VMEM((d, MOE_TF), BF16), pltpu.VMEM((d, MOE_TF), BF16),
                            pltpu.VMEM((MOE_TF, d), BF16), pltpu.SemaphoreType.DMA],
        ),
        out_shape=jax.ShapeDtypeStruct((n_blocks * MOE_TM, d), F32),
        compiler_params=_params("arbitrary", "arbitrary"),
        name="moe_experts",
    )(tok_pad, blk_expert, blk_rows, n_used, hn, w_gu, w_gu, w_down)


def _combine_body(p0_ref, p1_ref, ys_hbm, route_ref, x_ref, g_ref, o_ref, a_ref, b_ref, sem):
    tm = x_ref.shape[0]
    base = pl.program_id(0) * tm
    _start_row_gather(p0_ref, base, ys_hbm, a_ref, sem.at[0], 0)
    _start_row_gather(p1_ref, base, ys_hbm, b_ref, sem.at[1], 1)
    _wait_row_gather(p0_ref, base, ys_hbm, a_ref, sem.at[0])
    _wait_row_gather(p1_ref, base, ys_hbm, b_ref, sem.at[1])
    gate0 = route_ref[:, TOP_K:TOP_K + 1]
    gate1 = route_ref[:, TOP_K + 1:TOP_K + 2]
    a_ref[...] = a_ref[...] * gate0 + b_ref[...] * gate1
    _residual_norm_rows(x_ref, a_ref, g_ref, o_ref)


def _moe_combine(pos0, pos1, ys, route, x, gain, *, tm=256):
    m, d = x.shape
    return pl.pallas_call(
        _combine_body,
        grid_spec=pltpu.PrefetchScalarGridSpec(
            num_scalar_prefetch=2,
            grid=(m // tm,),
            in_specs=[
                pl.BlockSpec(memory_space=pl.ANY),
                pl.BlockSpec((tm, LANES), lambda i, p0, p1: (i, 0)),
                pl.BlockSpec((tm, d), lambda i, p0, p1: (i, 0)),
                pl.BlockSpec((1, d), lambda i, p0, p1: (0, 0)),
            ],
            out_specs=pl.BlockSpec((tm, d), lambda i, p0, p1: (i, 0)),
            scratch_shapes=[pltpu.VMEM((tm, d), F32), pltpu.VMEM((tm, d), F32),
                            pltpu.SemaphoreType.DMA((2,))],
        ),
        out_shape=jax.ShapeDtypeStruct((m, d), F32),
        compiler_params=_params("arbitrary"),
        name="moe_combine",
    )(pos0, pos1, ys, route, x, gain.reshape(1, d))


def _dispatch_plan(route):
    n = route.shape[0]
    experts = route[:, :TOP_K].astype(jnp.int32).reshape(-1)
    onehot = (experts[:, None] == jnp.arange(N_EXPERTS)[None, :]).astype(jnp.int32)
    rank = jnp.take_along_axis(jnp.cumsum(onehot, axis=0) - onehot, experts[:, None], axis=1)[:, 0]
    counts = jnp.sum(onehot, axis=0)
    blocks_per_expert = (counts + MOE_TM - 1) // MOE_TM
    block_end = jnp.cumsum(blocks_per_expert)
    block_start = block_end - blocks_per_expert
    dest = block_start[experts] * MOE_TM + rank
    n_blocks = (n * TOP_K) // MOE_TM + N_EXPERTS
    tok_pad = jnp.zeros((n_blocks * MOE_TM,), jnp.int32).at[dest].set(jnp.arange(n * TOP_K) // TOP_K)
    blocks = jnp.arange(n_blocks)
    blk_expert = jnp.clip(jnp.searchsorted(block_end, blocks, side='right'),
                          0, N_EXPERTS - 1).astype(jnp.int32)
    blk_rows = jnp.clip(counts[blk_expert] - (blocks - block_start[blk_expert]) * MOE_TM,
                        0, MOE_TM).astype(jnp.int32)
    n_used = block_end[-1].astype(jnp.int32)
    blk_expert = jnp.where(blocks < n_used, blk_expert, blk_expert[jnp.maximum(n_used - 1, 0)])
    pos = dest.reshape(n, TOP_K)
    return tok_pad, blk_expert, blk_rows, n_used.reshape(1), pos[:, 0], pos[:, 1]


def kernel(x, norm_gains, a_w_in, a_b_in, a_norm_v, a_w_s, a_b_s, a_w_out, kv_norm, w_kv,
           b_w_q, b_w_o, ffn_w_gu, ffn_w_down, moe_router, moe_w_gu, moe_w_down):
    batch, seq, d = x.shape
    h = x.reshape(batch * seq, d)
    bf = lambda w: w.astype(BF16)

    g = norm_gains[0]
    (z, ssq), (w_out, w_gu, w_down) = _gmlp_in(
        h, g[0], bf(a_w_in[0]), a_b_in[0], [a_w_out[0], ffn_w_gu[0], ffn_w_down[0]])
    h, (w_kv16, w_q) = _gmlp_out(z, ssq, a_norm_v[0], a_w_s[0], a_b_s[0], h, g[1], w_out,
                                 [_whole_rider(w, 64) for w in (w_kv, b_w_q[0])])
    h, (w_o,) = _ffn_dense(h, g[2], w_gu, w_down, g[3], [_whole_rider(b_w_o[0], 56)])

    g = norm_gains[1]
    kv, _ = _norm_proj(h, kv_norm, w_kv16, [], tm=1024, tn=1024, name="proj_kv")
    q, _ = _norm_proj(h, g[0], w_q, [], tm=1024, tn=1024, name="proj_q")
    base = jnp.exp2(-8.0 * jnp.arange(1, N_KV_HEADS + 1, dtype=F32) / N_KV_HEADS)
    o, _ = _attention(base, q, kv, batch, seq, [])
    h = _attn_out(o, h, g[1], w_o)

    route, hn = _moe_route(h, g[2], moe_router[0])
    tok_pad, blk_expert, blk_rows, n_used, pos0, pos1 = _dispatch_plan(route)
    ys = _moe_experts(tok_pad, blk_expert, blk_rows, n_used, hn, moe_w_gu[0], moe_w_down[0])
    h = _moe_combine(pos0, pos1, ys, route, h, g[3])
    return h.reshape(batch, seq, d)
```

```python
import functools
from typing import NamedTuple

import jax
import jax.numpy as jnp
from jax import lax
from jax.experimental import pallas as pl
from jax.experimental.pallas import tpu as pltpu

F32 = jnp.float32
BF16 = jnp.bfloat16

D_MODEL = 2048
EPS = 1e-6
CHUNK = 128
GMLP_HALF = 2 * D_MODEL
GMLP_GROUPS = 8
GMLP_GROUP_CH = GMLP_HALF // GMLP_GROUPS
HEAD_DIM = 128
N_KV_HEADS = D_MODEL // HEAD_DIM
DILATED_GROUPS = ((128, 1), (512, 4), (2048, 16))
BLOCK = 128
NEG = -1e30
D_FF = 7168
N_EXPERTS = 8
TOP_K = 2

LANES = 128
VMEM_LIMIT_BYTES = 56 * 1024 * 1024

NORM_ROWS = 128
EPILOGUE_ROWS = 128
MOE_TM = 1152
MOE_TF = 512


def _params(*sem):
    return pltpu.CompilerParams(dimension_semantics=sem, vmem_limit_bytes=VMEM_LIMIT_BYTES)


def _rms_scale(x):
    return lax.rsqrt(jnp.mean(x * x, axis=-1, keepdims=True) + EPS)


def _norm_rows_into(x_ref, g_ref, out_ref):
    def body(c, carry):
        rows = pl.ds(pl.multiple_of(c * NORM_ROWS, NORM_ROWS), NORM_ROWS)
        x = x_ref[rows, :]
        out_ref[rows, :] = ((x * _rms_scale(x)) * g_ref[...]).astype(out_ref.dtype)
        return carry
    lax.fori_loop(0, x_ref.shape[0] // NORM_ROWS, body, 0)


def _residual_norm_rows(res_ref, acc_ref, g_ref, out_ref):
    def body(c, carry):
        rows = pl.ds(pl.multiple_of(c * NORM_ROWS, NORM_ROWS), NORM_ROWS)
        a = acc_ref[rows, :]
        out_ref[rows, :] = res_ref[rows, :] + (a * _rms_scale(a)) * g_ref[...]
        return carry
    lax.fori_loop(0, acc_ref.shape[0] // NORM_ROWS, body, 0)


BF16_SUBLANES = 16


class _Rider(NamedTuple):
    src: jax.Array
    steps: int
    rows: int


def _whole_rider(w, host_steps):
    total = w.shape[0]
    for steps in range(host_steps, 0, -1):
        if total % steps == 0 and (total // steps) % BF16_SUBLANES == 0:
            return _Rider(w, steps, total // steps)
    raise ValueError(f"no chunking of {w.shape} over {host_steps} steps")


def _riding_body(body, n_in, n_out, n_riders, *refs):
    ins, refs = refs[:n_in], refs[n_in:]
    r_in, refs = refs[:n_riders], refs[n_riders:]
    outs, refs = refs[:n_out], refs[n_out:]
    r_out, scratch = refs[:n_riders], refs[n_riders:]

    def cast_part(part, n_parts):
        for src, dst in zip(r_in, r_out):
            lane_tiles = src.shape[1] // LANES
            assert lane_tiles * LANES == src.shape[1] and lane_tiles >= n_parts
            cols = slice(part * lane_tiles // n_parts * LANES,
                         (part + 1) * lane_tiles // n_parts * LANES)
            dst[:, cols] = src[:, cols].astype(BF16)
    body(cast_part, *ins, *outs, *scratch)


def _hosted_call(body, *, grid, in_specs, out_specs, out_shape, scratch_shapes, args, riders,
                 semantics, name, num_scalar_prefetch=0):
    n_grid = len(grid)
    strides = [1] * n_grid
    for ax in range(n_grid - 2, -1, -1):
        strides[ax] = strides[ax + 1] * grid[ax + 1]
    assert all(r.steps <= strides[0] * grid[0] for r in riders)

    r_specs, r_shapes = [], []
    for r in riders:
        def index(*g, r=r):
            step = sum(g[ax] * strides[ax] for ax in range(n_grid))
            return jnp.minimum(step, r.steps - 1), 0
        r_specs.append(pl.BlockSpec((r.rows, r.src.shape[1]), index))
        r_shapes.append(jax.ShapeDtypeStruct(r.src.shape, BF16))

    n_in = num_scalar_prefetch + len(in_specs)
    n_out = len(out_specs)
    all_in = list(in_specs) + r_specs
    all_out = list(out_specs) + r_specs
    kernel_fn = functools.partial(_riding_body, body, n_in, n_out, len(riders))
    if num_scalar_prefetch:
        spec = dict(grid_spec=pltpu.PrefetchScalarGridSpec(
            num_scalar_prefetch=num_scalar_prefetch, grid=grid, in_specs=all_in,
            out_specs=all_out, scratch_shapes=scratch_shapes))
    else:
        spec = dict(grid=grid, in_specs=all_in, out_specs=all_out, scratch_shapes=scratch_shapes)
    outs = pl.pallas_call(
        kernel_fn,
        out_shape=list(out_shape) + r_shapes,
        compiler_params=_params(*semantics),
        name=name,
        **spec,
    )(*args, *[r.src for r in riders])
    return outs[:n_out], outs[n_out:]


def _gelu_tanh(x):
    cdf = 0.5 * (1.0 + jnp.tanh(0.7978845608028654 * (x + 0.044715 * (x * x * x))))
    return x * cdf


def _row_chunks(ref, chunk=EPILOGUE_ROWS):
    n = ref.shape[0] // chunk
    return [(c, n, slice(c * chunk, (c + 1) * chunk)) for c in range(n)]


def _proj_body(cast_part, x_ref, g_ref, w_ref, o_ref, xn_ref):
    @pl.when(pl.program_id(1) == 0)
    def _():
        _norm_rows_into(x_ref, g_ref, xn_ref)
    for c, n, rows in _row_chunks(xn_ref, 4 * EPILOGUE_ROWS):
        o_ref[rows, :] = jnp.dot(xn_ref[rows, :], w_ref[...],
                                 preferred_element_type=F32).astype(o_ref.dtype)
        cast_part(c, n)


def _norm_proj(x, gain, w, riders, *, tm, tn, name):
    m, k = x.shape
    n = w.shape[1]
    (out,), casts = _hosted_call(
        _proj_body,
        grid=(m // tm, n // tn),
        in_specs=[
            pl.BlockSpec((tm, k), lambda i, j: (i, 0)),
            pl.BlockSpec((1, k), lambda i, j: (0, 0)),
            pl.BlockSpec((k, tn), lambda i, j: (0, j)),
        ],
        out_specs=[pl.BlockSpec((tm, tn), lambda i, j: (i, j))],
        out_shape=[jax.ShapeDtypeStruct((m, n), BF16)],
        scratch_shapes=[pltpu.VMEM((tm, k), BF16)],
        args=(x, gain.reshape(1, k), w),
        riders=riders,
        semantics=("arbitrary", "arbitrary"),
        name=name,
    )
    return out, casts


def _gmlp_in_body(cast_part, x_ref, g_ref, w_ref, b_ref, z_ref, ssq_ref, xn_ref, ss_ref):
    j = pl.program_id(1)
    nj = pl.num_programs(1)

    @pl.when(j == 0)
    def _():
        _norm_rows_into(x_ref, g_ref, xn_ref)
        ss_ref[...] = jnp.zeros_like(ss_ref)

    is_v = (j >= nj // 2).astype(F32)
    for c, n, rows in _row_chunks(xn_ref, 2 * EPILOGUE_ROWS):
        z = jnp.dot(xn_ref[rows, :], w_ref[...], preferred_element_type=F32) + b_ref[...]
        z = _gelu_tanh(z)
        z_ref[rows, :] = z.astype(z_ref.dtype)
        ss_ref[rows, :] += is_v * jnp.sum(z * z, axis=-1, keepdims=True)
        cast_part(c, n)

    @pl.when(j == nj - 1)
    def _():
        ssq_ref[...] = jnp.broadcast_to(ss_ref[...], ssq_ref.shape)


def _gmlp_in(x, gain, w_in, b_in, later_weights, *, tm=2048, tn=512):
    m, k = x.shape
    n = w_in.shape[1]
    grid = (m // tm, n // tn)
    return _hosted_call(
        _gmlp_in_body,
        grid=grid,
        in_specs=[
            pl.BlockSpec((tm, k), lambda i, j: (i, 0), pipeline_mode=pl.Buffered(1)),
            pl.BlockSpec((1, k), lambda i, j: (0, 0)),
            pl.BlockSpec((k, tn), lambda i, j: (0, j)),
            pl.BlockSpec((1, tn), lambda i, j: (0, j)),
        ],
        out_specs=[
            pl.BlockSpec((tm, tn), lambda i, j: (i, j)),
            pl.BlockSpec((tm, LANES), lambda i, j: (i, 0)),
        ],
        out_shape=[
            jax.ShapeDtypeStruct((m, n), BF16),
            jax.ShapeDtypeStruct((m, LANES), F32),
        ],
        scratch_shapes=[pltpu.VMEM((tm, k), BF16), pltpu.VMEM((tm, 1), F32)],
        args=(x, gain.reshape(1, k), w_in, b_in.reshape(1, n)),
        riders=[_whole_rider(w, grid[0] * grid[1]) for w in later_weights],
        semantics=("arbitrary", "arbitrary"),
        name="gmlp_in",
    )


def _gmlp_out_body(cast_part, u_ref, v_ref, ssq_ref, gv_ref, ws_ref, bs_ref, x_ref, g_ref, wo_ref,
                   o_ref):
    grp = pl.program_id(1)
    tm = u_ref.shape[0]
    acc_ref = o_ref

    @pl.when(grp == 0)
    def _():
        acc_ref[...] = jnp.zeros_like(acc_ref)

    row = lax.broadcasted_iota(jnp.int32, (CHUNK, CHUNK), 0)
    col = lax.broadcasted_iota(jnp.int32, (CHUNK, CHUNK), 1)
    ws = jnp.where(row >= col, ws_ref[0], 0.0).astype(BF16)
    r = lax.rsqrt(ssq_ref[:, 0:1] * (1.0 / GMLP_HALF) + EPS)
    def project(rows, gated):
        acc_ref[rows, :] += jnp.dot(gated, wo_ref[...], preferred_element_type=F32)

    pending = None
    n = tm // CHUNK
    for c in range(n):
        rows = slice(c * CHUNK, (c + 1) * CHUNK)
        vn = (v_ref[rows, :].astype(F32) * r[rows, :]) * gv_ref[...]
        mixed = jnp.dot(ws, vn.astype(BF16), preferred_element_type=F32) + bs_ref[0]
        gated = (u_ref[rows, :].astype(F32) * mixed).astype(BF16)
        if pending is not None:
            project(*pending)
        pending = (rows, gated)
        cast_part(c, n)
    project(*pending)

    @pl.when(grp == pl.num_programs(1) - 1)
    def _():
        _residual_norm_rows(x_ref, acc_ref, g_ref, o_ref)


def _gmlp_out(z, ssq, norm_v, w_s, b_s, x, gain, w_out, riders, *, tm=1024):
    m, d = x.shape
    gc = GMLP_GROUP_CH
    (out,), casts = _hosted_call(
        _gmlp_out_body,
        grid=(m // tm, GMLP_GROUPS),
        in_specs=[
            pl.BlockSpec((tm, gc), lambda i, g: (i, g)),
            pl.BlockSpec((tm, gc), lambda i, g: (i, GMLP_GROUPS + g)),
            pl.BlockSpec((tm, LANES), lambda i, g: (i, 0)),
            pl.BlockSpec((1, gc), lambda i, g: (0, g)),
            pl.BlockSpec((1, CHUNK, CHUNK), lambda i, g: (g, 0, 0)),
            pl.BlockSpec((1, CHUNK, 1), lambda i, g: (g, 0, 0)),
            pl.BlockSpec((tm, d), lambda i, g: (i, 0)),
            pl.BlockSpec((1, d), lambda i, g: (0, 0)),
            pl.BlockSpec((gc, d), lambda i, g: (g, 0)),
        ],
        out_specs=[pl.BlockSpec((tm, d), lambda i, g: (i, 0))],
        out_shape=[jax.ShapeDtypeStruct((m, d), F32)],
        scratch_shapes=[],
        args=(z, z, ssq, norm_v.reshape(1, GMLP_HALF), w_s, b_s[:, :, None], x,
              gain.reshape(1, d), w_out),
        riders=riders,
        semantics=("arbitrary", "arbitrary"),
        name="gmlp_out",
    )
    return out, casts


SWIGLU_ROWS = 256


def _chunks_covering(n_rows):
    assert n_rows % EPILOGUE_ROWS == 0
    bounds = list(range(0, n_rows, SWIGLU_ROWS)) + [n_rows]
    return [slice(lo, hi) for lo, hi in zip(bounds[:-1], bounds[1:])]


def _swiglu_accumulate(xn_ref, wg, wu, wd, acc_ref, cast_part=None, n_rows=None):
    chunks = _chunks_covering(n_rows or xn_ref.shape[0])

    def down(rows, a):
        acc_ref[rows, :] += jnp.dot(a, wd(), preferred_element_type=F32)

    pending = None
    for c, rows in enumerate(chunks):
        x = xn_ref[rows, :]
        g = jnp.dot(x, wg(), preferred_element_type=F32)
        u = jnp.dot(x, wu(), preferred_element_type=F32)
        a = ((g * jax.nn.sigmoid(g)) * u).astype(BF16)
        if pending is not None:
            down(*pending)
        pending = (rows, a)
        if cast_part is not None:
            cast_part(c, len(chunks))
    down(*pending)


def _cast_once(src_ref, dst_ref):
    done = []

    def get():
        if not done:
            dst_ref[...] = src_ref[0].astype(BF16)
            done.append(True)
        return dst_ref[...]
    return get


def _ffn_body(cast_part, x_ref, g_in_ref, wg_ref, wu_ref, wd_ref, g_out_ref, o_ref, xn_ref):
    j = pl.program_id(1)
    acc_ref = o_ref

    @pl.when(j == 0)
    def _():
        _norm_rows_into(x_ref, g_in_ref, xn_ref)
        acc_ref[...] = jnp.zeros_like(acc_ref)

    _swiglu_accumulate(xn_ref, lambda: wg_ref[...], lambda: wu_ref[...], lambda: wd_ref[...],
                       acc_ref, cast_part)

    @pl.when(j == pl.num_programs(1) - 1)
    def _():
        _residual_norm_rows(x_ref, acc_ref, g_out_ref, o_ref)


def _ffn_dense(x, g_in, w_gu, w_down, g_out, riders, *, tm=1024, tf=1024):
    m, d = x.shape
    nj = D_FF // tf
    once = dict(pipeline_mode=pl.Buffered(1))
    (out,), casts = _hosted_call(
        _ffn_body,
        grid=(m // tm, nj),
        in_specs=[
            pl.BlockSpec((tm, d), lambda i, j: (i, 0), **once),
            pl.BlockSpec((1, d), lambda i, j: (0, 0)),
            pl.BlockSpec((d, tf), lambda i, j: (0, j)),
            pl.BlockSpec((d, tf), lambda i, j: (0, nj + j)),
            pl.BlockSpec((tf, d), lambda i, j: (j, 0)),
            pl.BlockSpec((1, d), lambda i, j: (0, 0)),
        ],
        out_specs=[pl.BlockSpec((tm, d), lambda i, j: (i, 0), **once)],
        out_shape=[jax.ShapeDtypeStruct((m, d), F32)],
        scratch_shapes=[pltpu.VMEM((tm, d), BF16)],
        args=(x, g_in.reshape(1, d), w_gu, w_gu, w_down, g_out.reshape(1, d)),
        riders=riders,
        semantics=("arbitrary", "arbitrary"),
        name="ffn_dense",
    )
    return out, casts


SUB = 4
ATTN_UNROLL = 8


def _deinterleave(src_ref, dst_ref, span):
    part = span // SUB
    for base in range(0, src_ref.shape[0], span):
        for r in range(SUB):
            dst_ref[base + r * part: base + (r + 1) * part, :] = (
                src_ref[pl.ds(base + r, part, stride=SUB), :].astype(dst_ref.dtype))


def _interleave(src_ref, dst_ref, span):
    part = span // SUB
    for base in range(0, src_ref.shape[0], span):
        for r in range(SUB):
            dst_ref[pl.ds(base + r, part, stride=SUB), :] = (
                src_ref[base + r * part: base + (r + 1) * part, :])


def _window_attention(q_ref, k_ref, v_ref, slope, dilation, blocks_per_seq,
                      bias_c_ref, bias_p_ref, pc_ref, pp_ref, den_ref, o_ref, lse_ref):
    jq = lax.broadcasted_iota(jnp.int32, (BLOCK, BLOCK), 0)
    kk = lax.broadcasted_iota(jnp.int32, (BLOCK, BLOCK), 1)
    dist_c = ((jq - kk) * dilation).astype(F32)
    dist_p = ((BLOCK + jq - kk) * dilation).astype(F32)
    bias_c_ref[...] = jnp.where(kk <= jq, -(slope * dist_c), NEG)
    bias_p_ref[...] = jnp.where(kk >= jq, -(slope * dist_p), NEG)
    scale = HEAD_DIM ** -0.5
    contract_last = (((1,), (1,)), ((), ()))
    with_prev = blocks_per_seq > 1

    def block_rows(n):
        rows = pl.ds(pl.multiple_of(n * BLOCK, BLOCK), BLOCK)
        prev = pl.ds(pl.multiple_of(jnp.maximum(n - 1, 0) * BLOCK, BLOCK), BLOCK)
        return rows, prev

    def probabilities(n, carry):
        rows, prev = block_rows(n)
        q = q_ref[rows, :]
        s_c = lax.dot_general(q, k_ref[rows, :], contract_last, preferred_element_type=F32)
        s_c = s_c * scale + bias_c_ref[...]
        if with_prev:
            has_prev = (n % blocks_per_seq) != 0
            s_p = lax.dot_general(q, k_ref[prev, :], contract_last, preferred_element_type=F32)
            s_p = jnp.where(has_prev, s_p * scale + bias_p_ref[...], NEG)
            m = jnp.max(jnp.maximum(s_c, s_p), axis=-1, keepdims=True)
            p_c = jnp.exp(s_c - m)
            p_p = jnp.exp(s_p - m)
            den = jnp.sum(p_c + p_p, axis=-1, keepdims=True)
            pp_ref[rows, :] = p_p.astype(BF16)
        else:
            m = jnp.max(s_c, axis=-1, keepdims=True)
            p_c = jnp.exp(s_c - m)
            den = jnp.sum(p_c, axis=-1, keepdims=True)
        pc_ref[rows, :] = p_c.astype(BF16)
        den_ref[rows, :] = den
        lse_ref[rows, :] = jnp.broadcast_to(m + jnp.log(den), (BLOCK, LANES))
        return carry

    def values(n, carry):
        rows, prev = block_rows(n)
        o = jnp.dot(pc_ref[rows, :], v_ref[rows, :], preferred_element_type=F32)
        if with_prev:
            o = o + jnp.dot(pp_ref[rows, :], v_ref[prev, :], preferred_element_type=F32)
        o_ref[rows, :] = o / den_ref[rows, :]
        return carry

    n_blocks = q_ref.shape[0] // BLOCK
    lax.fori_loop(0, n_blocks, probabilities, 0, unroll=ATTN_UNROLL)
    lax.fori_loop(0, n_blocks, values, 0, unroll=ATTN_UNROLL)


def _attn_body(cast_part, base_ref, q1_ref, q4_ref, q16_ref, k_ref, v_ref, out_ref,
               stage_ref, tmp_ref, k4f_ref, v4f_ref,
               qp_ref, k4_ref, v4_ref, k16_ref, v16_ref,
               o1_ref, l1_ref, o4_ref, l4_ref, o16_ref, l16_ref, op_ref, lp_ref,
               bias_c_ref, bias_p_ref, pc_ref, pp_ref, den_ref):
    seq = k_ref.shape[0]
    base = base_ref[pl.program_id(1)]
    (_, d1), (_, d4), (_, d16) = DILATED_GROUPS
    cast_part(0, 1)

    stage_ref[...] = k_ref[...].astype(F32)
    _deinterleave(stage_ref, k4f_ref, seq)
    k4_ref[...] = k4f_ref[...].astype(BF16)
    _deinterleave(k4f_ref, k16_ref, seq // SUB)
    stage_ref[...] = v_ref[...].astype(F32)
    _deinterleave(stage_ref, v4f_ref, seq)
    v4_ref[...] = v4f_ref[...].astype(BF16)
    _deinterleave(v4f_ref, v16_ref, seq // SUB)

    _window_attention(q1_ref, k_ref, v_ref, base / d1, d1, seq // BLOCK,
                      bias_c_ref, bias_p_ref, pc_ref, pp_ref, den_ref, o1_ref, l1_ref)

    stage_ref[...] = q4_ref[...].astype(F32)
    _deinterleave(stage_ref, qp_ref, seq)
    _window_attention(qp_ref, k4_ref, v4_ref, base / d4, d4, seq // d4 // BLOCK,
                      bias_c_ref, bias_p_ref, pc_ref, pp_ref, den_ref, op_ref, lp_ref)
    _interleave(op_ref, o4_ref, seq)
    _interleave(lp_ref, l4_ref, seq)

    stage_ref[...] = q16_ref[...].astype(F32)
    _deinterleave(stage_ref, tmp_ref, seq)
    _deinterleave(tmp_ref, qp_ref, seq // SUB)
    _window_attention(qp_ref, k16_ref, v16_ref, base / d16, d16, seq // d16 // BLOCK,
                      bias_c_ref, bias_p_ref, pc_ref, pp_ref, den_ref, op_ref, lp_ref)
    _interleave(op_ref, tmp_ref, seq // SUB)
    _interleave(tmp_ref, o16_ref, seq)
    _interleave(lp_ref, tmp_ref, seq // SUB)
    _interleave(tmp_ref, l16_ref, seq)

    chunk = 2 * BLOCK

    def merge(c, carry):
        rows = pl.ds(pl.multiple_of(c * chunk, chunk), chunk)
        la, lb, lc = l1_ref[rows, :], l4_ref[rows, :], l16_ref[rows, :]
        mx = jnp.maximum(jnp.maximum(la, lb), lc)
        ea, eb, ec = jnp.exp(la - mx), jnp.exp(lb - mx), jnp.exp(lc - mx)
        den = ea + eb + ec
        o = (ea / den) * o1_ref[rows, :] + (eb / den) * o4_ref[rows, :] + (ec / den) * o16_ref[rows, :]
        out_ref[rows, :] = o.astype(out_ref.dtype)
        return carry
    lax.fori_loop(0, seq // chunk, merge, 0)


def _attention(base, q, kv, batch, seq, riders):
    for window, dilation in DILATED_GROUPS:
        assert window // dilation == BLOCK, "keys per query must span exactly one previous block"
    assert [d for _, d in DILATED_GROUPS] == [1, SUB, SUB * SUB]
    assert seq % (SUB * SUB * BLOCK) == 0
    h = N_KV_HEADS
    blk = (seq, HEAD_DIM)
    f32buf = pltpu.VMEM((seq, HEAD_DIM), F32)
    bf16buf = pltpu.VMEM((seq, HEAD_DIM), BF16)
    (out,), casts = _hosted_call(
        _attn_body,
        num_scalar_prefetch=1,
        grid=(batch, h),
        in_specs=[
            pl.BlockSpec(blk, lambda b, hh, base: (b, hh)),
            pl.BlockSpec(blk, lambda b, hh, base: (b, h + hh)),
            pl.BlockSpec(blk, lambda b, hh, base: (b, 2 * h + hh)),
            pl.BlockSpec(blk, lambda b, hh, base: (b, hh)),
            pl.BlockSpec(blk, lambda b, hh, base: (b, h + hh)),
        ],
        out_specs=[pl.BlockSpec(blk, lambda b, hh, base: (b, hh))],
        out_shape=[jax.ShapeDtypeStruct((batch * seq, D_MODEL), BF16)],
        scratch_shapes=[f32buf] * 4 + [bf16buf] * 5 + [f32buf] * 8
                       + [pltpu.VMEM((BLOCK, BLOCK), F32)] * 2
                       + [bf16buf] * 2 + [pltpu.VMEM((seq, 1), F32)],
        args=(base, q, q, q, kv, kv),
        riders=riders,
        semantics=("arbitrary", "arbitrary"),
        name="attn",
    )
    return out, casts


def _attn_out_body(o_ref, x_ref, g_ref, wo_ref, out_ref, acc_ref):
    acc_ref[...] = jnp.dot(o_ref[...], wo_ref[...], preferred_element_type=F32)
    _residual_norm_rows(x_ref, acc_ref, g_ref, out_ref)


def _attn_out(o, x, gain, w_o, *, tm=512):
    m, d = x.shape
    row_blk = pl.BlockSpec((tm, d), lambda i: (i, 0))
    return pl.pallas_call(
        _attn_out_body,
        grid=(m // tm,),
        in_specs=[row_blk, row_blk,
                  pl.BlockSpec((1, d), lambda i: (0, 0)),
                  pl.BlockSpec((d, d), lambda i: (0, 0))],
        out_specs=row_blk,
        out_shape=jax.ShapeDtypeStruct((m, d), F32),
        scratch_shapes=[pltpu.VMEM((tm, d), F32)],
        compiler_params=_params("parallel"),
        name="attn_out",
    )(o, x, gain.reshape(1, d), w_o)


def _route_body(x_ref, g_ref, wr_ref, o_ref, xn_ref):
    _norm_rows_into(x_ref, g_ref, xn_ref)
    logits = jnp.dot(xn_ref[...], wr_ref[...], preferred_element_type=F32,
                     precision=lax.Precision.HIGHEST)
    lane = lax.broadcasted_iota(jnp.int32, logits.shape, 1)
    logits = jnp.where(lane < N_EXPERTS, logits, -jnp.inf)
    m1 = jnp.max(logits, axis=-1, keepdims=True)
    i1 = jnp.min(jnp.where(logits == m1, lane, LANES), axis=-1, keepdims=True)
    rest = jnp.where(lane == i1, -jnp.inf, logits)
    m2 = jnp.max(rest, axis=-1, keepdims=True)
    i2 = jnp.min(jnp.where(rest == m2, lane, LANES), axis=-1, keepdims=True)
    t = jnp.exp(m2 - m1)
    den = 1.0 + t
    out = jnp.where(lane == 0, i1.astype(F32), 0.0)
    out = jnp.where(lane == 1, i2.astype(F32), out)
    out = jnp.where(lane == 2, 1.0 / den, out)
    out = jnp.where(lane == 3, t / den, out)
    o_ref[...] = out


def _moe_route(x, gain, w_router, *, tm=512):
    m, d = x.shape
    wr = jnp.zeros((d, LANES), F32).at[:, :N_EXPERTS].set(w_router)
    return pl.pallas_call(
        _route_body,
        grid=(m // tm,),
        in_specs=[pl.BlockSpec((tm, d), lambda i: (i, 0)),
                  pl.BlockSpec((1, d), lambda i: (0, 0)),
                  pl.BlockSpec((d, LANES), lambda i: (0, 0))],
        out_specs=[pl.BlockSpec((tm, LANES), lambda i: (i, 0)),
                   pl.BlockSpec((tm, d), lambda i: (i, 0))],
        out_shape=[jax.ShapeDtypeStruct((m, LANES), F32),
                   jax.ShapeDtypeStruct((m, d), F32)],
        compiler_params=_params("parallel"),
        name="moe_route",
    )(x, gain.reshape(1, d), wr)


def _row_copy(src_hbm, row, dst_ref, r, sem):
    return pltpu.make_async_copy(src_hbm.at[pl.ds(row, 1), :], dst_ref.at[pl.ds(r, 1), :], sem)


GATHER_UNROLL = 8
GATHER_QUEUE = 1
GATHER_SLICES = 12


def _start_row_gather(idx_ref, base, src_hbm, dst_ref, sem, priority, first=0, count=None):
    count = dst_ref.shape[0] if count is None else count

    def start(k, carry):
        r = first + k
        _row_copy(src_hbm, idx_ref[base + r], dst_ref, r, sem).start(priority=priority)
        return carry
    lax.fori_loop(0, count, start, 0, unroll=GATHER_UNROLL)


def _wait_row_gather(idx_ref, base, src_hbm, dst_ref, sem):
    def wait(r, carry):
        _row_copy(src_hbm, idx_ref[base + r], dst_ref, r, sem).wait()
        return carry
    lax.fori_loop(0, dst_ref.shape[0], wait, 0, unroll=GATHER_UNROLL)


def _experts_body(tok_ref, bexp_ref, rows_ref, nused_ref, hn_hbm, wg_ref, wu_ref, wd_ref,
                  ys_ref, xg_ref, xn_ref, wg16_ref, wu16_ref, wd16_ref, sem):
    i = pl.program_id(0)
    j = pl.program_id(1)
    n_used = nused_ref[0]
    used = i < n_used
    acc_ref = ys_ref

    @pl.when(j == 0)
    def _():
        acc_ref[...] = jnp.zeros_like(acc_ref)

    @pl.when(jnp.logical_and(used, j == 0))
    def _():
        @pl.when(i == 0)
        def _():
            _start_row_gather(tok_ref, 0, hn_hbm, xg_ref, sem, GATHER_QUEUE)
        _wait_row_gather(tok_ref, i * MOE_TM, hn_hbm, xg_ref, sem)
        xn_ref[...] = xg_ref[...].astype(BF16)

    per_step = MOE_TM // GATHER_SLICES
    assert GATHER_SLICES < D_FF // MOE_TF and per_step * GATHER_SLICES == MOE_TM

    @pl.when(jnp.logical_and(i + 1 < n_used, jnp.logical_and(j >= 1, j <= GATHER_SLICES)))
    def _():
        _start_row_gather(tok_ref, (i + 1) * MOE_TM, hn_hbm, xg_ref, sem, GATHER_QUEUE,
                          first=(j - 1) * per_step, count=per_step)

    n_rows = rows_ref[i]
    variants = [r for r in (128, 256, 512, 1024, 2048) if r < MOE_TM] + [MOE_TM]
    lower = 0
    for upper in variants:
        fits = jnp.logical_and(n_rows > lower, n_rows <= upper)

        @pl.when(jnp.logical_and(used, fits))
        def _(upper=upper):
            _swiglu_accumulate(xn_ref, _cast_once(wg_ref, wg16_ref), _cast_once(wu_ref, wu16_ref),
                               _cast_once(wd_ref, wd16_ref), acc_ref, n_rows=upper)
        lower = upper


def _moe_experts(tok_pad, blk_expert, blk_rows, n_used, hn, w_gu, w_down):
    d = D_MODEL
    n_blocks = tok_pad.shape[0] // MOE_TM
    nj = D_FF // MOE_TF

    def col(i, j, nused):
        return jnp.where(i < nused[0], j, nj - 1)

    return pl.pallas_call(
        _experts_body,
        grid_spec=pltpu.PrefetchScalarGridSpec(
            num_scalar_prefetch=4,
            grid=(n_blocks, nj),
            in_specs=[
                pl.BlockSpec(memory_space=pl.ANY),
                pl.BlockSpec((1, d, MOE_TF), lambda i, j, tok, be, br, nu: (be[i], 0, col(i, j, nu))),
                pl.BlockSpec((1, d, MOE_TF),
                             lambda i, j, tok, be, br, nu: (be[i], 0, nj + col(i, j, nu))),
                pl.BlockSpec((1, MOE_TF, d), lambda i, j, tok, be, br, nu: (be[i], col(i, j, nu), 0)),
            ],
            out_specs=pl.BlockSpec((MOE_TM, d), lambda i, j, tok, be, br, nu: (i, 0),
                                   pipeline_mode=pl.Buffered(1)),
            scratch_shapes=[pltpu.VMEM((MOE_TM, d), F32), pltpu.VMEM((MOE_TM, d), BF16),
                            pltpu.VMEM((d, MOE_TF), BF16), pltpu.VMEM((d, MOE_TF), BF16),
                            pltpu.VMEM((MOE_TF, d), BF16), pltpu.SemaphoreType.DMA],
        ),
        out_shape=jax.ShapeDtypeStruct((n_blocks * MOE_TM, d), F32),
        compiler_params=_params("arbitrary", "arbitrary"),
        name="moe_experts",
    )(tok_pad, blk_expert, blk_rows, n_used, hn, w_gu, w_gu, w_down)


def _combine_body(p0_ref, p1_ref, ys_hbm, route_ref, x_ref, g_ref, o_ref, a_ref, b_ref, sem):
    tm = x_ref.shape[0]
    base = pl.program_id(0) * tm
    _start_row_gather(p0_ref, base, ys_hbm, a_ref, sem.at[0], 0)
    _start_row_gather(p1_ref, base, ys_hbm, b_ref, sem.at[1], 1)
    _wait_row_gather(p0_ref, base, ys_hbm, a_ref, sem.at[0])
    _wait_row_gather(p1_ref, base, ys_hbm, b_ref, sem.at[1])
    gate0 = route_ref[:, TOP_K:TOP_K + 1]
    gate1 = route_ref[:, TOP_K + 1:TOP_K + 2]
    a_ref[...] = a_ref[...] * gate0 + b_ref[...] * gate1
    _residual_norm_rows(x_ref, a_ref, g_ref, o_ref)


def _moe_combine(pos0, pos1, ys, route, x, gain, *, tm=256):
    m, d = x.shape
    return pl.pallas_call(
        _combine_body,
        grid_spec=pltpu.PrefetchScalarGridSpec(
            num_scalar_prefetch=2,
            grid=(m // tm,),
            in_specs=[
                pl.BlockSpec(memory_space=pl.ANY),
                pl.BlockSpec((tm, LANES), lambda i, p0, p1: (i, 0)),
                pl.BlockSpec((tm, d), lambda i, p0, p1: (i, 0)),
                pl.BlockSpec((1, d), lambda i, p0, p1: (0, 0)),
            ],
            out_specs=pl.BlockSpec((tm, d), lambda i, p0, p1: (i, 0)),
            scratch_shapes=[pltpu.VMEM((tm, d), F32), pltpu.VMEM((tm, d), F32),
                            pltpu.SemaphoreType.DMA((2,))],
        ),
        out_shape=jax.ShapeDtypeStruct((m, d), F32),
        compiler_params=_params("arbitrary"),
        name="moe_combine",
    )(pos0, pos1, ys, route, x, gain.reshape(1, d))


def _dispatch_plan(route):
    n = route.shape[0]
    experts = route[:, :TOP_K].astype(jnp.int32).reshape(-1)
    onehot = (experts[:, None] == jnp.arange(N_EXPERTS)[None, :]).astype(jnp.int32)
    rank = jnp.take_along_axis(jnp.cumsum(onehot, axis=0) - onehot, experts[:, None], axis=1)[:, 0]
    counts = jnp.sum(onehot, axis=0)
    blocks_per_expert = (counts + MOE_TM - 1) // MOE_TM
    block_end = jnp.cumsum(blocks_per_expert)
    block_start = block_end - blocks_per_expert
    dest = block_start[experts] * MOE_TM + rank
    n_blocks = (n * TOP_K) // MOE_TM + N_EXPERTS
    tok_pad = jnp.zeros((n_blocks * MOE_TM,), jnp.int32).at[dest].set(jnp.arange(n * TOP_K) // TOP_K)
    blocks = jnp.arange(n_blocks)
    blk_expert = jnp.clip(jnp.searchsorted(block_end, blocks, side='right'),
                          0, N_EXPERTS - 1).astype(jnp.int32)
    blk_rows = jnp.clip(counts[blk_expert] - (blocks - block_start[blk_expert]) * MOE_TM,
                        0, MOE_TM).astype(jnp.int32)
    n_used = block_end[-1].astype(jnp.int32)
    blk_expert = jnp.where(blocks < n_used, blk_expert, blk_expert[jnp.maximum(n_used - 1, 0)])
    pos = dest.reshape(n, TOP_K)
    return tok_pad, blk_expert, blk_rows, n_used.reshape(1), pos[:, 0], pos[:, 1]


def kernel(x, norm_gains, a_w_in, a_b_in, a_norm_v, a_w_s, a_b_s, a_w_out, kv_norm, w_kv,
           b_w_q, b_w_o, ffn_w_gu, ffn_w_down, moe_router, moe_w_gu, moe_w_down):
    batch, seq, d = x.shape
    h = x.reshape(batch * seq, d)
    bf = lambda w: w.astype(BF16)

    g = norm_gains[0]
    (z, ssq), (w_out, w_gu, w_down) = _gmlp_in(
        h, g[0], bf(a_w_in[0]), a_b_in[0], [a_w_out[0], ffn_w_gu[0], ffn_w_down[0]])
    h, (w_kv16, w_q) = _gmlp_out(z, ssq, a_norm_v[0], a_w_s[0], a_b_s[0], h, g[1], w_out,
                                 [_whole_rider(w, 64) for w in (w_kv, b_w_q[0])])
    h, (w_o,) = _ffn_dense(h, g[2], w_gu, w_down, g[3], [_whole_rider(b_w_o[0], 56)])

    g = norm_gains[1]
    kv, _ = _norm_proj(h, kv_norm, w_kv16, [], tm=1024, tn=1024, name="proj_kv")
    q, _ = _norm_proj(h, g[0], w_q, [], tm=1024, tn=1024, name="proj_q")
    base = jnp.exp2(-8.0 * jnp.arange(1, N_KV_HEADS + 1, dtype=F32) / N_KV_HEADS)
    o, _ = _attention(base, q, kv, batch, seq, [])
    h = _attn_out(o, h, g[1], w_o)

    route, hn = _moe_route(h, g[2], moe_router[0])
    tok_pad, blk_expert, blk_rows, n_used, pos0, pos1 = _dispatch_plan(route)
    ys = _moe_experts(tok_pad, blk_expert, blk_rows, n_used, hn, moe_w_gu[0], moe_w_down[0])
    h = _moe_combine(pos0, pos1, ys, route, h, g[3])
    return h.reshape(batch, seq, d)
```

```python
import functools
from typing import NamedTuple

import jax
import jax.numpy as jnp
from jax import lax
from jax.experimental import pallas as pl
from jax.experimental.pallas import tpu as pltpu

F32 = jnp.float32
BF16 = jnp.bfloat16

D_MODEL = 2048
EPS = 1e-6
CHUNK = 128
GMLP_HALF = 2 * D_MODEL
GMLP_GROUPS = 8
GMLP_GROUP_CH = GMLP_HALF // GMLP_GROUPS
HEAD_DIM = 128
N_KV_HEADS = D_MODEL // HEAD_DIM
DILATED_GROUPS = ((128, 1), (512, 4), (2048, 16))
BLOCK = 128
NEG = -1e30
D_FF = 7168
N_EXPERTS = 8
TOP_K = 2

LANES = 128
VMEM_LIMIT_BYTES = 56 * 1024 * 1024

NORM_ROWS = 128
EPILOGUE_ROWS = 128
MOE_TM = 1152
MOE_TF = 512


def _params(*sem):
    return pltpu.CompilerParams(dimension_semantics=sem, vmem_limit_bytes=VMEM_LIMIT_BYTES)


def _rms_scale(x):
    return lax.rsqrt(jnp.mean(x * x, axis=-1, keepdims=True) + EPS)


def _norm_rows_into(x_ref, g_ref, out_ref):
    def body(c, carry):
        rows = pl.ds(pl.multiple_of(c * NORM_ROWS, NORM_ROWS), NORM_ROWS)
        x = x_ref[rows, :]
        out_ref[rows, :] = ((x * _rms_scale(x)) * g_ref[...]).astype(out_ref.dtype)
        return carry
    lax.fori_loop(0, x_ref.shape[0] // NORM_ROWS, body, 0)


def _residual_norm_rows(res_ref, acc_ref, g_ref, out_ref):
    def body(c, carry):
        rows = pl.ds(pl.multiple_of(c * NORM_ROWS, NORM_ROWS), NORM_ROWS)
        a = acc_ref[rows, :]
        out_ref[rows, :] = res_ref[rows, :] + (a * _rms_scale(a)) * g_ref[...]
        return carry
    lax.fori_loop(0, acc_ref.shape[0] // NORM_ROWS, body, 0)


BF16_SUBLANES = 16


class _Rider(NamedTuple):
    src: jax.Array
    steps: int
    rows: int


def _whole_rider(w, host_steps):
    total = w.shape[0]
    for steps in range(host_steps, 0, -1):
        if total % steps == 0 and (total // steps) % BF16_SUBLANES == 0:
            return _Rider(w, steps, total // steps)
    raise ValueError(f"no chunking of {w.shape} over {host_steps} steps")


def _riding_body(body, n_in, n_out, n_riders, *refs):
    ins, refs = refs[:n_in], refs[n_in:]
    r_in, refs = refs[:n_riders], refs[n_riders:]
    outs, refs = refs[:n_out], refs[n_out:]
    r_out, scratch = refs[:n_riders], refs[n_riders:]

    def cast_part(part, n_parts):
        for src, dst in zip(r_in, r_out):
            lane_tiles = src.shape[1] // LANES
            assert lane_tiles * LANES == src.shape[1] and lane_tiles >= n_parts
            cols = slice(part * lane_tiles // n_parts * LANES,
                         (part + 1) * lane_tiles // n_parts * LANES)
            dst[:, cols] = src[:, cols].astype(BF16)
    body(cast_part, *ins, *outs, *scratch)


def _hosted_call(body, *, grid, in_specs, out_specs, out_shape, scratch_shapes, args, riders,
                 semantics, name, num_scalar_prefetch=0):
    n_grid = len(grid)
    strides = [1] * n_grid
    for ax in range(n_grid - 2, -1, -1):
        strides[ax] = strides[ax + 1] * grid[ax + 1]
    assert all(r.steps <= strides[0] * grid[0] for r in riders)

    r_specs, r_shapes = [], []
    for r in riders:
        def index(*g, r=r):
            step = sum(g[ax] * strides[ax] for ax in range(n_grid))
            return jnp.minimum(step, r.steps - 1), 0
        r_specs.append(pl.BlockSpec((r.rows, r.src.shape[1]), index))
        r_shapes.append(jax.ShapeDtypeStruct(r.src.shape, BF16))

    n_in = num_scalar_prefetch + len(in_specs)
    n_out = len(out_specs)
    all_in = list(in_specs) + r_specs
    all_out = list(out_specs) + r_specs
    kernel_fn = functools.partial(_riding_body, body, n_in, n_out, len(riders))
    if num_scalar_prefetch:
        spec = dict(grid_spec=pltpu.PrefetchScalarGridSpec(
            num_scalar_prefetch=num_scalar_prefetch, grid=grid, in_specs=all_in,
            out_specs=all_out, scratch_shapes=scratch_shapes))
    else:
        spec = dict(grid=grid, in_specs=all_in, out_specs=all_out, scratch_shapes=scratch_shapes)
    outs = pl.pallas_call(
        kernel_fn,
        out_shape=list(out_shape) + r_shapes,
        compiler_params=_params(*semantics),
        name=name,
        **spec,
    )(*args, *[r.src for r in riders])
    return outs[:n_out], outs[n_out:]


def _gelu_tanh(x):
    cdf = 0.5 * (1.0 + jnp.tanh(0.7978845608028654 * (x + 0.044715 * (x * x * x))))
    return x * cdf


def _row_chunks(ref, chunk=EPILOGUE_ROWS):
    n = ref.shape[0] // chunk
    return [(c, n, slice(c * chunk, (c + 1) * chunk)) for c in range(n)]


def _proj_body(cast_part, x_ref, g_ref, w_ref, o_ref, xn_ref):
    @pl.when(pl.program_id(1) == 0)
    def _():
        _norm_rows_into(x_ref, g_ref, xn_ref)
    for c, n, rows in _row_chunks(xn_ref, 4 * EPILOGUE_ROWS):
        o_ref[rows, :] = jnp.dot(xn_ref[rows, :], w_ref[...],
                                 preferred_element_type=F32).astype(o_ref.dtype)
        cast_part(c, n)


def _norm_proj(x, gain, w, riders, *, tm, tn, name):
    m, k = x.shape
    n = w.shape[1]
    (out,), casts = _hosted_call(
        _proj_body,
        grid=(m // tm, n // tn),
        in_specs=[
            pl.BlockSpec((tm, k), lambda i, j: (i, 0)),
            pl.BlockSpec((1, k), lambda i, j: (0, 0)),
            pl.BlockSpec((k, tn), lambda i, j: (0, j)),
        ],
        out_specs=[pl.BlockSpec((tm, tn), lambda i, j: (i, j))],
        out_shape=[jax.ShapeDtypeStruct((m, n), BF16)],
        scratch_shapes=[pltpu.VMEM((tm, k), BF16)],
        args=(x, gain.reshape(1, k), w),
        riders=riders,
        semantics=("arbitrary", "arbitrary"),
        name=name,
    )
    return out, casts


def _gmlp_in_body(cast_part, x_ref, g_ref, w_ref, b_ref, z_ref, ssq_ref, xn_ref, ss_ref):
    j = pl.program_id(1)
    nj = pl.num_programs(1)

    @pl.when(j == 0)
    def _():
        _norm_rows_into(x_ref, g_ref, xn_ref)
        ss_ref[...] = jnp.zeros_like(ss_ref)

    is_v = (j >= nj // 2).astype(F32)
    for c, n, rows in _row_chunks(xn_ref, 2 * EPILOGUE_ROWS):
        z = jnp.dot(xn_ref[rows, :], w_ref[...], preferred_element_type=F32) + b_ref[...]
        z = _gelu_tanh(z)
        z_ref[rows, :] = z.astype(z_ref.dtype)
        ss_ref[rows, :] += is_v * jnp.sum(z * z, axis=-1, keepdims=True)
        cast_part(c, n)

    @pl.when(j == nj - 1)
    def _():
        ssq_ref[...] = jnp.broadcast_to(ss_ref[...], ssq_ref.shape)


def _gmlp_in(x, gain, w_in, b_in, later_weights, *, tm=2048, tn=512):
    m, k = x.shape
    n = w_in.shape[1]
    grid = (m // tm, n // tn)
    return _hosted_call(
        _gmlp_in_body,
        grid=grid,
        in_specs=[
            pl.BlockSpec((tm, k), lambda i, j: (i, 0), pipeline_mode=pl.Buffered(1)),
            pl.BlockSpec((1, k), lambda i, j: (0, 0)),
            pl.BlockSpec((k, tn), lambda i, j: (0, j)),
            pl.BlockSpec((1, tn), lambda i, j: (0, j)),
        ],
        out_specs=[
            pl.BlockSpec((tm, tn), lambda i, j: (i, j)),
            pl.BlockSpec((tm, LANES), lambda i, j: (i, 0)),
        ],
        out_shape=[
            jax.ShapeDtypeStruct((m, n), BF16),
            jax.ShapeDtypeStruct((m, LANES), F32),
        ],
        scratch_shapes=[pltpu.VMEM((tm, k), BF16), pltpu.VMEM((tm, 1), F32)],
        args=(x, gain.reshape(1, k), w_in, b_in.reshape(1, n)),
        riders=[_whole_rider(w, grid[0] * grid[1]) for w in later_weights],
        semantics=("arbitrary", "arbitrary"),
        name="gmlp_in",
    )


def _gmlp_out_body(cast_part, u_ref, v_ref, ssq_ref, gv_ref, ws_ref, bs_ref, x_ref, g_ref, wo_ref,
                   o_ref):
    grp = pl.program_id(1)
    tm = u_ref.shape[0]
    acc_ref = o_ref

    @pl.when(grp == 0)
    def _():
        acc_ref[...] = jnp.zeros_like(acc_ref)

    row = lax.broadcasted_iota(jnp.int32, (CHUNK, CHUNK), 0)
    col = lax.broadcasted_iota(jnp.int32, (CHUNK, CHUNK), 1)
    ws = jnp.where(row >= col, ws_ref[0], 0.0).astype(BF16)
    r = lax.rsqrt(ssq_ref[:, 0:1] * (1.0 / GMLP_HALF) + EPS)
    def project(rows, gated):
        acc_ref[rows, :] += jnp.dot(gated, wo_ref[...], preferred_element_type=F32)

    pending = None
    n = tm // CHUNK
    for c in range(n):
        rows = slice(c * CHUNK, (c + 1) * CHUNK)
        vn = (v_ref[rows, :].astype(F32) * r[rows, :]) * gv_ref[...]
        mixed = jnp.dot(ws, vn.astype(BF16), preferred_element_type=F32) + bs_ref[0]
        gated = (u_ref[rows, :].astype(F32) * mixed).astype(BF16)
        if pending is not None:
            project(*pending)
        pending = (rows, gated)
        cast_part(c, n)
    project(*pending)

    @pl.when(grp == pl.num_programs(1) - 1)
    def _():
        _residual_norm_rows(x_ref, acc_ref, g_ref, o_ref)


def _gmlp_out(z, ssq, norm_v, w_s, b_s, x, gain, w_out, riders, *, tm=1024):
    m, d = x.shape
    gc = GMLP_GROUP_CH
    (out,), casts = _hosted_call(
        _gmlp_out_body,
        grid=(m // tm, GMLP_GROUPS),
        in_specs=[
            pl.BlockSpec((tm, gc), lambda i, g: (i, g)),
            pl.BlockSpec((tm, gc), lambda i, g: (i, GMLP_GROUPS + g)),
            pl.BlockSpec((tm, LANES), lambda i, g: (i, 0)),
            pl.BlockSpec((1, gc), lambda i, g: (0, g)),
            pl.BlockSpec((1, CHUNK, CHUNK), lambda i, g: (g, 0, 0)),
            pl.BlockSpec((1, CHUNK, 1), lambda i, g: (g, 0, 0)),
            pl.BlockSpec((tm, d), lambda i, g: (i, 0)),
            pl.BlockSpec((1, d), lambda i, g: (0, 0)),
            pl.BlockSpec((gc, d), lambda i, g: (g, 0)),
        ],
        out_specs=[pl.BlockSpec((tm, d), lambda i, g: (i, 0))],
        out_shape=[jax.ShapeDtypeStruct((m, d), F32)],
        scratch_shapes=[],
        args=(z, z, ssq, norm_v.reshape(1, GMLP_HALF), w_s, b_s[:, :, None], x,
              gain.reshape(1, d), w_out),
        riders=riders,
        semantics=("arbitrary", "arbitrary"),
        name="gmlp_out",
    )
    return out, casts


SWIGLU_ROWS = 256


def _chunks_covering(n_rows):
    assert n_rows % EPILOGUE_ROWS == 0
    bounds = list(range(0, n_rows, SWIGLU_ROWS)) + [n_rows]
    return [slice(lo, hi) for lo, hi in zip(bounds[:-1], bounds[1:])]


def _swiglu_accumulate(xn_ref, wg, wu, wd, acc_ref, cast_part=None, n_rows=None):
    chunks = _chunks_covering(n_rows or xn_ref.shape[0])

    def down(rows, a):
        acc_ref[rows, :] += jnp.dot(a, wd(), preferred_element_type=F32)

    pending = None
    for c, rows in enumerate(chunks):
        x = xn_ref[rows, :]
        g = jnp.dot(x, wg(), preferred_element_type=F32)
        u = jnp.dot(x, wu(), preferred_element_type=F32)
        a = ((g * jax.nn.sigmoid(g)) * u).astype(BF16)
        if pending is not None:
            down(*pending)
        pending = (rows, a)
        if cast_part is not None:
            cast_part(c, len(chunks))
    down(*pending)


def _cast_once(src_ref, dst_ref):
    done = []

    def get():
        if not done:
            dst_ref[...] = src_ref[0].astype(BF16)
            done.append(True)
        return dst_ref[...]
    return get


def _ffn_body(cast_part, x_ref, g_in_ref, wg_ref, wu_ref, wd_ref, g_out_ref, o_ref, xn_ref):
    j = pl.program_id(1)
    acc_ref = o_ref

    @pl.when(j == 0)
    def _():
        _norm_rows_into(x_ref, g_in_ref, xn_ref)
        acc_ref[...] = jnp.zeros_like(acc_ref)

    _swiglu_accumulate(xn_ref, lambda: wg_ref[...], lambda: wu_ref[...], lambda: wd_ref[...],
                       acc_ref, cast_part)

    @pl.when(j == pl.num_programs(1) - 1)
    def _():
        _residual_norm_rows(x_ref, acc_ref, g_out_ref, o_ref)


def _ffn_dense(x, g_in, w_gu, w_down, g_out, riders, *, tm=1024, tf=1024):
    m, d = x.shape
    nj = D_FF // tf
    once = dict(pipeline_mode=pl.Buffered(1))
    (out,), casts = _hosted_call(
        _ffn_body,
        grid=(m // tm, nj),
        in_specs=[
            pl.BlockSpec((tm, d), lambda i, j: (i, 0), **once),
            pl.BlockSpec((1, d), lambda i, j: (0, 0)),
            pl.BlockSpec((d, tf), lambda i, j: (0, j)),
            pl.BlockSpec((d, tf), lambda i, j: (0, nj + j)),
            pl.BlockSpec((tf, d), lambda i, j: (j, 0)),
            pl.BlockSpec((1, d), lambda i, j: (0, 0)),
        ],
        out_specs=[pl.BlockSpec((tm, d), lambda i, j: (i, 0), **once)],
        out_shape=[jax.ShapeDtypeStruct((m, d), F32)],
        scratch_shapes=[pltpu.VMEM((tm, d), BF16)],
        args=(x, g_in.reshape(1, d), w_gu, w_gu, w_down, g_out.reshape(1, d)),
        riders=riders,
        semantics=("arbitrary", "arbitrary"),
        name="ffn_dense",
    )
    return out, casts


SUB = 4
ATTN_UNROLL = 8


def _deinterleave(src_ref, dst_ref, span):
    part = span // SUB
    for base in range(0, src_ref.shape[0], span):
        for r in range(SUB):
            dst_ref[base + r * part: base + (r + 1) * part, :] = (
                src_ref[pl.ds(base + r, part, stride=SUB), :].astype(dst_ref.dtype))


def _interleave(src_ref, dst_ref, span):
    part = span // SUB
    for base in range(0, src_ref.shape[0], span):
        for r in range(SUB):
            dst_ref[pl.ds(base + r, part, stride=SUB), :] = (
                src_ref[base + r * part: base + (r + 1) * part, :])


def _window_attention(q_ref, k_ref, v_ref, slope, dilation, blocks_per_seq,
                      bias_c_ref, bias_p_ref, pc_ref, pp_ref, den_ref, o_ref, lse_ref):
    jq = lax.broadcasted_iota(jnp.int32, (BLOCK, BLOCK), 0)
    kk = lax.broadcasted_iota(jnp.int32, (BLOCK, BLOCK), 1)
    dist_c = ((jq - kk) * dilation).astype(F32)
    dist_p = ((BLOCK + jq - kk) * dilation).astype(F32)
    bias_c_ref[...] = jnp.where(kk <= jq, -(slope * dist_c), NEG)
    bias_p_ref[...] = jnp.where(kk >= jq, -(slope * dist_p), NEG)
    scale = HEAD_DIM ** -0.5
    contract_last = (((1,), (1,)), ((), ()))
    with_prev = blocks_per_seq > 1

    def block_rows(n):
        rows = pl.ds(pl.multiple_of(n * BLOCK, BLOCK), BLOCK)
        prev = pl.ds(pl.multiple_of(jnp.maximum(n - 1, 0) * BLOCK, BLOCK), BLOCK)
        return rows, prev

    def probabilities(n, carry):
        rows, prev = block_rows(n)
        q = q_ref[rows, :]
        s_c = lax.dot_general(q, k_ref[rows, :], contract_last, preferred_element_type=F32)
        s_c = s_c * scale + bias_c_ref[...]
        if with_prev:
            has_prev = (n % blocks_per_seq) != 0
            s_p = lax.dot_general(q, k_ref[prev, :], contract_last, preferred_element_type=F32)
            s_p = jnp.where(has_prev, s_p * scale + bias_p_ref[...], NEG)
            m = jnp.max(jnp.maximum(s_c, s_p), axis=-1, keepdims=True)
            p_c = jnp.exp(s_c - m)
            p_p = jnp.exp(s_p - m)
            den = jnp.sum(p_c + p_p, axis=-1, keepdims=True)
            pp_ref[rows, :] = p_p.astype(BF16)
        else:
            m = jnp.max(s_c, axis=-1, keepdims=True)
            p_c = jnp.exp(s_c - m)
            den = jnp.sum(p_c, axis=-1, keepdims=True)
        pc_ref[rows, :] = p_c.astype(BF16)
        den_ref[rows, :] = den
        lse_ref[rows, :] = jnp.broadcast_to(m + jnp.log(den), (BLOCK, LANES))
        return carry

    def values(n, carry):
        rows, prev = block_rows(n)
        o = jnp.dot(pc_ref[rows, :], v_ref[rows, :], preferred_element_type=F32)
        if with_prev:
            o = o + jnp.dot(pp_ref[rows, :], v_ref[prev, :], preferred_element_type=F32)
        o_ref[rows, :] = o / den_ref[rows, :]
        return carry

    n_blocks = q_ref.shape[0] // BLOCK
    lax.fori_loop(0, n_blocks, probabilities, 0, unroll=ATTN_UNROLL)
    lax.fori_loop(0, n_blocks, values, 0, unroll=ATTN_UNROLL)


def _attn_body(cast_part, base_ref, q1_ref, q4_ref, q16_ref, k_ref, v_ref, out_ref,
               stage_ref, tmp_ref, k4f_ref, v4f_ref,
               qp_ref, k4_ref, v4_ref, k16_ref, v16_ref,
               o1_ref, l1_ref, o4_ref, l4_ref, o16_ref, l16_ref, op_ref, lp_ref,
               bias_c_ref, bias_p_ref, pc_ref, pp_ref, den_ref):
    seq = k_ref.shape[0]
    base = base_ref[pl.program_id(1)]
    (_, d1), (_, d4), (_, d16) = DILATED_GROUPS
    cast_part(0, 1)

    stage_ref[...] = k_ref[...].astype(F32)
    _deinterleave(stage_ref, k4f_ref, seq)
    k4_ref[...] = k4f_ref[...].astype(BF16)
    _deinterleave(k4f_ref, k16_ref, seq // SUB)
    stage_ref[...] = v_ref[...].astype(F32)
    _deinterleave(stage_ref, v4f_ref, seq)
    v4_ref[...] = v4f_ref[...].astype(BF16)
    _deinterleave(v4f_ref, v16_ref, seq // SUB)

    _window_attention(q1_ref, k_ref, v_ref, base / d1, d1, seq // BLOCK,
                      bias_c_ref, bias_p_ref, pc_ref, pp_ref, den_ref, o1_ref, l1_ref)

    stage_ref[...] = q4_ref[...].astype(F32)
    _deinterleave(stage_ref, qp_ref, seq)
    _window_attention(qp_ref, k4_ref, v4_ref, base / d4, d4, seq // d4 // BLOCK,
                      bias_c_ref, bias_p_ref, pc_ref, pp_ref, den_ref, op_ref, lp_ref)
    _interleave(op_ref, o4_ref, seq)
    _interleave(lp_ref, l4_ref, seq)

    stage_ref[...] = q16_ref[...].astype(F32)
    _deinterleave(stage_ref, tmp_ref, seq)
    _deinterleave(tmp_ref, qp_ref, seq // SUB)
    _window_attention(qp_ref, k16_ref, v16_ref, base / d16, d16, seq // d16 // BLOCK,
                      bias_c_ref, bias_p_ref, pc_ref, pp_ref, den_ref, op_ref, lp_ref)
    _interleave(op_ref, tmp_ref, seq // SUB)
    _interleave(tmp_ref, o16_ref, seq)
    _interleave(lp_ref, tmp_ref, seq // SUB)
    _interleave(tmp_ref, l16_ref, seq)

    chunk = 2 * BLOCK

    def merge(c, carry):
        rows = pl.ds(pl.multiple_of(c * chunk, chunk), chunk)
        la, lb, lc = l1_ref[rows, :], l4_ref[rows, :], l16_ref[rows, :]
        mx = jnp.maximum(jnp.maximum(la, lb), lc)
        ea, eb, ec = jnp.exp(la - mx), jnp.exp(lb - mx), jnp.exp(lc - mx)
        den = ea + eb + ec
        o = (ea / den) * o1_ref[rows, :] + (eb / den) * o4_ref[rows, :] + (ec / den) * o16_ref[rows, :]
        out_ref[rows, :] = o.astype(out_ref.dtype)
        return carry
    lax.fori_loop(0, seq // chunk, merge, 0)


def _attention(base, q, kv, batch, seq, riders):
    for window, dilation in DILATED_GROUPS:
        assert window // dilation == BLOCK, "keys per query must span exactly one previous block"
    assert [d for _, d in DILATED_GROUPS] == [1, SUB, SUB * SUB]
    assert seq % (SUB * SUB * BLOCK) == 0
    h = N_KV_HEADS
    blk = (seq, HEAD_DIM)
    f32buf = pltpu.VMEM((seq, HEAD_DIM), F32)
    bf16buf = pltpu.VMEM((seq, HEAD_DIM), BF16)
    (out,), casts = _hosted_call(
        _attn_body,
        num_scalar_prefetch=1,
        grid=(batch, h),
        in_specs=[
            pl.BlockSpec(blk, lambda b, hh, base: (b, hh)),
            pl.BlockSpec(blk, lambda b, hh, base: (b, h + hh)),
            pl.BlockSpec(blk, lambda b, hh, base: (b, 2 * h + hh)),
            pl.BlockSpec(blk, lambda b, hh, base: (b, hh)),
            pl.BlockSpec(blk, lambda b, hh, base: (b, h + hh)),
        ],
        out_specs=[pl.BlockSpec(blk, lambda b, hh, base: (b, hh))],
        out_shape=[jax.ShapeDtypeStruct((batch * seq, D_MODEL), BF16)],
        scratch_shapes=[f32buf] * 4 + [bf16buf] * 5 + [f32buf] * 8
                       + [pltpu.VMEM((BLOCK, BLOCK), F32)] * 2
                       + [bf16buf] * 2 + [pltpu.VMEM((seq, 1), F32)],
        args=(base, q, q, q, kv, kv),
        riders=riders,
        semantics=("arbitrary", "arbitrary"),
        name="attn",
    )
    return out, casts


def _attn_out_body(o_ref, x_ref, g_ref, wo_ref, out_ref, acc_ref):
    acc_ref[...] = jnp.dot(o_ref[...], wo_ref[...], preferred_element_type=F32)
    _residual_norm_rows(x_ref, acc_ref, g_ref, out_ref)


def _attn_out(o, x, gain, w_o, *, tm=512):
    m, d = x.shape
    row_blk = pl.BlockSpec((tm, d), lambda i: (i, 0))
    return pl.pallas_call(
        _attn_out_body,
        grid=(m // tm,),
        in_specs=[row_blk, row_blk,
                  pl.BlockSpec((1, d), lambda i: (0, 0)),
                  pl.BlockSpec((d, d), lambda i: (0, 0))],
        out_specs=row_blk,
        out_shape=jax.ShapeDtypeStruct((m, d), F32),
        scratch_shapes=[pltpu.VMEM((tm, d), F32)],
        compiler_params=_params("parallel"),
        name="attn_out",
    )(o, x, gain.reshape(1, d), w_o)


def _route_body(x_ref, g_ref, wr_ref, o_ref, xn_ref):
    _norm_rows_into(x_ref, g_ref, xn_ref)
    logits = jnp.dot(xn_ref[...], wr_ref[...], preferred_element_type=F32,
                     precision=lax.Precision.HIGHEST)
    lane = lax.broadcasted_iota(jnp.int32, logits.shape, 1)
    logits = jnp.where(lane < N_EXPERTS, logits, -jnp.inf)
    m1 = jnp.max(logits, axis=-1, keepdims=True)
    i1 = jnp.min(jnp.where(logits == m1, lane, LANES), axis=-1, keepdims=True)
    rest = jnp.where(lane == i1, -jnp.inf, logits)
    m2 = jnp.max(rest, axis=-1, keepdims=True)
    i2 = jnp.min(jnp.where(rest == m2, lane, LANES), axis=-1, keepdims=True)
    t = jnp.exp(m2 - m1)
    den = 1.0 + t
    out = jnp.where(lane == 0, i1.astype(F32), 0.0)
    out = jnp.where(lane == 1, i2.astype(F32), out)
    out = jnp.where(lane == 2, 1.0 / den, out)
    out = jnp.where(lane == 3, t / den, out)
    o_ref[...] = out


def _moe_route(x, gain, w_router, *, tm=512):
    m, d = x.shape
    wr = jnp.zeros((d, LANES), F32).at[:, :N_EXPERTS].set(w_router)
    return pl.pallas_call(
        _route_body,
        grid=(m // tm,),
        in_specs=[pl.BlockSpec((tm, d), lambda i: (i, 0)),
                  pl.BlockSpec((1, d), lambda i: (0, 0)),
                  pl.BlockSpec((d, LANES), lambda i: (0, 0))],
        out_specs=pl.BlockSpec((tm, LANES), lambda i: (i, 0)),
        out_shape=jax.ShapeDtypeStruct((m, LANES), F32),
        scratch_shapes=[pltpu.VMEM((tm, d), F32)],
        compiler_params=_params("parallel"),
        name="moe_route",
    )(x, gain.reshape(1, d), wr)


def _row_copy(src_hbm, row, dst_ref, r, sem):
    return pltpu.make_async_copy(src_hbm.at[pl.ds(row, 1), :], dst_ref.at[pl.ds(r, 1), :], sem)


GATHER_UNROLL = 8
GATHER_QUEUE = 1
GATHER_SLICES = 12


def _start_row_gather(idx_ref, base, src_hbm, dst_ref, sem, priority, first=0, count=None):
    count = dst_ref.shape[0] if count is None else count

    def start(k, carry):
        r = first + k
        _row_copy(src_hbm, idx_ref[base + r], dst_ref, r, sem).start(priority=priority)
        return carry
    lax.fori_loop(0, count, start, 0, unroll=GATHER_UNROLL)


def _wait_row_gather(idx_ref, base, src_hbm, dst_ref, sem):
    def wait(r, carry):
        _row_copy(src_hbm, idx_ref[base + r], dst_ref, r, sem).wait()
        return carry
    lax.fori_loop(0, dst_ref.shape[0], wait, 0, unroll=GATHER_UNROLL)


def _experts_body(tok_ref, bexp_ref, rows_ref, nused_ref, hn_hbm, g_ref, wg_ref, wu_ref, wd_ref,
                  ys_ref, xg_ref, xn_ref, wg16_ref, wu16_ref, wd16_ref, sem):
    i = pl.program_id(0)
    j = pl.program_id(1)
    n_used = nused_ref[0]
    used = i < n_used
    acc_ref = ys_ref

    @pl.when(j == 0)
    def _():
        acc_ref[...] = jnp.zeros_like(acc_ref)

    @pl.when(jnp.logical_and(used, j == 0))
    def _():
        @pl.when(i == 0)
        def _():
            _start_row_gather(tok_ref, 0, hn_hbm, xg_ref, sem, GATHER_QUEUE)
        _wait_row_gather(tok_ref, i * MOE_TM, hn_hbm, xg_ref, sem)
        _norm_rows_into(xg_ref, g_ref, xn_ref)

    per_step = MOE_TM // GATHER_SLICES
    assert GATHER_SLICES < D_FF // MOE_TF and per_step * GATHER_SLICES == MOE_TM

    @pl.when(jnp.logical_and(i + 1 < n_used, jnp.logical_and(j >= 1, j <= GATHER_SLICES)))
    def _():
        _start_row_gather(tok_ref, (i + 1) * MOE_TM, hn_hbm, xg_ref, sem, GATHER_QUEUE,
                          first=(j - 1) * per_step, count=per_step)

    n_rows = rows_ref[i]
    variants = [r for r in (128, 256, 512, 1024, 2048) if r < MOE_TM] + [MOE_TM]
    lower = 0
    for upper in variants:
        fits = jnp.logical_and(n_rows > lower, n_rows <= upper)

        @pl.when(jnp.logical_and(used, fits))
        def _(upper=upper):
            _swiglu_accumulate(xn_ref, _cast_once(wg_ref, wg16_ref), _cast_once(wu_ref, wu16_ref),
                               _cast_once(wd_ref, wd16_ref), acc_ref, n_rows=upper)
        lower = upper


def _moe_experts(tok_pad, blk_expert, blk_rows, n_used, h, gain, w_gu, w_down):
    d = D_MODEL
    n_blocks = tok_pad.shape[0] // MOE_TM
    nj = D_FF // MOE_TF

    def col(i, j, nused):
        return jnp.where(i < nused[0], j, nj - 1)

    return pl.pallas_call(
        _experts_body,
        grid_spec=pltpu.PrefetchScalarGridSpec(
            num_scalar_prefetch=4,
            grid=(n_blocks, nj),
            in_specs=[
                pl.BlockSpec(memory_space=pl.ANY),
                pl.BlockSpec((1, d), lambda i, j, tok, be, br, nu: (0, 0)),
                pl.BlockSpec((1, d, MOE_TF), lambda i, j, tok, be, br, nu: (be[i], 0, col(i, j, nu))),
                pl.BlockSpec((1, d, MOE_TF),
                             lambda i, j, tok, be, br, nu: (be[i], 0, nj + col(i, j, nu))),
                pl.BlockSpec((1, MOE_TF, d), lambda i, j, tok, be, br, nu: (be[i], col(i, j, nu), 0)),
            ],
            out_specs=pl.BlockSpec((MOE_TM, d), lambda i, j, tok, be, br, nu: (i, 0),
                                   pipeline_mode=pl.Buffered(1)),
            scratch_shapes=[pltpu.VMEM((MOE_TM, d), F32), pltpu.VMEM((MOE_TM, d), BF16),
                            pltpu.VMEM((d, MOE_TF), BF16), pltpu.VMEM((d, MOE_TF), BF16),
                            pltpu.VMEM((MOE_TF, d), BF16), pltpu.SemaphoreType.DMA],
        ),
        out_shape=jax.ShapeDtypeStruct((n_blocks * MOE_TM, d), F32),
        compiler_params=_params("arbitrary", "arbitrary"),
        name="moe_experts",
    )(tok_pad, blk_expert, blk_rows, n_used, h, gain.reshape(1, d), w_gu, w_gu, w_down)


def _combine_body(p0_ref, p1_ref, ys_hbm, route_ref, x_ref, g_ref, o_ref, a_ref, b_ref, sem):
    tm = x_ref.shape[0]
    base = pl.program_id(0) * tm
    _start_row_gather(p0_ref, base, ys_hbm, a_ref, sem.at[0], 0)
    _start_row_gather(p1_ref, base, ys_hbm, b_ref, sem.at[1], 1)
    _wait_row_gather(p0_ref, base, ys_hbm, a_ref, sem.at[0])
    _wait_row_gather(p1_ref, base, ys_hbm, b_ref, sem.at[1])
    gate0 = route_ref[:, TOP_K:TOP_K + 1]
    gate1 = route_ref[:, TOP_K + 1:TOP_K + 2]
    a_ref[...] = a_ref[...] * gate0 + b_ref[...] * gate1
    _residual_norm_rows(x_ref, a_ref, g_ref, o_ref)


def _moe_combine(pos0, pos1, ys, route, x, gain, *, tm=512):
    m, d = x.shape
    return pl.pallas_call(
        _combine_body,
        grid_spec=pltpu.PrefetchScalarGridSpec(
            num_scalar_prefetch=2,
            grid=(m // tm,),
            in_specs=[
                pl.BlockSpec(memory_space=pl.ANY),
                pl.BlockSpec((tm, LANES), lambda i, p0, p1: (i, 0)),
                pl.BlockSpec((tm, d), lambda i, p0, p1: (i, 0)),
                pl.BlockSpec((1, d), lambda i, p0, p1: (0, 0)),
            ],
            out_specs=pl.BlockSpec((tm, d), lambda i, p0, p1: (i, 0)),
            scratch_shapes=[pltpu.VMEM((tm, d), F32), pltpu.VMEM((tm, d), F32),
                            pltpu.SemaphoreType.DMA((2,))],
        ),
        out_shape=jax.ShapeDtypeStruct((m, d), F32),
        compiler_params=_params("arbitrary"),
        name="moe_combine",
    )(pos0, pos1, ys, route, x, gain.reshape(1, d))


def _dispatch_plan(route):
    n = route.shape[0]
    experts = route[:, :TOP_K].astype(jnp.int32).reshape(-1)
    onehot = (experts[:, None] == jnp.arange(N_EXPERTS)[None, :]).astype(jnp.int32)
    rank = jnp.take_along_axis(jnp.cumsum(onehot, axis=0) - onehot, experts[:, None], axis=1)[:, 0]
    counts = jnp.sum(onehot, axis=0)
    blocks_per_expert = (counts + MOE_TM - 1) // MOE_TM
    block_end = jnp.cumsum(blocks_per_expert)
    block_start = block_end - blocks_per_expert
    dest = block_start[experts] * MOE_TM + rank
    n_blocks = (n * TOP_K) // MOE_TM + N_EXPERTS
    tok_pad = jnp.zeros((n_blocks * MOE_TM,), jnp.int32).at[dest].set(jnp.arange(n * TOP_K) // TOP_K)
    blocks = jnp.arange(n_blocks)
    blk_expert = jnp.clip(jnp.searchsorted(block_end, blocks, side='right'),
                          0, N_EXPERTS - 1).astype(jnp.int32)
    blk_rows = jnp.clip(counts[blk_expert] - (blocks - block_start[blk_expert]) * MOE_TM,
                        0, MOE_TM).astype(jnp.int32)
    n_used = block_end[-1].astype(jnp.int32)
    blk_expert = jnp.where(blocks < n_used, blk_expert, blk_expert[jnp.maximum(n_used - 1, 0)])
    pos = dest.reshape(n, TOP_K)
    return tok_pad, blk_expert, blk_rows, n_used.reshape(1), pos[:, 0], pos[:, 1]


def kernel(x, norm_gains, a_w_in, a_b_in, a_norm_v, a_w_s, a_b_s, a_w_out, kv_norm, w_kv,
           b_w_q, b_w_o, ffn_w_gu, ffn_w_down, moe_router, moe_w_gu, moe_w_down):
    batch, seq, d = x.shape
    h = x.reshape(batch * seq, d)
    bf = lambda w: w.astype(BF16)

    g = norm_gains[0]
    (z, ssq), (w_out, w_gu, w_down) = _gmlp_in(
        h, g[0], bf(a_w_in[0]), a_b_in[0], [a_w_out[0], ffn_w_gu[0], ffn_w_down[0]])
    h, (w_kv16, w_q) = _gmlp_out(z, ssq, a_norm_v[0], a_w_s[0], a_b_s[0], h, g[1], w_out,
                                 [_whole_rider(w, 64) for w in (w_kv, b_w_q[0])])
    h, (w_o,) = _ffn_dense(h, g[2], w_gu, w_down, g[3], [_whole_rider(b_w_o[0], 56)])

    g = norm_gains[1]
    kv, _ = _norm_proj(h, kv_norm, w_kv16, [], tm=1024, tn=1024, name="proj_kv")
    q, _ = _norm_proj(h, g[0], w_q, [], tm=1024, tn=1024, name="proj_q")
    base = jnp.exp2(-8.0 * jnp.arange(1, N_KV_HEADS + 1, dtype=F32) / N_KV_HEADS)
    o, _ = _attention(base, q, kv, batch, seq, [])
    h = _attn_out(o, h, g[1], w_o)

    route = _moe_route(h, g[2], moe_router[0])
    tok_pad, blk_expert, blk_rows, n_used, pos0, pos1 = _dispatch_plan(route)
    ys = _moe_experts(tok_pad, blk_expert, blk_rows, n_used, h, g[2], moe_w_gu[0], moe_w_down[0])
    h = _moe_combine(pos0, pos1, ys, route, h, g[3])
    return h.reshape(batch, seq, d)
```

```python
import functools
from typing import NamedTuple

import jax
import jax.numpy as jnp
from jax import lax
from jax.experimental import pallas as pl
from jax.experimental.pallas import tpu as pltpu

F32 = jnp.float32
BF16 = jnp.bfloat16

D_MODEL = 2048
EPS = 1e-6
CHUNK = 128
GMLP_HALF = 2 * D_MODEL
GMLP_GROUPS = 8
GMLP_GROUP_CH = GMLP_HALF // GMLP_GROUPS
HEAD_DIM = 128
N_KV_HEADS = D_MODEL // HEAD_DIM
DILATED_GROUPS = ((128, 1), (512, 4), (2048, 16))
BLOCK = 128
NEG = -1e30
D_FF = 7168
N_EXPERTS = 8
TOP_K = 2

LANES = 128
VMEM_LIMIT_BYTES = 56 * 1024 * 1024

NORM_ROWS = 128
EPILOGUE_ROWS = 128
MOE_TM = 1152
MOE_TF = 512


def _params(*sem):
    return pltpu.CompilerParams(dimension_semantics=sem, vmem_limit_bytes=VMEM_LIMIT_BYTES)


def _rms_scale(x):
    return lax.rsqrt(jnp.mean(x * x, axis=-1, keepdims=True) + EPS)


def _norm_rows_into(x_ref, g_ref, out_ref):
    def body(c, carry):
        rows = pl.ds(pl.multiple_of(c * NORM_ROWS, NORM_ROWS), NORM_ROWS)
        x = x_ref[rows, :]
        out_ref[rows, :] = ((x * _rms_scale(x)) * g_ref[...]).astype(out_ref.dtype)
        return carry
    lax.fori_loop(0, x_ref.shape[0] // NORM_ROWS, body, 0)


def _residual_norm_rows(res_ref, acc_ref, g_ref, out_ref):
    def body(c, carry):
        rows = pl.ds(pl.multiple_of(c * NORM_ROWS, NORM_ROWS), NORM_ROWS)
        a = acc_ref[rows, :]
        out_ref[rows, :] = res_ref[rows, :] + (a * _rms_scale(a)) * g_ref[...]
        return carry
    lax.fori_loop(0, acc_ref.shape[0] // NORM_ROWS, body, 0)


BF16_SUBLANES = 16


class _Rider(NamedTuple):
    src: jax.Array
    steps: int
    rows: int


def _whole_rider(w, host_steps):
    total = w.shape[0]
    for steps in range(host_steps, 0, -1):
        if total % steps == 0 and (total // steps) % BF16_SUBLANES == 0:
            return _Rider(w, steps, total // steps)
    raise ValueError(f"no chunking of {w.shape} over {host_steps} steps")


def _riding_body(body, n_in, n_out, n_riders, *refs):
    ins, refs = refs[:n_in], refs[n_in:]
    r_in, refs = refs[:n_riders], refs[n_riders:]
    outs, refs = refs[:n_out], refs[n_out:]
    r_out, scratch = refs[:n_riders], refs[n_riders:]

    def cast_part(part, n_parts):
        for src, dst in zip(r_in, r_out):
            lane_tiles = src.shape[1] // LANES
            assert lane_tiles * LANES == src.shape[1] and lane_tiles >= n_parts
            cols = slice(part * lane_tiles // n_parts * LANES,
                         (part + 1) * lane_tiles // n_parts * LANES)
            dst[:, cols] = src[:, cols].astype(BF16)
    body(cast_part, *ins, *outs, *scratch)


def _hosted_call(body, *, grid, in_specs, out_specs, out_shape, scratch_shapes, args, riders,
                 semantics, name, num_scalar_prefetch=0):
    n_grid = len(grid)
    strides = [1] * n_grid
    for ax in range(n_grid - 2, -1, -1):
        strides[ax] = strides[ax + 1] * grid[ax + 1]
    assert all(r.steps <= strides[0] * grid[0] for r in riders)

    r_specs, r_shapes = [], []
    for r in riders:
        def index(*g, r=r):
            step = sum(g[ax] * strides[ax] for ax in range(n_grid))
            return jnp.minimum(step, r.steps - 1), 0
        r_specs.append(pl.BlockSpec((r.rows, r.src.shape[1]), index))
        r_shapes.append(jax.ShapeDtypeStruct(r.src.shape, BF16))

    n_in = num_scalar_prefetch + len(in_specs)
    n_out = len(out_specs)
    all_in = list(in_specs) + r_specs
    all_out = list(out_specs) + r_specs
    kernel_fn = functools.partial(_riding_body, body, n_in, n_out, len(riders))
    if num_scalar_prefetch:
        spec = dict(grid_spec=pltpu.PrefetchScalarGridSpec(
            num_scalar_prefetch=num_scalar_prefetch, grid=grid, in_specs=all_in,
            out_specs=all_out, scratch_shapes=scratch_shapes))
    else:
        spec = dict(grid=grid, in_specs=all_in, out_specs=all_out, scratch_shapes=scratch_shapes)
    outs = pl.pallas_call(
        kernel_fn,
        out_shape=list(out_shape) + r_shapes,
        compiler_params=_params(*semantics),
        name=name,
        **spec,
    )(*args, *[r.src for r in riders])
    return outs[:n_out], outs[n_out:]


def _gelu_tanh(x):
    cdf = 0.5 * (1.0 + jnp.tanh(0.7978845608028654 * (x + 0.044715 * (x * x * x))))
    return x * cdf


def _row_chunks(ref, chunk=EPILOGUE_ROWS):
    n = ref.shape[0] // chunk
    return [(c, n, slice(c * chunk, (c + 1) * chunk)) for c in range(n)]


def _proj_body(cast_part, x_ref, g_ref, w_ref, o_ref, xn_ref):
    @pl.when(pl.program_id(1) == 0)
    def _():
        _norm_rows_into(x_ref, g_ref, xn_ref)
    for c, n, rows in _row_chunks(xn_ref, 4 * EPILOGUE_ROWS):
        o_ref[rows, :] = jnp.dot(xn_ref[rows, :], w_ref[...],
                                 preferred_element_type=F32).astype(o_ref.dtype)
        cast_part(c, n)


def _norm_proj(x, gain, w, riders, *, tm, tn, name):
    m, k = x.shape
    n = w.shape[1]
    (out,), casts = _hosted_call(
        _proj_body,
        grid=(m // tm, n // tn),
        in_specs=[
            pl.BlockSpec((tm, k), lambda i, j: (i, 0)),
            pl.BlockSpec((1, k), lambda i, j: (0, 0)),
            pl.BlockSpec((k, tn), lambda i, j: (0, j)),
        ],
        out_specs=[pl.BlockSpec((tm, tn), lambda i, j: (i, j))],
        out_shape=[jax.ShapeDtypeStruct((m, n), BF16)],
        scratch_shapes=[pltpu.VMEM((tm, k), BF16)],
        args=(x, gain.reshape(1, k), w),
        riders=riders,
        semantics=("arbitrary", "arbitrary"),
        name=name,
    )
    return out, casts


def _gmlp_in_body(cast_part, x_ref, g_ref, w_ref, b_ref, z_ref, ssq_ref, xn_ref, ss_ref):
    j = pl.program_id(1)
    nj = pl.num_programs(1)

    @pl.when(j == 0)
    def _():
        _norm_rows_into(x_ref, g_ref, xn_ref)
        ss_ref[...] = jnp.zeros_like(ss_ref)

    is_v = (j >= nj // 2).astype(F32)
    for c, n, rows in _row_chunks(xn_ref, 2 * EPILOGUE_ROWS):
        z = jnp.dot(xn_ref[rows, :], w_ref[...], preferred_element_type=F32) + b_ref[...]
        z = _gelu_tanh(z)
        z_ref[rows, :] = z.astype(z_ref.dtype)
        ss_ref[rows, :] += is_v * jnp.sum(z * z, axis=-1, keepdims=True)
        cast_part(c, n)

    @pl.when(j == nj - 1)
    def _():
        ssq_ref[...] = jnp.broadcast_to(ss_ref[...], ssq_ref.shape)


def _gmlp_in(x, gain, w_in, b_in, later_weights, *, tm=2048, tn=512):
    m, k = x.shape
    n = w_in.shape[1]
    grid = (m // tm, n // tn)
    return _hosted_call(
        _gmlp_in_body,
        grid=grid,
        in_specs=[
            pl.BlockSpec((tm, k), lambda i, j: (i, 0), pipeline_mode=pl.Buffered(1)),
            pl.BlockSpec((1, k), lambda i, j: (0, 0)),
            pl.BlockSpec((k, tn), lambda i, j: (0, j)),
            pl.BlockSpec((1, tn), lambda i, j: (0, j)),
        ],
        out_specs=[
            pl.BlockSpec((tm, tn), lambda i, j: (i, j)),
            pl.BlockSpec((tm, LANES), lambda i, j: (i, 0)),
        ],
        out_shape=[
            jax.ShapeDtypeStruct((m, n), BF16),
            jax.ShapeDtypeStruct((m, LANES), F32),
        ],
        scratch_shapes=[pltpu.VMEM((tm, k), BF16), pltpu.VMEM((tm, 1), F32)],
        args=(x, gain.reshape(1, k), w_in, b_in.reshape(1, n)),
        riders=[_whole_rider(w, grid[0] * grid[1]) for w in later_weights],
        semantics=("arbitrary", "arbitrary"),
        name="gmlp_in",
    )


def _gmlp_out_body(cast_part, u_ref, v_ref, ssq_ref, gv_ref, ws_ref, bs_ref, x_ref, g_ref, wo_ref,
                   o_ref):
    grp = pl.program_id(1)
    tm = u_ref.shape[0]
    acc_ref = o_ref

    @pl.when(grp == 0)
    def _():
        acc_ref[...] = jnp.zeros_like(acc_ref)

    row = lax.broadcasted_iota(jnp.int32, (CHUNK, CHUNK), 0)
    col = lax.broadcasted_iota(jnp.int32, (CHUNK, CHUNK), 1)
    ws = jnp.where(row >= col, ws_ref[0], 0.0).astype(BF16)
    r = lax.rsqrt(ssq_ref[:, 0:1] * (1.0 / GMLP_HALF) + EPS)
    def project(rows, gated):
        acc_ref[rows, :] += jnp.dot(gated, wo_ref[...], preferred_element_type=F32)

    pending = None
    n = tm // CHUNK
    for c in range(n):
        rows = slice(c * CHUNK, (c + 1) * CHUNK)
        vn = (v_ref[rows, :].astype(F32) * r[rows, :]) * gv_ref[...]
        mixed = jnp.dot(ws, vn.astype(BF16), preferred_element_type=F32) + bs_ref[0]
        gated = (u_ref[rows, :].astype(F32) * mixed).astype(BF16)
        if pending is not None:
            project(*pending)
        pending = (rows, gated)
        cast_part(c, n)
    project(*pending)

    @pl.when(grp == pl.num_programs(1) - 1)
    def _():
        _residual_norm_rows(x_ref, acc_ref, g_ref, o_ref)


def _gmlp_out(z, ssq, norm_v, w_s, b_s, x, gain, w_out, riders, *, tm=1024):
    m, d = x.shape
    gc = GMLP_GROUP_CH
    (out,), casts = _hosted_call(
        _gmlp_out_body,
        grid=(m // tm, GMLP_GROUPS),
        in_specs=[
            pl.BlockSpec((tm, gc), lambda i, g: (i, g)),
            pl.BlockSpec((tm, gc), lambda i, g: (i, GMLP_GROUPS + g)),
            pl.BlockSpec((tm, LANES), lambda i, g: (i, 0)),
            pl.BlockSpec((1, gc), lambda i, g: (0, g)),
            pl.BlockSpec((1, CHUNK, CHUNK), lambda i, g: (g, 0, 0)),
            pl.BlockSpec((1, CHUNK, 1), lambda i, g: (g, 0, 0)),
            pl.BlockSpec((tm, d), lambda i, g: (i, 0)),
            pl.BlockSpec((1, d), lambda i, g: (0, 0)),
            pl.BlockSpec((gc, d), lambda i, g: (g, 0)),
        ],
        out_specs=[pl.BlockSpec((tm, d), lambda i, g: (i, 0))],
        out_shape=[jax.ShapeDtypeStruct((m, d), F32)],
        scratch_shapes=[],
        args=(z, z, ssq, norm_v.reshape(1, GMLP_HALF), w_s, b_s[:, :, None], x,
              gain.reshape(1, d), w_out),
        riders=riders,
        semantics=("arbitrary", "arbitrary"),
        name="gmlp_out",
    )
    return out, casts


SWIGLU_ROWS = 256


def _chunks_covering(n_rows):
    assert n_rows % EPILOGUE_ROWS == 0
    bounds = list(range(0, n_rows, SWIGLU_ROWS)) + [n_rows]
    return [slice(lo, hi) for lo, hi in zip(bounds[:-1], bounds[1:])]


def _swiglu_accumulate(xn_ref, wg, wu, wd, acc_ref, cast_part=None, n_rows=None):
    chunks = _chunks_covering(n_rows or xn_ref.shape[0])

    def down(rows, a):
        acc_ref[rows, :] += jnp.dot(a, wd(), preferred_element_type=F32)

    pending = None
    for c, rows in enumerate(chunks):
        x = xn_ref[rows, :]
        g = jnp.dot(x, wg(), preferred_element_type=F32)
        u = jnp.dot(x, wu(), preferred_element_type=F32)
        a = ((g * jax.nn.sigmoid(g)) * u).astype(BF16)
        if pending is not None:
            down(*pending)
        pending = (rows, a)
        if cast_part is not None:
            cast_part(c, len(chunks))
    down(*pending)


def _cast_once(src_ref, dst_ref):
    done = []

    def get():
        if not done:
            dst_ref[...] = src_ref[0].astype(BF16)
            done.append(True)
        return dst_ref[...]
    return get


def _ffn_body(cast_part, x_ref, g_in_ref, wg_ref, wu_ref, wd_ref, g_out_ref, o_ref, xn_ref):
    j = pl.program_id(1)
    acc_ref = o_ref

    @pl.when(j == 0)
    def _():
        _norm_rows_into(x_ref, g_in_ref, xn_ref)
        acc_ref[...] = jnp.zeros_like(acc_ref)

    _swiglu_accumulate(xn_ref, lambda: wg_ref[...], lambda: wu_ref[...], lambda: wd_ref[...],
                       acc_ref, cast_part)

    @pl.when(j == pl.num_programs(1) - 1)
    def _():
        _residual_norm_rows(x_ref, acc_ref, g_out_ref, o_ref)


def _ffn_dense(x, g_in, w_gu, w_down, g_out, riders, *, tm=1024, tf=1024):
    m, d = x.shape
    nj = D_FF // tf
    once = dict(pipeline_mode=pl.Buffered(1))
    (out,), casts = _hosted_call(
        _ffn_body,
        grid=(m // tm, nj),
        in_specs=[
            pl.BlockSpec((tm, d), lambda i, j: (i, 0), **once),
            pl.BlockSpec((1, d), lambda i, j: (0, 0)),
            pl.BlockSpec((d, tf), lambda i, j: (0, j)),
            pl.BlockSpec((d, tf), lambda i, j: (0, nj + j)),
            pl.BlockSpec((tf, d), lambda i, j: (j, 0)),
            pl.BlockSpec((1, d), lambda i, j: (0, 0)),
        ],
        out_specs=[pl.BlockSpec((tm, d), lambda i, j: (i, 0), **once)],
        out_shape=[jax.ShapeDtypeStruct((m, d), F32)],
        scratch_shapes=[pltpu.VMEM((tm, d), BF16)],
        args=(x, g_in.reshape(1, d), w_gu, w_gu, w_down, g_out.reshape(1, d)),
        riders=riders,
        semantics=("arbitrary", "arbitrary"),
        name="ffn_dense",
    )
    return out, casts


SUB = 4
ATTN_UNROLL = 8


def _deinterleave(src_ref, dst_ref, span):
    part = span // SUB
    for base in range(0, src_ref.shape[0], span):
        for r in range(SUB):
            dst_ref[base + r * part: base + (r + 1) * part, :] = (
                src_ref[pl.ds(base + r, part, stride=SUB), :].astype(dst_ref.dtype))


def _interleave(src_ref, dst_ref, span):
    part = span // SUB
    for base in range(0, src_ref.shape[0], span):
        for r in range(SUB):
            dst_ref[pl.ds(base + r, part, stride=SUB), :] = (
                src_ref[base + r * part: base + (r + 1) * part, :])


def _window_attention(q_ref, k_ref, v_ref, slope, dilation, blocks_per_seq,
                      bias_c_ref, bias_p_ref, pc_ref, pp_ref, den_ref, o_ref, lse_ref):
    jq = lax.broadcasted_iota(jnp.int32, (BLOCK, BLOCK), 0)
    kk = lax.broadcasted_iota(jnp.int32, (BLOCK, BLOCK), 1)
    dist_c = ((jq - kk) * dilation).astype(F32)
    dist_p = ((BLOCK + jq - kk) * dilation).astype(F32)
    bias_c_ref[...] = jnp.where(kk <= jq, -(slope * dist_c), NEG)
    bias_p_ref[...] = jnp.where(kk >= jq, -(slope * dist_p), NEG)
    scale = HEAD_DIM ** -0.5
    contract_last = (((1,), (1,)), ((), ()))
    with_prev = blocks_per_seq > 1

    def block_rows(n):
        rows = pl.ds(pl.multiple_of(n * BLOCK, BLOCK), BLOCK)
        prev = pl.ds(pl.multiple_of(jnp.maximum(n - 1, 0) * BLOCK, BLOCK), BLOCK)
        return rows, prev

    def probabilities(n, carry):
        rows, prev = block_rows(n)
        q = q_ref[rows, :]
        s_c = lax.dot_general(q, k_ref[rows, :], contract_last, preferred_element_type=F32)
        s_c = s_c * scale + bias_c_ref[...]
        if with_prev:
            has_prev = (n % blocks_per_seq) != 0
            s_p = lax.dot_general(q, k_ref[prev, :], contract_last, preferred_element_type=F32)
            s_p = jnp.where(has_prev, s_p * scale + bias_p_ref[...], NEG)
            m = jnp.max(jnp.maximum(s_c, s_p), axis=-1, keepdims=True)
            p_c = jnp.exp(s_c - m)
            p_p = jnp.exp(s_p - m)
            den = jnp.sum(p_c + p_p, axis=-1, keepdims=True)
            pp_ref[rows, :] = p_p.astype(BF16)
        else:
            m = jnp.max(s_c, axis=-1, keepdims=True)
            p_c = jnp.exp(s_c - m)
            den = jnp.sum(p_c, axis=-1, keepdims=True)
        pc_ref[rows, :] = p_c.astype(BF16)
        den_ref[rows, :] = den
        lse_ref[rows, :] = jnp.broadcast_to(m + jnp.log(den), (BLOCK, LANES))
        return carry

    def values(n, carry):
        rows, prev = block_rows(n)
        o = jnp.dot(pc_ref[rows, :], v_ref[rows, :], preferred_element_type=F32)
        if with_prev:
            o = o + jnp.dot(pp_ref[rows, :], v_ref[prev, :], preferred_element_type=F32)
        o_ref[rows, :] = o / den_ref[rows, :]
        return carry

    n_blocks = q_ref.shape[0] // BLOCK
    lax.fori_loop(0, n_blocks, probabilities, 0, unroll=ATTN_UNROLL)
    lax.fori_loop(0, n_blocks, values, 0, unroll=ATTN_UNROLL)


def _attn_body(cast_part, base_ref, q1_ref, q4_ref, q16_ref, k_ref, v_ref, out_ref,
               stage_ref, tmp_ref, k4f_ref, v4f_ref,
               qp_ref, k4_ref, v4_ref, k16_ref, v16_ref,
               o1_ref, l1_ref, o4_ref, l4_ref, o16_ref, l16_ref, op_ref, lp_ref,
               bias_c_ref, bias_p_ref, pc_ref, pp_ref, den_ref):
    seq = k_ref.shape[0]
    base = base_ref[pl.program_id(1)]
    (_, d1), (_, d4), (_, d16) = DILATED_GROUPS
    cast_part(0, 1)

    stage_ref[...] = k_ref[...].astype(F32)
    _deinterleave(stage_ref, k4f_ref, seq)
    k4_ref[...] = k4f_ref[...].astype(BF16)
    _deinterleave(k4f_ref, k16_ref, seq // SUB)
    stage_ref[...] = v_ref[...].astype(F32)
    _deinterleave(stage_ref, v4f_ref, seq)
    v4_ref[...] = v4f_ref[...].astype(BF16)
    _deinterleave(v4f_ref, v16_ref, seq // SUB)

    _window_attention(q1_ref, k_ref, v_ref, base / d1, d1, seq // BLOCK,
                      bias_c_ref, bias_p_ref, pc_ref, pp_ref, den_ref, o1_ref, l1_ref)

    stage_ref[...] = q4_ref[...].astype(F32)
    _deinterleave(stage_ref, qp_ref, seq)
    _window_attention(qp_ref, k4_ref, v4_ref, base / d4, d4, seq // d4 // BLOCK,
                      bias_c_ref, bias_p_ref, pc_ref, pp_ref, den_ref, op_ref, lp_ref)
    _interleave(op_ref, o4_ref, seq)
    _interleave(lp_ref, l4_ref, seq)

    stage_ref[...] = q16_ref[...].astype(F32)
    _deinterleave(stage_ref, tmp_ref, seq)
    _deinterleave(tmp_ref, qp_ref, seq // SUB)
    _window_attention(qp_ref, k16_ref, v16_ref, base / d16, d16, seq // d16 // BLOCK,
                      bias_c_ref, bias_p_ref, pc_ref, pp_ref, den_ref, op_ref, lp_ref)
    _interleave(op_ref, tmp_ref, seq // SUB)
    _interleave(tmp_ref, o16_ref, seq)
    _interleave(lp_ref, tmp_ref, seq // SUB)
    _interleave(tmp_ref, l16_ref, seq)

    chunk = 2 * BLOCK

    def merge(c, carry):
        rows = pl.ds(pl.multiple_of(c * chunk, chunk), chunk)
        la, lb, lc = l1_ref[rows, :], l4_ref[rows, :], l16_ref[rows, :]
        mx = jnp.maximum(jnp.maximum(la, lb), lc)
        ea, eb, ec = jnp.exp(la - mx), jnp.exp(lb - mx), jnp.exp(lc - mx)
        den = ea + eb + ec
        o = (ea / den) * o1_ref[rows, :] + (eb / den) * o4_ref[rows, :] + (ec / den) * o16_ref[rows, :]
        out_ref[rows, :] = o.astype(out_ref.dtype)
        return carry
    lax.fori_loop(0, seq // chunk, merge, 0)


def _attention(base, q, kv, batch, seq, riders):
    for window, dilation in DILATED_GROUPS:
        assert window // dilation == BLOCK, "keys per query must span exactly one previous block"
    assert [d for _, d in DILATED_GROUPS] == [1, SUB, SUB * SUB]
    assert seq % (SUB * SUB * BLOCK) == 0
    h = N_KV_HEADS
    blk = (seq, HEAD_DIM)
    f32buf = pltpu.VMEM((seq, HEAD_DIM), F32)
    bf16buf = pltpu.VMEM((seq, HEAD_DIM), BF16)
    (out,), casts = _hosted_call(
        _attn_body,
        num_scalar_prefetch=1,
        grid=(batch, h),
        in_specs=[
            pl.BlockSpec(blk, lambda b, hh, base: (b, hh)),
            pl.BlockSpec(blk, lambda b, hh, base: (b, h + hh)),
            pl.BlockSpec(blk, lambda b, hh, base: (b, 2 * h + hh)),
            pl.BlockSpec(blk, lambda b, hh, base: (b, hh)),
            pl.BlockSpec(blk, lambda b, hh, base: (b, h + hh)),
        ],
        out_specs=[pl.BlockSpec(blk, lambda b, hh, base: (b, hh))],
        out_shape=[jax.ShapeDtypeStruct((batch * seq, D_MODEL), BF16)],
        scratch_shapes=[f32buf] * 4 + [bf16buf] * 5 + [f32buf] * 8
                       + [pltpu.VMEM((BLOCK, BLOCK), F32)] * 2
                       + [bf16buf] * 2 + [pltpu.VMEM((seq, 1), F32)],
        args=(base, q, q, q, kv, kv),
        riders=riders,
        semantics=("arbitrary", "arbitrary"),
        name="attn",
    )
    return out, casts


def _attn_out_body(o_ref, x_ref, g_ref, wo_ref, out_ref, acc_ref):
    acc_ref[...] = jnp.dot(o_ref[...], wo_ref[...], preferred_element_type=F32)
    _residual_norm_rows(x_ref, acc_ref, g_ref, out_ref)


def _attn_out(o, x, gain, w_o, *, tm=512):
    m, d = x.shape
    row_blk = pl.BlockSpec((tm, d), lambda i: (i, 0))
    return pl.pallas_call(
        _attn_out_body,
        grid=(m // tm,),
        in_specs=[row_blk, row_blk,
                  pl.BlockSpec((1, d), lambda i: (0, 0)),
                  pl.BlockSpec((d, d), lambda i: (0, 0))],
        out_specs=row_blk,
        out_shape=jax.ShapeDtypeStruct((m, d), F32),
        scratch_shapes=[pltpu.VMEM((tm, d), F32)],
        compiler_params=_params("parallel"),
        name="attn_out",
    )(o, x, gain.reshape(1, d), w_o)


def _route_body(x_ref, g_ref, wr_ref, o_ref, xn_ref):
    _norm_rows_into(x_ref, g_ref, xn_ref)
    logits = jnp.dot(xn_ref[...], wr_ref[...], preferred_element_type=F32,
                     precision=lax.Precision.HIGHEST)
    lane = lax.broadcasted_iota(jnp.int32, logits.shape, 1)
    logits = jnp.where(lane < N_EXPERTS, logits, -jnp.inf)
    m1 = jnp.max(logits, axis=-1, keepdims=True)
    i1 = jnp.min(jnp.where(logits == m1, lane, LANES), axis=-1, keepdims=True)
    rest = jnp.where(lane == i1, -jnp.inf, logits)
    m2 = jnp.max(rest, axis=-1, keepdims=True)
    i2 = jnp.min(jnp.where(rest == m2, lane, LANES), axis=-1, keepdims=True)
    t = jnp.exp(m2 - m1)
    den = 1.0 + t
    out = jnp.where(lane == 0, i1.astype(F32), 0.0)
    out = jnp.where(lane == 1, i2.astype(F32), out)
    out = jnp.where(lane == 2, 1.0 / den, out)
    out = jnp.where(lane == 3, t / den, out)
    o_ref[...] = out


def _moe_route(x, gain, w_router, *, tm=512):
    m, d = x.shape
    wr = jnp.zeros((d, LANES), F32).at[:, :N_EXPERTS].set(w_router)
    return pl.pallas_call(
        _route_body,
        grid=(m // tm,),
        in_specs=[pl.BlockSpec((tm, d), lambda i: (i, 0)),
                  pl.BlockSpec((1, d), lambda i: (0, 0)),
                  pl.BlockSpec((d, LANES), lambda i: (0, 0))],
        out_specs=[pl.BlockSpec((tm, LANES), lambda i: (i, 0)),
                   pl.BlockSpec((tm, d), lambda i: (i, 0))],
        out_shape=[jax.ShapeDtypeStruct((m, LANES), F32),
                   jax.ShapeDtypeStruct((m, d), F32)],
        compiler_params=_params("parallel"),
        name="moe_route",
    )(x, gain.reshape(1, d), wr)


def _row_copy(src_hbm, row, dst_ref, r, sem):
    return pltpu.make_async_copy(src_hbm.at[pl.ds(row, 1), :], dst_ref.at[pl.ds(r, 1), :], sem)


GATHER_UNROLL = 8
GATHER_QUEUE = 1
GATHER_SLICES = 12


def _start_row_gather(idx_ref, base, src_hbm, dst_ref, sem, priority, first=0, count=None):
    count = dst_ref.shape[0] if count is None else count

    def start(k, carry):
        r = first + k
        _row_copy(src_hbm, idx_ref[base + r], dst_ref, r, sem).start(priority=priority)
        return carry
    lax.fori_loop(0, count, start, 0, unroll=GATHER_UNROLL)


def _wait_row_gather(idx_ref, base, src_hbm, dst_ref, sem):
    def wait(r, carry):
        _row_copy(src_hbm, idx_ref[base + r], dst_ref, r, sem).wait()
        return carry
    lax.fori_loop(0, dst_ref.shape[0], wait, 0, unroll=GATHER_UNROLL)


def _experts_body(tok_ref, bexp_ref, rows_ref, nused_ref, hn_hbm, wg_ref, wu_ref, wd_ref,
                  ys_ref, xg_ref, xn_ref, wg16_ref, wu16_ref, wd16_ref, sem):
    i = pl.program_id(0)
    j = pl.program_id(1)
    n_used = nused_ref[0]
    used = i < n_used
    acc_ref = ys_ref

    @pl.when(j == 0)
    def _():
        acc_ref[...] = jnp.zeros_like(acc_ref)

    @pl.when(jnp.logical_and(used, j == 0))
    def _():
        @pl.when(i == 0)
        def _():
            _start_row_gather(tok_ref, 0, hn_hbm, xg_ref, sem, GATHER_QUEUE)
        _wait_row_gather(tok_ref, i * MOE_TM, hn_hbm, xg_ref, sem)
        xn_ref[...] = xg_ref[...].astype(BF16)

    per_step = MOE_TM // GATHER_SLICES
    assert GATHER_SLICES < D_FF // MOE_TF and per_step * GATHER_SLICES == MOE_TM

    @pl.when(jnp.logical_and(i + 1 < n_used, jnp.logical_and(j >= 1, j <= GATHER_SLICES)))
    def _():
        _start_row_gather(tok_ref, (i + 1) * MOE_TM, hn_hbm, xg_ref, sem, GATHER_QUEUE,
                          first=(j - 1) * per_step, count=per_step)

    n_rows = rows_ref[i]
    variants = [r for r in (128, 256, 512, 1024, 2048) if r < MOE_TM] + [MOE_TM]
    lower = 0
    for upper in variants:
        fits = jnp.logical_and(n_rows > lower, n_rows <= upper)

        @pl.when(jnp.logical_and(used, fits))
        def _(upper=upper):
            _swiglu_accumulate(xn_ref, _cast_once(wg_ref, wg16_ref), _cast_once(wu_ref, wu16_ref),
                               _cast_once(wd_ref, wd16_ref), acc_ref, n_rows=upper)
        lower = upper


def _moe_experts(tok_pad, blk_expert, blk_rows, n_used, hn, w_gu, w_down):
    d = D_MODEL
    n_blocks = tok_pad.shape[0] // MOE_TM
    nj = D_FF // MOE_TF

    def col(i, j, nused):
        return jnp.where(i < nused[0], j, nj - 1)

    return pl.pallas_call(
        _experts_body,
        grid_spec=pltpu.PrefetchScalarGridSpec(
            num_scalar_prefetch=4,
            grid=(n_blocks, nj),
            in_specs=[
                pl.BlockSpec(memory_space=pl.ANY),
                pl.BlockSpec((1, d, MOE_TF), lambda i, j, tok, be, br, nu: (be[i], 0, col(i, j, nu))),
                pl.BlockSpec((1, d, MOE_TF),
                             lambda i, j, tok, be, br, nu: (be[i], 0, nj + col(i, j, nu))),
                pl.BlockSpec((1, MOE_TF, d), lambda i, j, tok, be, br, nu: (be[i], col(i, j, nu), 0)),
            ],
            out_specs=pl.BlockSpec((MOE_TM, d), lambda i, j, tok, be, br, nu: (i, 0),
                                   pipeline_mode=pl.Buffered(1)),
            scratch_shapes=[pltpu.VMEM((MOE_TM, d), F32), pltpu.VMEM((MOE_TM, d), BF16),
                            pltpu.VMEM((d, MOE_TF), BF16), pltpu.VMEM((d, MOE_TF), BF16),
                            pltpu.VMEM((MOE_TF, d), BF16), pltpu.SemaphoreType.DMA],
        ),
        out_shape=jax.ShapeDtypeStruct((n_blocks * MOE_TM, d), F32),
        compiler_params=_params("arbitrary", "arbitrary"),
        name="moe_experts",
    )(tok_pad, blk_expert, blk_rows, n_used, hn, w_gu, w_gu, w_down)


def _combine_body(p0_ref, p1_ref, ys_hbm, route_ref, x_ref, g_ref, o_ref, a_ref, b_ref, sem):
    tm = x_ref.shape[0]
    base = pl.program_id(0) * tm
    _start_row_gather(p0_ref, base, ys_hbm, a_ref, sem.at[0], 0)
    _start_row_gather(p1_ref, base, ys_hbm, b_ref, sem.at[1], 1)
    _wait_row_gather(p0_ref, base, ys_hbm, a_ref, sem.at[0])
    _wait_row_gather(p1_ref, base, ys_hbm, b_ref, sem.at[1])
    gate0 = route_ref[:, TOP_K:TOP_K + 1]
    gate1 = route_ref[:, TOP_K + 1:TOP_K + 2]
    a_ref[...] = a_ref[...] * gate0 + b_ref[...] * gate1
    _residual_norm_rows(x_ref, a_ref, g_ref, o_ref)


def _moe_combine(pos0, pos1, ys, route, x, gain, *, tm=512):
    m, d = x.shape
    return pl.pallas_call(
        _combine_body,
        grid_spec=pltpu.PrefetchScalarGridSpec(
            num_scalar_prefetch=2,
            grid=(m // tm,),
            in_specs=[
                pl.BlockSpec(memory_space=pl.ANY),
                pl.BlockSpec((tm, LANES), lambda i, p0, p1: (i, 0)),
                pl.BlockSpec((tm, d), lambda i, p0, p1: (i, 0)),
                pl.BlockSpec((1, d), lambda i, p0, p1: (0, 0)),
            ],
            out_specs=pl.BlockSpec((tm, d), lambda i, p0, p1: (i, 0)),
            scratch_shapes=[pltpu.VMEM((tm, d), F32), pltpu.VMEM((tm, d), F32),
                            pltpu.SemaphoreType.DMA((2,))],
        ),
        out_shape=jax.ShapeDtypeStruct((m, d), F32),
        compiler_params=_params("arbitrary"),
        name="moe_combine",
    )(pos0, pos1, ys, route, x, gain.reshape(1, d))


def _dispatch_plan(route):
    n = route.shape[0]
    experts = route[:, :TOP_K].astype(jnp.int32).reshape(-1)
    onehot = (experts[:, None] == jnp.arange(N_EXPERTS)[None, :]).astype(jnp.int32)
    rank = jnp.take_along_axis(jnp.cumsum(onehot, axis=0) - onehot, experts[:, None], axis=1)[:, 0]
    counts = jnp.sum(onehot, axis=0)
    blocks_per_expert = (counts + MOE_TM - 1) // MOE_TM
    block_end = jnp.cumsum(blocks_per_expert)
    block_start = block_end - blocks_per_expert
    dest = block_start[experts] * MOE_TM + rank
    n_blocks = (n * TOP_K) // MOE_TM + N_EXPERTS
    tok_pad = jnp.zeros((n_blocks * MOE_TM,), jnp.int32).at[dest].set(jnp.arange(n * TOP_K) // TOP_K)
    blocks = jnp.arange(n_blocks)
    blk_expert = jnp.clip(jnp.searchsorted(block_end, blocks, side='right'),
                          0, N_EXPERTS - 1).astype(jnp.int32)
    blk_rows = jnp.clip(counts[blk_expert] - (blocks - block_start[blk_expert]) * MOE_TM,
                        0, MOE_TM).astype(jnp.int32)
    n_used = block_end[-1].astype(jnp.int32)
    blk_expert = jnp.where(blocks < n_used, blk_expert, blk_expert[jnp.maximum(n_used - 1, 0)])
    pos = dest.reshape(n, TOP_K)
    return tok_pad, blk_expert, blk_rows, n_used.reshape(1), pos[:, 0], pos[:, 1]


def kernel(x, norm_gains, a_w_in, a_b_in, a_norm_v, a_w_s, a_b_s, a_w_out, kv_norm, w_kv,
           b_w_q, b_w_o, ffn_w_gu, ffn_w_down, moe_router, moe_w_gu, moe_w_down):
    batch, seq, d = x.shape
    h = x.reshape(batch * seq, d)
    bf = lambda w: w.astype(BF16)

    g = norm_gains[0]
    (z, ssq), (w_out, w_gu, w_down) = _gmlp_in(
        h, g[0], bf(a_w_in[0]), a_b_in[0], [a_w_out[0], ffn_w_gu[0], ffn_w_down[0]])
    h, (w_kv16, w_q) = _gmlp_out(z, ssq, a_norm_v[0], a_w_s[0], a_b_s[0], h, g[1], w_out,
                                 [_whole_rider(w, 64) for w in (w_kv, b_w_q[0])])
    h, (w_o,) = _ffn_dense(h, g[2], w_gu, w_down, g[3], [_whole_rider(b_w_o[0], 56)])

    g = norm_gains[1]
    kv, _ = _norm_proj(h, kv_norm, w_kv16, [], tm=1024, tn=1024, name="proj_kv")
    q, _ = _norm_proj(h, g[0], w_q, [], tm=1024, tn=1024, name="proj_q")
    base = jnp.exp2(-8.0 * jnp.arange(1, N_KV_HEADS + 1, dtype=F32) / N_KV_HEADS)
    o, _ = _attention(base, q, kv, batch, seq, [])
    h = _attn_out(o, h, g[1], w_o)

    route, hn = _moe_route(h, g[2], moe_router[0])
    tok_pad, blk_expert, blk_rows, n_used, pos0, pos1 = _dispatch_plan(route)
    ys = _moe_experts(tok_pad, blk_expert, blk_rows, n_used, hn, moe_w_gu[0], moe_w_down[0])
    h = _moe_combine(pos0, pos1, ys, route, h, g[3])
    return h.reshape(batch, seq, d)
```

```python
import functools
from typing import NamedTuple

import jax
import jax.numpy as jnp
from jax import lax
from jax.experimental import pallas as pl
from jax.experimental.pallas import tpu as pltpu

F32 = jnp.float32
BF16 = jnp.bfloat16

D_MODEL = 2048
EPS = 1e-6
CHUNK = 128
GMLP_HALF = 2 * D_MODEL
GMLP_GROUPS = 8
GMLP_GROUP_CH = GMLP_HALF // GMLP_GROUPS
HEAD_DIM = 128
N_KV_HEADS = D_MODEL // HEAD_DIM
DILATED_GROUPS = ((128, 1), (512, 4), (2048, 16))
BLOCK = 128
NEG = -1e30
D_FF = 7168
N_EXPERTS = 8
TOP_K = 2

LANES = 128
VMEM_LIMIT_BYTES = 56 * 1024 * 1024

NORM_ROWS = 128
EPILOGUE_ROWS = 128
MOE_TM = 1152
MOE_TF = 512


def _params(*sem):
    return pltpu.CompilerParams(dimension_semantics=sem, vmem_limit_bytes=VMEM_LIMIT_BYTES)


def _rms_scale(x):
    return lax.rsqrt(jnp.mean(x * x, axis=-1, keepdims=True) + EPS)


def _norm_rows_into(x_ref, g_ref, out_ref):
    def body(c, carry):
        rows = pl.ds(pl.multiple_of(c * NORM_ROWS, NORM_ROWS), NORM_ROWS)
        x = x_ref[rows, :]
        out_ref[rows, :] = ((x * _rms_scale(x)) * g_ref[...]).astype(out_ref.dtype)
        return carry
    lax.fori_loop(0, x_ref.shape[0] // NORM_ROWS, body, 0)


def _residual_norm_rows(res_ref, acc_ref, g_ref, out_ref):
    def body(c, carry):
        rows = pl.ds(pl.multiple_of(c * NORM_ROWS, NORM_ROWS), NORM_ROWS)
        a = acc_ref[rows, :]
        out_ref[rows, :] = res_ref[rows, :] + (a * _rms_scale(a)) * g_ref[...]
        return carry
    lax.fori_loop(0, acc_ref.shape[0] // NORM_ROWS, body, 0)


BF16_SUBLANES = 16


class _Rider(NamedTuple):
    src: jax.Array
    steps: int
    rows: int


def _whole_rider(w, host_steps):
    total = w.shape[0]
    for steps in range(host_steps, 0, -1):
        if total % steps == 0 and (total // steps) % BF16_SUBLANES == 0:
            return _Rider(w, steps, total // steps)
    raise ValueError(f"no chunking of {w.shape} over {host_steps} steps")


def _riding_body(body, n_in, n_out, n_riders, *refs):
    ins, refs = refs[:n_in], refs[n_in:]
    r_in, refs = refs[:n_riders], refs[n_riders:]
    outs, refs = refs[:n_out], refs[n_out:]
    r_out, scratch = refs[:n_riders], refs[n_riders:]

    def cast_part(part, n_parts):
        for src, dst in zip(r_in, r_out):
            lane_tiles = src.shape[1] // LANES
            assert lane_tiles * LANES == src.shape[1] and lane_tiles >= n_parts
            cols = slice(part * lane_tiles // n_parts * LANES,
                         (part + 1) * lane_tiles // n_parts * LANES)
            dst[:, cols] = src[:, cols].astype(BF16)
    body(cast_part, *ins, *outs, *scratch)


def _hosted_call(body, *, grid, in_specs, out_specs, out_shape, scratch_shapes, args, riders,
                 semantics, name, num_scalar_prefetch=0):
    n_grid = len(grid)
    strides = [1] * n_grid
    for ax in range(n_grid - 2, -1, -1):
        strides[ax] = strides[ax + 1] * grid[ax + 1]
    assert all(r.steps <= strides[0] * grid[0] for r in riders)

    r_specs, r_shapes = [], []
    for r in riders:
        def index(*g, r=r):
            step = sum(g[ax] * strides[ax] for ax in range(n_grid))
            return jnp.minimum(step, r.steps - 1), 0
        r_specs.append(pl.BlockSpec((r.rows, r.src.shape[1]), index))
        r_shapes.append(jax.ShapeDtypeStruct(r.src.shape, BF16))

    n_in = num_scalar_prefetch + len(in_specs)
    n_out = len(out_specs)
    all_in = list(in_specs) + r_specs
    all_out = list(out_specs) + r_specs
    kernel_fn = functools.partial(_riding_body, body, n_in, n_out, len(riders))
    if num_scalar_prefetch:
        spec = dict(grid_spec=pltpu.PrefetchScalarGridSpec(
            num_scalar_prefetch=num_scalar_prefetch, grid=grid, in_specs=all_in,
            out_specs=all_out, scratch_shapes=scratch_shapes))
    else:
        spec = dict(grid=grid, in_specs=all_in, out_specs=all_out, scratch_shapes=scratch_shapes)
    outs = pl.pallas_call(
        kernel_fn,
        out_shape=list(out_shape) + r_shapes,
        compiler_params=_params(*semantics),
        name=name,
        **spec,
    )(*args, *[r.src for r in riders])
    return outs[:n_out], outs[n_out:]


def _gelu_tanh(x):
    cdf = 0.5 * (1.0 + jnp.tanh(0.7978845608028654 * (x + 0.044715 * (x * x * x))))
    return x * cdf


def _row_chunks(ref, chunk=EPILOGUE_ROWS):
    n = ref.shape[0] // chunk
    return [(c, n, slice(c * chunk, (c + 1) * chunk)) for c in range(n)]


def _proj_body(cast_part, x_ref, g_ref, w_ref, o_ref, xn_ref):
    @pl.when(pl.program_id(1) == 0)
    def _():
        _norm_rows_into(x_ref, g_ref, xn_ref)
    for c, n, rows in _row_chunks(xn_ref, 4 * EPILOGUE_ROWS):
        o_ref[rows, :] = jnp.dot(xn_ref[rows, :], w_ref[...],
                                 preferred_element_type=F32).astype(o_ref.dtype)
        cast_part(c, n)


def _norm_proj(x, gain, w, riders, *, tm, tn, name):
    m, k = x.shape
    n = w.shape[1]
    (out,), casts = _hosted_call(
        _proj_body,
        grid=(m // tm, n // tn),
        in_specs=[
            pl.BlockSpec((tm, k), lambda i, j: (i, 0)),
            pl.BlockSpec((1, k), lambda i, j: (0, 0)),
            pl.BlockSpec((k, tn), lambda i, j: (0, j)),
        ],
        out_specs=[pl.BlockSpec((tm, tn), lambda i, j: (i, j))],
        out_shape=[jax.ShapeDtypeStruct((m, n), BF16)],
        scratch_shapes=[pltpu.VMEM((tm, k), BF16)],
        args=(x, gain.reshape(1, k), w),
        riders=riders,
        semantics=("arbitrary", "arbitrary"),
        name=name,
    )
    return out, casts


def _gmlp_in_body(cast_part, x_ref, g_ref, w_ref, b_ref, z_ref, ssq_ref, xn_ref, ss_ref):
    j = pl.program_id(1)
    nj = pl.num_programs(1)

    @pl.when(j == 0)
    def _():
        _norm_rows_into(x_ref, g_ref, xn_ref)
        ss_ref[...] = jnp.zeros_like(ss_ref)

    is_v = (j >= nj // 2).astype(F32)
    for c, n, rows in _row_chunks(xn_ref, 2 * EPILOGUE_ROWS):
        z = jnp.dot(xn_ref[rows, :], w_ref[...], preferred_element_type=F32) + b_ref[...]
        z = _gelu_tanh(z)
        z_ref[rows, :] = z.astype(z_ref.dtype)
        ss_ref[rows, :] += is_v * jnp.sum(z * z, axis=-1, keepdims=True)
        cast_part(c, n)

    @pl.when(j == nj - 1)
    def _():
        ssq_ref[...] = jnp.broadcast_to(ss_ref[...], ssq_ref.shape)


def _gmlp_in(x, gain, w_in, b_in, later_weights, *, tm=2048, tn=512):
    m, k = x.shape
    n = w_in.shape[1]
    grid = (m // tm, n // tn)
    return _hosted_call(
        _gmlp_in_body,
        grid=grid,
        in_specs=[
            pl.BlockSpec((tm, k), lambda i, j: (i, 0), pipeline_mode=pl.Buffered(1)),
            pl.BlockSpec((1, k), lambda i, j: (0, 0)),
            pl.BlockSpec((k, tn), lambda i, j: (0, j)),
            pl.BlockSpec((1, tn), lambda i, j: (0, j)),
        ],
        out_specs=[
            pl.BlockSpec((tm, tn), lambda i, j: (i, j)),
            pl.BlockSpec((tm, LANES), lambda i, j: (i, 0)),
        ],
        out_shape=[
            jax.ShapeDtypeStruct((m, n), BF16),
            jax.ShapeDtypeStruct((m, LANES), F32),
        ],
        scratch_shapes=[pltpu.VMEM((tm, k), BF16), pltpu.VMEM((tm, 1), F32)],
        args=(x, gain.reshape(1, k), w_in, b_in.reshape(1, n)),
        riders=[_whole_rider(w, grid[0] * grid[1]) for w in later_weights],
        semantics=("arbitrary", "arbitrary"),
        name="gmlp_in",
    )


def _gmlp_out_body(cast_part, u_ref, v_ref, ssq_ref, gv_ref, ws_ref, bs_ref, x_ref, g_ref, wo_ref,
                   o_ref):
    grp = pl.program_id(1)
    tm = u_ref.shape[0]
    acc_ref = o_ref

    @pl.when(grp == 0)
    def _():
        acc_ref[...] = jnp.zeros_like(acc_ref)

    row = lax.broadcasted_iota(jnp.int32, (CHUNK, CHUNK), 0)
    col = lax.broadcasted_iota(jnp.int32, (CHUNK, CHUNK), 1)
    ws = jnp.where(row >= col, ws_ref[0], 0.0).astype(BF16)
    r = lax.rsqrt(ssq_ref[:, 0:1] * (1.0 / GMLP_HALF) + EPS)
    def project(rows, gated):
        acc_ref[rows, :] += jnp.dot(gated, wo_ref[...], preferred_element_type=F32)

    pending = None
    n = tm // CHUNK
    for c in range(n):
        rows = slice(c * CHUNK, (c + 1) * CHUNK)
        vn = (v_ref[rows, :].astype(F32) * r[rows, :]) * gv_ref[...]
        mixed = jnp.dot(ws, vn.astype(BF16), preferred_element_type=F32) + bs_ref[0]
        gated = (u_ref[rows, :].astype(F32) * mixed).astype(BF16)
        if pending is not None:
            project(*pending)
        pending = (rows, gated)
        cast_part(c, n)
    project(*pending)

    @pl.when(grp == pl.num_programs(1) - 1)
    def _():
        _residual_norm_rows(x_ref, acc_ref, g_ref, o_ref)


def _gmlp_out(z, ssq, norm_v, w_s, b_s, x, gain, w_out, riders, *, tm=1024):
    m, d = x.shape
    gc = GMLP_GROUP_CH
    (out,), casts = _hosted_call(
        _gmlp_out_body,
        grid=(m // tm, GMLP_GROUPS),
        in_specs=[
            pl.BlockSpec((tm, gc), lambda i, g: (i, g)),
            pl.BlockSpec((tm, gc), lambda i, g: (i, GMLP_GROUPS + g)),
            pl.BlockSpec((tm, LANES), lambda i, g: (i, 0)),
            pl.BlockSpec((1, gc), lambda i, g: (0, g)),
            pl.BlockSpec((1, CHUNK, CHUNK), lambda i, g: (g, 0, 0)),
            pl.BlockSpec((1, CHUNK, 1), lambda i, g: (g, 0, 0)),
            pl.BlockSpec((tm, d), lambda i, g: (i, 0)),
            pl.BlockSpec((1, d), lambda i, g: (0, 0)),
            pl.BlockSpec((gc, d), lambda i, g: (g, 0)),
        ],
        out_specs=[pl.BlockSpec((tm, d), lambda i, g: (i, 0))],
        out_shape=[jax.ShapeDtypeStruct((m, d), F32)],
        scratch_shapes=[],
        args=(z, z, ssq, norm_v.reshape(1, GMLP_HALF), w_s, b_s[:, :, None], x,
              gain.reshape(1, d), w_out),
        riders=riders,
        semantics=("arbitrary", "arbitrary"),
        name="gmlp_out",
    )
    return out, casts


SWIGLU_ROWS = 256


def _chunks_covering(n_rows):
    assert n_rows % EPILOGUE_ROWS == 0
    bounds = list(range(0, n_rows, SWIGLU_ROWS)) + [n_rows]
    return [slice(lo, hi) for lo, hi in zip(bounds[:-1], bounds[1:])]


def _swiglu_accumulate(xn_ref, wg, wu, wd, acc_ref, cast_part=None, n_rows=None):
    chunks = _chunks_covering(n_rows or xn_ref.shape[0])

    def down(rows, a):
        acc_ref[rows, :] += jnp.dot(a, wd(), preferred_element_type=F32)

    pending = None
    for c, rows in enumerate(chunks):
        x = xn_ref[rows, :]
        g = jnp.dot(x, wg(), preferred_element_type=F32)
        u = jnp.dot(x, wu(), preferred_element_type=F32)
        a = ((g * jax.nn.sigmoid(g)) * u).astype(BF16)
        if pending is not None:
            down(*pending)
        pending = (rows, a)
        if cast_part is not None:
            cast_part(c, len(chunks))
    down(*pending)


def _cast_once(src_ref, dst_ref):
    done = []

    def get():
        if not done:
            dst_ref[...] = src_ref[0].astype(BF16)
            done.append(True)
        return dst_ref[...]
    return get


def _ffn_body(cast_part, x_ref, g_in_ref, wg_ref, wu_ref, wd_ref, g_out_ref, o_ref, xn_ref):
    j = pl.program_id(1)
    acc_ref = o_ref

    @pl.when(j == 0)
    def _():
        _norm_rows_into(x_ref, g_in_ref, xn_ref)
        acc_ref[...] = jnp.zeros_like(acc_ref)

    _swiglu_accumulate(xn_ref, lambda: wg_ref[...], lambda: wu_ref[...], lambda: wd_ref[...],
                       acc_ref, cast_part)

    @pl.when(j == pl.num_programs(1) - 1)
    def _():
        _residual_norm_rows(x_ref, acc_ref, g_out_ref, o_ref)


def _ffn_dense(x, g_in, w_gu, w_down, g_out, riders, *, tm=1024, tf=1024):
    m, d = x.shape
    nj = D_FF // tf
    once = dict(pipeline_mode=pl.Buffered(1))
    (out,), casts = _hosted_call(
        _ffn_body,
        grid=(m // tm, nj),
        in_specs=[
            pl.BlockSpec((tm, d), lambda i, j: (i, 0), **once),
            pl.BlockSpec((1, d), lambda i, j: (0, 0)),
            pl.BlockSpec((d, tf), lambda i, j: (0, j)),
            pl.BlockSpec((d, tf), lambda i, j: (0, nj + j)),
            pl.BlockSpec((tf, d), lambda i, j: (j, 0)),
            pl.BlockSpec((1, d), lambda i, j: (0, 0)),
        ],
        out_specs=[pl.BlockSpec((tm, d), lambda i, j: (i, 0), **once)],
        out_shape=[jax.ShapeDtypeStruct((m, d), F32)],
        scratch_shapes=[pltpu.VMEM((tm, d), BF16)],
        args=(x, g_in.reshape(1, d), w_gu, w_gu, w_down, g_out.reshape(1, d)),
        riders=riders,
        semantics=("arbitrary", "arbitrary"),
        name="ffn_dense",
    )
    return out, casts


SUB = 4
ATTN_UNROLL = 8


def _deinterleave(src_ref, dst_ref, span):
    part = span // SUB
    for base in range(0, src_ref.shape[0], span):
        for r in range(SUB):
            dst_ref[base + r * part: base + (r + 1) * part, :] = (
                src_ref[pl.ds(base + r, part, stride=SUB), :].astype(dst_ref.dtype))


def _interleave(src_ref, dst_ref, span):
    part = span // SUB
    for base in range(0, src_ref.shape[0], span):
        for r in range(SUB):
            dst_ref[pl.ds(base + r, part, stride=SUB), :] = (
                src_ref[base + r * part: base + (r + 1) * part, :])


def _window_attention(q_ref, k_ref, v_ref, slope, dilation, blocks_per_seq,
                      bias_c_ref, bias_p_ref, pc_ref, pp_ref, den_ref, o_ref, lse_ref):
    jq = lax.broadcasted_iota(jnp.int32, (BLOCK, BLOCK), 0)
    kk = lax.broadcasted_iota(jnp.int32, (BLOCK, BLOCK), 1)
    dist_c = ((jq - kk) * dilation).astype(F32)
    dist_p = ((BLOCK + jq - kk) * dilation).astype(F32)
    bias_c_ref[...] = jnp.where(kk <= jq, -(slope * dist_c), NEG)
    bias_p_ref[...] = jnp.where(kk >= jq, -(slope * dist_p), NEG)
    scale = HEAD_DIM ** -0.5
    contract_last = (((1,), (1,)), ((), ()))
    with_prev = blocks_per_seq > 1

    def block_rows(n):
        rows = pl.ds(pl.multiple_of(n * BLOCK, BLOCK), BLOCK)
        prev = pl.ds(pl.multiple_of(jnp.maximum(n - 1, 0) * BLOCK, BLOCK), BLOCK)
        return rows, prev

    def probabilities(n, carry):
        rows, prev = block_rows(n)
        q = q_ref[rows, :]
        s_c = lax.dot_general(q, k_ref[rows, :], contract_last, preferred_element_type=F32)
        s_c = s_c * scale + bias_c_ref[...]
        if with_prev:
            has_prev = (n % blocks_per_seq) != 0
            s_p = lax.dot_general(q, k_ref[prev, :], contract_last, preferred_element_type=F32)
            s_p = jnp.where(has_prev, s_p * scale + bias_p_ref[...], NEG)
            m = jnp.max(jnp.maximum(s_c, s_p), axis=-1, keepdims=True)
            p_c = jnp.exp(s_c - m)
            p_p = jnp.exp(s_p - m)
            den = jnp.sum(p_c + p_p, axis=-1, keepdims=True)
            pp_ref[rows, :] = p_p.astype(BF16)
        else:
            m = jnp.max(s_c, axis=-1, keepdims=True)
            p_c = jnp.exp(s_c - m)
            den = jnp.sum(p_c, axis=-1, keepdims=True)
        pc_ref[rows, :] = p_c.astype(BF16)
        den_ref[rows, :] = den
        lse_ref[rows, :] = jnp.broadcast_to(m + jnp.log(den), (BLOCK, LANES))
        return carry

    def values(n, carry):
        rows, prev = block_rows(n)
        o = jnp.dot(pc_ref[rows, :], v_ref[rows, :], preferred_element_type=F32)
        if with_prev:
            o = o + jnp.dot(pp_ref[rows, :], v_ref[prev, :], preferred_element_type=F32)
        o_ref[rows, :] = o / den_ref[rows, :]
        return carry

    n_blocks = q_ref.shape[0] // BLOCK
    lax.fori_loop(0, n_blocks, probabilities, 0, unroll=ATTN_UNROLL)
    lax.fori_loop(0, n_blocks, values, 0, unroll=ATTN_UNROLL)


def _attn_body(cast_part, base_ref, q1_ref, q4_ref, q16_ref, k_ref, v_ref, out_ref,
               stage_ref, tmp_ref, k4f_ref, v4f_ref,
               qp_ref, k4_ref, v4_ref, k16_ref, v16_ref,
               o1_ref, l1_ref, o4_ref, l4_ref, o16_ref, l16_ref, op_ref, lp_ref,
               bias_c_ref, bias_p_ref, pc_ref, pp_ref, den_ref):
    seq = k_ref.shape[0]
    base = base_ref[pl.program_id(1)]
    (_, d1), (_, d4), (_, d16) = DILATED_GROUPS
    cast_part(0, 1)

    stage_ref[...] = k_ref[...].astype(F32)
    _deinterleave(stage_ref, k4f_ref, seq)
    k4_ref[...] = k4f_ref[...].astype(BF16)
    _deinterleave(k4f_ref, k16_ref, seq // SUB)
    stage_ref[...] = v_ref[...].astype(F32)
    _deinterleave(stage_ref, v4f_ref, seq)
    v4_ref[...] = v4f_ref[...].astype(BF16)
    _deinterleave(v4f_ref, v16_ref, seq // SUB)

    _window_attention(q1_ref, k_ref, v_ref, base / d1, d1, seq // BLOCK,
                      bias_c_ref, bias_p_ref, pc_ref, pp_ref, den_ref, o1_ref, l1_ref)

    stage_ref[...] = q4_ref[...].astype(F32)
    _deinterleave(stage_ref, qp_ref, seq)
    _window_attention(qp_ref, k4_ref, v4_ref, base / d4, d4, seq // d4 // BLOCK,
                      bias_c_ref, bias_p_ref, pc_ref, pp_ref, den_ref, op_ref, lp_ref)
    _interleave(op_ref, o4_ref, seq)
    _interleave(lp_ref, l4_ref, seq)

    stage_ref[...] = q16_ref[...].astype(F32)
    _deinterleave(stage_ref, tmp_ref, seq)
    _deinterleave(tmp_ref, qp_ref, seq // SUB)
    _window_attention(qp_ref, k16_ref, v16_ref, base / d16, d16, seq // d16 // BLOCK,
                      bias_c_ref, bias_p_ref, pc_ref, pp_ref, den_ref, op_ref, lp_ref)
    _interleave(op_ref, tmp_ref, seq // SUB)
    _interleave(tmp_ref, o16_ref, seq)
    _interleave(lp_ref, tmp_ref, seq // SUB)
    _interleave(tmp_ref, l16_ref, seq)

    chunk = 2 * BLOCK

    def merge(c, carry):
        rows = pl.ds(pl.multiple_of(c * chunk, chunk), chunk)
        la, lb, lc = l1_ref[rows, :], l4_ref[rows, :], l16_ref[rows, :]
        mx = jnp.maximum(jnp.maximum(la, lb), lc)
        ea, eb, ec = jnp.exp(la - mx), jnp.exp(lb - mx), jnp.exp(lc - mx)
        den = ea + eb + ec
        o = (ea / den) * o1_ref[rows, :] + (eb / den) * o4_ref[rows, :] + (ec / den) * o16_ref[rows, :]
        out_ref[rows, :] = o.astype(out_ref.dtype)
        return carry
    lax.fori_loop(0, seq // chunk, merge, 0)


def _attention(base, q, kv, batch, seq, riders):
    for window, dilation in DILATED_GROUPS:
        assert window // dilation == BLOCK, "keys per query must span exactly one previous block"
    assert [d for _, d in DILATED_GROUPS] == [1, SUB, SUB * SUB]
    assert seq % (SUB * SUB * BLOCK) == 0
    h = N_KV_HEADS
    blk = (seq, HEAD_DIM)
    f32buf = pltpu.VMEM((seq, HEAD_DIM), F32)
    bf16buf = pltpu.VMEM((seq, HEAD_DIM), BF16)
    (out,), casts = _hosted_call(
        _attn_body,
        num_scalar_prefetch=1,
        grid=(batch, h),
        in_specs=[
            pl.BlockSpec(blk, lambda b, hh, base: (b, hh)),
            pl.BlockSpec(blk, lambda b, hh, base: (b, h + hh)),
            pl.BlockSpec(blk, lambda b, hh, base: (b, 2 * h + hh)),
            pl.BlockSpec(blk, lambda b, hh, base: (b, hh)),
            pl.BlockSpec(blk, lambda b, hh, base: (b, h + hh)),
        ],
        out_specs=[pl.BlockSpec(blk, lambda b, hh, base: (b, hh))],
        out_shape=[jax.ShapeDtypeStruct((batch * seq, D_MODEL), BF16)],
        scratch_shapes=[f32buf] * 4 + [bf16buf] * 5 + [f32buf] * 8
                       + [pltpu.VMEM((BLOCK, BLOCK), F32)] * 2
                       + [bf16buf] * 2 + [pltpu.VMEM((seq, 1), F32)],
        args=(base, q, q, q, kv, kv),
        riders=riders,
        semantics=("arbitrary", "arbitrary"),
        name="attn",
    )
    return out, casts


def _attn_out_body(o_ref, x_ref, g_ref, wo_ref, out_ref, acc_ref):
    acc_ref[...] = jnp.dot(o_ref[...], wo_ref[...], preferred_element_type=F32)
    _residual_norm_rows(x_ref, acc_ref, g_ref, out_ref)


def _attn_out(o, x, gain, w_o, *, tm=512):
    m, d = x.shape
    row_blk = pl.BlockSpec((tm, d), lambda i: (i, 0))
    return pl.pallas_call(
        _attn_out_body,
        grid=(m // tm,),
        in_specs=[row_blk, row_blk,
                  pl.BlockSpec((1, d), lambda i: (0, 0)),
                  pl.BlockSpec((d, d), lambda i: (0, 0))],
        out_specs=row_blk,
        out_shape=jax.ShapeDtypeStruct((m, d), F32),
        scratch_shapes=[pltpu.VMEM((tm, d), F32)],
        compiler_params=_params("parallel"),
        name="attn_out",
    )(o, x, gain.reshape(1, d), w_o)


def _route_body(x_ref, g_ref, wr_ref, o_ref, xn_ref):
    _norm_rows_into(x_ref, g_ref, xn_ref)
    xn, w = xn_ref[...], wr_ref[...]
    x_hi, w_hi = xn.astype(BF16), w.astype(BF16)
    x_lo = (xn - x_hi.astype(F32)).astype(BF16)
    w_lo = (w - w_hi.astype(F32)).astype(BF16)
    logits = (jnp.dot(x_hi, w_hi, preferred_element_type=F32)
              + (jnp.dot(x_lo, w_hi, preferred_element_type=F32)
                 + jnp.dot(x_hi, w_lo, preferred_element_type=F32)))
    lane = lax.broadcasted_iota(jnp.int32, logits.shape, 1)
    logits = jnp.where(lane < N_EXPERTS, logits, -jnp.inf)
    m1 = jnp.max(logits, axis=-1, keepdims=True)
    i1 = jnp.min(jnp.where(logits == m1, lane, LANES), axis=-1, keepdims=True)
    rest = jnp.where(lane == i1, -jnp.inf, logits)
    m2 = jnp.max(rest, axis=-1, keepdims=True)
    i2 = jnp.min(jnp.where(rest == m2, lane, LANES), axis=-1, keepdims=True)
    t = jnp.exp(m2 - m1)
    den = 1.0 + t
    out = jnp.where(lane == 0, i1.astype(F32), 0.0)
    out = jnp.where(lane == 1, i2.astype(F32), out)
    out = jnp.where(lane == 2, 1.0 / den, out)
    out = jnp.where(lane == 3, t / den, out)
    o_ref[...] = out


def _moe_route(x, gain, w_router, *, tm=512):
    m, d = x.shape
    wr = jnp.zeros((d, LANES), F32).at[:, :N_EXPERTS].set(w_router)
    return pl.pallas_call(
        _route_body,
        grid=(m // tm,),
        in_specs=[pl.BlockSpec((tm, d), lambda i: (i, 0)),
                  pl.BlockSpec((1, d), lambda i: (0, 0)),
                  pl.BlockSpec((d, LANES), lambda i: (0, 0))],
        out_specs=[pl.BlockSpec((tm, LANES), lambda i: (i, 0)),
                   pl.BlockSpec((tm, d), lambda i: (i, 0))],
        out_shape=[jax.ShapeDtypeStruct((m, LANES), F32),
                   jax.ShapeDtypeStruct((m, d), F32)],
        compiler_params=_params("parallel"),
        name="moe_route",
    )(x, gain.reshape(1, d), wr)


def _row_copy(src_hbm, row, dst_ref, r, sem):
    return pltpu.make_async_copy(src_hbm.at[pl.ds(row, 1), :], dst_ref.at[pl.ds(r, 1), :], sem)


GATHER_UNROLL = 8
GATHER_QUEUE = 1
GATHER_SLICES = 12


def _start_row_gather(idx_ref, base, src_hbm, dst_ref, sem, priority, first=0, count=None):
    count = dst_ref.shape[0] if count is None else count

    def start(k, carry):
        r = first + k
        _row_copy(src_hbm, idx_ref[base + r], dst_ref, r, sem).start(priority=priority)
        return carry
    lax.fori_loop(0, count, start, 0, unroll=GATHER_UNROLL)


def _wait_row_gather(idx_ref, base, src_hbm, dst_ref, sem):
    def wait(r, carry):
        _row_copy(src_hbm, idx_ref[base + r], dst_ref, r, sem).wait()
        return carry
    lax.fori_loop(0, dst_ref.shape[0], wait, 0, unroll=GATHER_UNROLL)


def _experts_body(tok_ref, bexp_ref, rows_ref, nused_ref, hn_hbm, wg_ref, wu_ref, wd_ref,
                  ys_ref, xg_ref, xn_ref, wg16_ref, wu16_ref, wd16_ref, sem):
    i = pl.program_id(0)
    j = pl.program_id(1)
    n_used = nused_ref[0]
    used = i < n_used
    acc_ref = ys_ref

    @pl.when(j == 0)
    def _():
        acc_ref[...] = jnp.zeros_like(acc_ref)

    @pl.when(jnp.logical_and(used, j == 0))
    def _():
        @pl.when(i == 0)
        def _():
            _start_row_gather(tok_ref, 0, hn_hbm, xg_ref, sem, GATHER_QUEUE)
        _wait_row_gather(tok_ref, i * MOE_TM, hn_hbm, xg_ref, sem)
        xn_ref[...] = xg_ref[...].astype(BF16)

    per_step = MOE_TM // GATHER_SLICES
    assert GATHER_SLICES < D_FF // MOE_TF and per_step * GATHER_SLICES == MOE_TM

    @pl.when(jnp.logical_and(i + 1 < n_used, jnp.logical_and(j >= 1, j <= GATHER_SLICES)))
    def _():
        _start_row_gather(tok_ref, (i + 1) * MOE_TM, hn_hbm, xg_ref, sem, GATHER_QUEUE,
                          first=(j - 1) * per_step, count=per_step)

    n_rows = rows_ref[i]
    variants = [r for r in (128, 256, 512, 1024, 2048) if r < MOE_TM] + [MOE_TM]
    lower = 0
    for upper in variants:
        fits = jnp.logical_and(n_rows > lower, n_rows <= upper)

        @pl.when(jnp.logical_and(used, fits))
        def _(upper=upper):
            _swiglu_accumulate(xn_ref, _cast_once(wg_ref, wg16_ref), _cast_once(wu_ref, wu16_ref),
                               _cast_once(wd_ref, wd16_ref), acc_ref, n_rows=upper)
        lower = upper


def _moe_experts(tok_pad, blk_expert, blk_rows, n_used, hn, w_gu, w_down):
    d = D_MODEL
    n_blocks = tok_pad.shape[0] // MOE_TM
    nj = D_FF // MOE_TF

    def col(i, j, nused):
        return jnp.where(i < nused[0], j, nj - 1)

    return pl.pallas_call(
        _experts_body,
        grid_spec=pltpu.PrefetchScalarGridSpec(
            num_scalar_prefetch=4,
            grid=(n_blocks, nj),
            in_specs=[
                pl.BlockSpec(memory_space=pl.ANY),
                pl.BlockSpec((1, d, MOE_TF), lambda i, j, tok, be, br, nu: (be[i], 0, col(i, j, nu))),
                pl.BlockSpec((1, d, MOE_TF),
                             lambda i, j, tok, be, br, nu: (be[i], 0, nj + col(i, j, nu))),
                pl.BlockSpec((1, MOE_TF, d), lambda i, j, tok, be, br, nu: (be[i], col(i, j, nu), 0)),
            ],
            out_specs=pl.BlockSpec((MOE_TM, d), lambda i, j, tok, be, br, nu: (i, 0),
                                   pipeline_mode=pl.Buffered(1)),
            scratch_shapes=[pltpu.VMEM((MOE_TM, d), F32), pltpu.VMEM((MOE_TM, d), BF16),
                            pltpu.VMEM((d, MOE_TF), BF16), pltpu.VMEM((d, MOE_TF), BF16),
                            pltpu.VMEM((MOE_TF, d), BF16), pltpu.SemaphoreType.DMA],
        ),
        out_shape=jax.ShapeDtypeStruct((n_blocks * MOE_TM, d), F32),
        compiler_params=_params("arbitrary", "arbitrary"),
        name="moe_experts",
    )(tok_pad, blk_expert, blk_rows, n_used, hn, w_gu, w_gu, w_down)


def _combine_body(p0_ref, p1_ref, ys_hbm, route_ref, x_ref, g_ref, o_ref, a_ref, b_ref, sem):
    tm = x_ref.shape[0]
    base = pl.program_id(0) * tm
    _start_row_gather(p0_ref, base, ys_hbm, a_ref, sem.at[0], 0)
    _start_row_gather(p1_ref, base, ys_hbm, b_ref, sem.at[1], 1)
    _wait_row_gather(p0_ref, base, ys_hbm, a_ref, sem.at[0])
    _wait_row_gather(p1_ref, base, ys_hbm, b_ref, sem.at[1])
    gate0 = route_ref[:, TOP_K:TOP_K + 1]
    gate1 = route_ref[:, TOP_K + 1:TOP_K + 2]
    a_ref[...] = a_ref[...] * gate0 + b_ref[...] * gate1
    _residual_norm_rows(x_ref, a_ref, g_ref, o_ref)


def _moe_combine(pos0, pos1, ys, route, x, gain, *, tm=512):
    m, d = x.shape
    return pl.pallas_call(
        _combine_body,
        grid_spec=pltpu.PrefetchScalarGridSpec(
            num_scalar_prefetch=2,
            grid=(m // tm,),
            in_specs=[
                pl.BlockSpec(memory_space=pl.ANY),
                pl.BlockSpec((tm, LANES), lambda i, p0, p1: (i, 0)),
                pl.BlockSpec((tm, d), lambda i, p0, p1: (i, 0)),
                pl.BlockSpec((1, d), lambda i, p0, p1: (0, 0)),
            ],
            out_specs=pl.BlockSpec((tm, d), lambda i, p0, p1: (i, 0)),
            scratch_shapes=[pltpu.VMEM((tm, d), F32), pltpu.VMEM((tm, d), F32),
                            pltpu.SemaphoreType.DMA((2,))],
        ),
        out_shape=jax.ShapeDtypeStruct((m, d), F32),
        compiler_params=_params("arbitrary"),
        name="moe_combine",
    )(pos0, pos1, ys, route, x, gain.reshape(1, d))


def _dispatch_plan(route):
    n = route.shape[0]
    experts = route[:, :TOP_K].astype(jnp.int32).reshape(-1)
    onehot = (experts[:, None] == jnp.arange(N_EXPERTS)[None, :]).astype(jnp.int32)
    rank = jnp.take_along_axis(jnp.cumsum(onehot, axis=0) - onehot, experts[:, None], axis=1)[:, 0]
    counts = jnp.sum(onehot, axis=0)
    blocks_per_expert = (counts + MOE_TM - 1) // MOE_TM
    block_end = jnp.cumsum(blocks_per_expert)
    block_start = block_end - blocks_per_expert
    dest = block_start[experts] * MOE_TM + rank
    n_blocks = (n * TOP_K) // MOE_TM + N_EXPERTS
    tok_pad = jnp.zeros((n_blocks * MOE_TM,), jnp.int32).at[dest].set(jnp.arange(n * TOP_K) // TOP_K)
    blocks = jnp.arange(n_blocks)
    blk_expert = jnp.clip(jnp.searchsorted(block_end, blocks, side='right'),
                          0, N_EXPERTS - 1).astype(jnp.int32)
    blk_rows = jnp.clip(counts[blk_expert] - (blocks - block_start[blk_expert]) * MOE_TM,
                        0, MOE_TM).astype(jnp.int32)
    n_used = block_end[-1].astype(jnp.int32)
    blk_expert = jnp.where(blocks < n_used, blk_expert, blk_expert[jnp.maximum(n_used - 1, 0)])
    pos = dest.reshape(n, TOP_K)
    return tok_pad, blk_expert, blk_rows, n_used.reshape(1), pos[:, 0], pos[:, 1]


def kernel(x, norm_gains, a_w_in, a_b_in, a_norm_v, a_w_s, a_b_s, a_w_out, kv_norm, w_kv,
           b_w_q, b_w_o, ffn_w_gu, ffn_w_down, moe_router, moe_w_gu, moe_w_down):
    batch, seq, d = x.shape
    h = x.reshape(batch * seq, d)
    bf = lambda w: w.astype(BF16)

    g = norm_gains[0]
    (z, ssq), (w_out, w_gu, w_down) = _gmlp_in(
        h, g[0], bf(a_w_in[0]), a_b_in[0], [a_w_out[0], ffn_w_gu[0], ffn_w_down[0]])
    h, (w_kv16, w_q) = _gmlp_out(z, ssq, a_norm_v[0], a_w_s[0], a_b_s[0], h, g[1], w_out,
                                 [_whole_rider(w, 64) for w in (w_kv, b_w_q[0])])
    h, (w_o,) = _ffn_dense(h, g[2], w_gu, w_down, g[3], [_whole_rider(b_w_o[0], 56)])

    g = norm_gains[1]
    kv, _ = _norm_proj(h, kv_norm, w_kv16, [], tm=1024, tn=1024, name="proj_kv")
    q, _ = _norm_proj(h, g[0], w_q, [], tm=1024, tn=1024, name="proj_q")
    base = jnp.exp2(-8.0 * jnp.arange(1, N_KV_HEADS + 1, dtype=F32) / N_KV_HEADS)
    o, _ = _attention(base, q, kv, batch, seq, [])
    h = _attn_out(o, h, g[1], w_o)

    route, hn = _moe_route(h, g[2], moe_router[0])
    tok_pad, blk_expert, blk_rows, n_used, pos0, pos1 = _dispatch_plan(route)
    ys = _moe_experts(tok_pad, blk_expert, blk_rows, n_used, hn, moe_w_gu[0], moe_w_down[0])
    h = _moe_combine(pos0, pos1, ys, route, h, g[3])
    return h.reshape(batch, seq, d)
```
